```python
import jax, jax.numpy as jnp
from jax import lax
import numpy as np

D_MODEL = 1024
BATCH = 8
SEQ = 4096
DEPTH = 2

GRID_W = 64
CTX_LEN = 256
HEAD_DIM = 64
A_HEADS = 8
A_KV_HEADS = 2
A_GROUP = A_HEADS // A_KV_HEADS
A_BLOCK = 128
ROPE_THETA = 10000.0
B_HEADS = 8
NA_ROWS = 8
NA_COLS = 16
C_GROUPS = 4
C_GROUP_W = 128
BRANCH_W = 512
N_BRANCH = 3
D_FF = -(-8 * D_MODEL // (3 * 256)) * 256
IN_SIZES = (A_HEADS * HEAD_DIM, A_KV_HEADS * HEAD_DIM, A_KV_HEADS * HEAD_DIM,
            B_HEADS * HEAD_DIM, B_HEADS * HEAD_DIM, B_HEADS * HEAD_DIM,
            C_GROUPS * C_GROUP_W, N_BRANCH * D_MODEL)
IN_SPLITS = tuple(sum(IN_SIZES[:i + 1]) for i in range(len(IN_SIZES) - 1))
D_IN = sum(IN_SIZES)
DEEPNORM_ALPHA = (2.0 * DEPTH) ** 0.25
DEEPNORM_BETA = (8.0 * DEPTH) ** -0.25
LN_EPS = 1e-6
RMS_EPS = 1e-6

kernel_name = 'hybrid_gqa_natten_fnet_dit_block'


def _layernorm(x, g=None, b=None):
    xf = x.astype(jnp.float32)
    mu = xf.mean(-1, keepdims=True)
    var = jnp.square(xf - mu).mean(-1, keepdims=True)
    y = (xf - mu) * lax.rsqrt(var + LN_EPS)
    if g is not None:
        y = y * g.astype(jnp.float32) + b.astype(jnp.float32)
    return y.astype(x.dtype)


def _rmsnorm(x, g):
    xf = x.astype(jnp.float32)
    y = xf * lax.rsqrt(jnp.mean(jnp.square(xf), -1, keepdims=True) + RMS_EPS)
    return (y * g.astype(jnp.float32)).astype(x.dtype)


def _modulate(h, shift, scale):
    return h * (1.0 + scale) + shift


def _rope_2d(x, rows, cols):
    quarter = HEAD_DIM // 4
    half = HEAD_DIM // 2
    freqs = ROPE_THETA ** (-jnp.arange(quarter, dtype=jnp.float32) / quarter)

    def rot(xa, pos):
        ang = pos.astype(jnp.float32)[:, None] * freqs
        cos = jnp.cos(ang)[None, :, None, :].astype(x.dtype)
        sin = jnp.sin(ang)[None, :, None, :].astype(x.dtype)
        x1, x2 = xa[..., :quarter], xa[..., quarter:]
        return jnp.concatenate([x1 * cos - x2 * sin, x1 * sin + x2 * cos], axis=-1)

    return jnp.concatenate([rot(x[..., :half], rows), rot(x[..., half:], cols)], axis=-1)


def _gqa(q, k, v):
    s = jnp.einsum('bqkgd,bskd->bkgqs', q, k).astype(jnp.float32) * (HEAD_DIM ** -0.5)
    p = jax.nn.softmax(s, axis=-1).astype(v.dtype)
    return jnp.einsum('bkgqs,bskd->bqkgd', p, v)


def _mha(q, k, v):
    s = jnp.einsum('bqhd,bshd->bhqs', q, k).astype(jnp.float32) * (HEAD_DIM ** -0.5)
    p = jax.nn.softmax(s, axis=-1).astype(v.dtype)
    return jnp.einsum('bhqs,bshd->bqhd', p, v)


def _neighbourhood_attn(q, k, v, kc, vc, rpb, rows_n):
    B, L, H, dh = q.shape
    kr = min(NA_ROWS, rows_n)
    kcn = NA_COLS
    cols = jnp.arange(GRID_W)
    col_start = jnp.clip(cols - kcn // 2, 0, GRID_W - kcn)
    col_idx = col_start[:, None] + jnp.arange(kcn)[None, :]
    dc = col_idx - cols[:, None] + (NA_COLS - 1)
    k_grid = k.reshape(B, rows_n, GRID_W, H, dh)
    v_grid = v.reshape(B, rows_n, GRID_W, H, dh)
    q_rows = q.reshape(B, rows_n, GRID_W, H, dh).transpose(1, 0, 2, 3, 4)
    scale = HEAD_DIM ** -0.5

    def row_fn(args):
        r, q_r = args
        r0 = jnp.clip(r - kr // 2, 0, rows_n - kr)
        k_band = lax.dynamic_slice_in_dim(k_grid, r0, kr, axis=1)
        v_band = lax.dynamic_slice_in_dim(v_grid, r0, kr, axis=1)
        k_win = k_band[:, :, col_idx]
        v_win = v_band[:, :, col_idx]
        s_nb = jnp.einsum('bqhd,brqjhd->bhqrj', q_r, k_win).astype(jnp.float32) * scale
        dr = r0 + jnp.arange(kr) - r + (NA_ROWS - 1)
        bias = rpb[:, dr][:, :, dc].transpose(0, 2, 1, 3)
        s_nb = s_nb + bias[None].astype(jnp.float32)
        s_ctx = jnp.einsum('bqhd,bshd->bhqs', q_r, kc).astype(jnp.float32) * scale
        s = jnp.concatenate([s_nb.reshape(B, H, GRID_W, kr * kcn), s_ctx], axis=-1)
        p = jax.nn.softmax(s, axis=-1).astype(v.dtype)
        p_nb = p[..., :kr * kcn].reshape(B, H, GRID_W, kr, kcn)
        p_ctx = p[..., kr * kcn:]
        return (jnp.einsum('bhqrj,brqjhd->bqhd', p_nb, v_win)
                + jnp.einsum('bhqs,bshd->bqhd', p_ctx, vc))

    out = lax.map(row_fn, (jnp.arange(rows_n), q_rows))
    return out.transpose(1, 0, 2, 3, 4).reshape(B, L, H * dh)


def _fourier_mix(u):
    B, L, _ = u.shape
    uf = u.astype(jnp.float32).reshape(B, L, C_GROUPS, C_GROUP_W)
    y = jnp.fft.fft2(uf, axes=(1, 3), norm='ortho').real
    return y.reshape(B, L, C_GROUPS * C_GROUP_W).astype(u.dtype)


def _merge(gz, oa, ob, oc, w_branch, w_out):
    ga, gb, gf = jnp.split(jax.nn.sigmoid(gz), N_BRANCH, axis=-1)
    y = ga * (oa @ w_branch[0]) + gb * (ob @ w_branch[1]) + gf * (oc @ w_branch[2])
    return y @ w_out


def _token_mixers(h, hc, w_in, q_norm, k_norm, rpb, w_branch, w_out, rows, cols, rows_n, with_ctx_out):
    B, L, _ = h.shape
    Lc = hc.shape[1]
    qa, ka, va, qb, kb, vb, uc, gz = jnp.split(h @ w_in, IN_SPLITS, axis=-1)
    qac, kac, vac, qbc, kbc, vbc, ucc, gzc = jnp.split(hc @ w_in, IN_SPLITS, axis=-1)

    qa = _rope_2d(_rmsnorm(qa.reshape(B, L, A_HEADS, HEAD_DIM), q_norm), rows, cols)
    ka = _rope_2d(_rmsnorm(ka.reshape(B, L, A_KV_HEADS, HEAD_DIM), k_norm), rows, cols)
    va = va.reshape(B, L, A_KV_HEADS, HEAD_DIM)
    kac = _rmsnorm(kac.reshape(B, Lc, A_KV_HEADS, HEAD_DIM), k_norm)
    vac = vac.reshape(B, Lc, A_KV_HEADS, HEAD_DIM)
    k_all = jnp.concatenate([kac, ka], axis=1)
    v_all = jnp.concatenate([vac, va], axis=1)
    n_blk = L // A_BLOCK
    q_blocks = qa.reshape(B, n_blk, A_BLOCK, A_KV_HEADS, A_GROUP, HEAD_DIM).transpose(1, 0, 2, 3, 4, 5)
    oa = lax.map(lambda qq: _gqa(qq, k_all, v_all), q_blocks)
    oa = oa.transpose(1, 0, 2, 3, 4, 5).reshape(B, L, BRANCH_W)

    kbc = kbc.reshape(B, Lc, B_HEADS, HEAD_DIM)
    vbc = vbc.reshape(B, Lc, B_HEADS, HEAD_DIM)
    ob = _neighbourhood_attn(qb.reshape(B, L, B_HEADS, HEAD_DIM), kb.reshape(B, L, B_HEADS, HEAD_DIM),
                             vb.reshape(B, L, B_HEADS, HEAD_DIM), kbc, vbc, rpb, rows_n)

    oc = _fourier_mix(uc)

    y = _merge(gz, oa, ob, oc, w_branch, w_out)
    if not with_ctx_out:
        return y, None

    qac = _rmsnorm(qac.reshape(B, Lc, A_HEADS, HEAD_DIM), q_norm).reshape(B, Lc, A_KV_HEADS, A_GROUP, HEAD_DIM)
    oac = _gqa(qac, kac, vac).reshape(B, Lc, BRANCH_W)
    obc = _mha(qbc.reshape(B, Lc, B_HEADS, HEAD_DIM), kbc, vbc).reshape(B, Lc, BRANCH_W)
    occ = _fourier_mix(ucc)
    yc = _merge(gzc, oac, obc, occ, w_branch, w_out)
    return y, yc


def _swiglu(h, w_gu, w_down):
    g, u = jnp.split(h @ w_gu, 2, axis=-1)
    return (jax.nn.silu(g) * u) @ w_down


def setup_inputs(seed: int = 0) -> dict:
    key = jax.random.key(seed)
    ks = jax.random.split(key, 18)

    def nrm(k, shape, s):
        return jax.random.normal(k, shape, jnp.float32) * s

    D = D_MODEL
    return {
        'x': nrm(ks[0], (BATCH, SEQ, D), 1.0),
        'c': nrm(ks[1], (BATCH, D), 1.0),
        'ctx': nrm(ks[2], (BATCH, CTX_LEN, D), 1.0),
        'c_ctx': nrm(ks[3], (D,), 1.0),
        'w_ada': nrm(ks[4], (DEPTH, D, 6 * D), 0.5 * D ** -0.5),
        'b_ada': nrm(ks[5], (DEPTH, 6 * D), 0.02),
        'w_in': nrm(ks[6], (DEPTH, D, D_IN), D ** -0.5),
        'q_norm': 1.0 + nrm(ks[7], (DEPTH, HEAD_DIM), 0.02),
        'k_norm': 1.0 + nrm(ks[8], (DEPTH, HEAD_DIM), 0.02),
        'rpb': nrm(ks[9], (DEPTH, B_HEADS, 2 * NA_ROWS - 1, 2 * NA_COLS - 1), 0.2),
        'w_branch': nrm(ks[10], (DEPTH, N_BRANCH, BRANCH_W, D), BRANCH_W ** -0.5),
        'w_out': nrm(ks[11], (DEPTH, D, D), D ** -0.5 * DEEPNORM_BETA),
        'ln1_g': 1.0 + nrm(ks[12], (DEPTH, D), 0.02),
        'ln1_b': nrm(ks[13], (DEPTH, D), 0.02),
        'w_gu': nrm(ks[14], (DEPTH, D, 2 * D_FF), D ** -0.5),
        'w_down': nrm(ks[15], (DEPTH, D_FF, D), D_FF ** -0.5 * DEEPNORM_BETA),
        'ln2_g': 1.0 + nrm(ks[16], (DEPTH, D), 0.02),
        'ln2_b': nrm(ks[17], (DEPTH, D), 0.02),
    }


def reference(x, c, ctx, c_ctx, w_ada, b_ada, w_in, q_norm, k_norm, rpb, w_branch, w_out,
              ln1_g, ln1_b, w_gu, w_down, ln2_g, ln2_b):
    L = x.shape[1]
    rows_n = L // GRID_W
    pos = jnp.arange(L, dtype=jnp.int32)
    rows = pos // GRID_W
    cols = pos % GRID_W
    xc = ctx
    c_act = jax.nn.silu(c)
    cc_act = jax.nn.silu(c_ctx)
    for l in range(DEPTH):
        with_ctx = l < DEPTH - 1
        mods = jnp.split(c_act @ w_ada[l] + b_ada[l], 6, axis=-1)
        sh1, sc1, g1, sh2, sc2, g2 = [t[:, None, :] for t in mods]
        shc1, scc1, gc1, shc2, scc2, gc2 = jnp.split(cc_act @ w_ada[l] + b_ada[l], 6, axis=-1)

        h = _modulate(_layernorm(x), sh1, sc1)
        hc = _modulate(_layernorm(xc), shc1, scc1)
        y, yc = _token_mixers(h, hc, w_in[l], q_norm[l], k_norm[l], rpb[l], w_branch[l], w_out[l],
                              rows, cols, rows_n, with_ctx)
        x = _layernorm(DEEPNORM_ALPHA * x + g1 * y, ln1_g[l], ln1_b[l])

        f = _swiglu(_modulate(_layernorm(x), sh2, sc2), w_gu[l], w_down[l])
        x = _layernorm(DEEPNORM_ALPHA * x + g2 * f, ln2_g[l], ln2_b[l])

        if with_ctx:
            xc = _layernorm(DEEPNORM_ALPHA * xc + gc1 * yc, ln1_g[l], ln1_b[l])
            fc = _swiglu(_modulate(_layernorm(xc), shc2, scc2), w_gu[l], w_down[l])
            xc = _layernorm(DEEPNORM_ALPHA * xc + gc2 * fc, ln2_g[l], ln2_b[l])
    return x
```

```python
import functools
import math

import numpy as np
import jax
import jax.numpy as jnp
from jax import lax
from jax.experimental import pallas as pl
from jax.experimental.pallas import tpu as pltpu

F32 = jnp.float32
BF16 = jnp.bfloat16

D_MODEL = 1024
GRID_W = 64
HEAD_DIM = 64
A_HEADS = 8
A_KV_HEADS = 2
A_GROUP = A_HEADS // A_KV_HEADS
B_HEADS = 8
NA_ROWS = 8
NA_COLS = 16
C_GROUPS = 4
C_GROUP_W = 128
BRANCH_W = 512
N_BRANCH = 3
D_FF = 2816
ROPE_THETA = 10000.0
LN_EPS = 1e-6
RMS_EPS = 1e-6
N_QKVU = 2816
Q_ROWS = 8
WIN_ROWS = 16
B_UNROLL = 2
NEG = -1e30
LOG2E = math.log2(math.e)
Q_SCALE = HEAD_DIM ** -0.5 * LOG2E
LANES = 128
VMEM_LIMIT = 56 * 1024 * 1024


def _cparams(*sem):
    return pltpu.CompilerParams(dimension_semantics=sem, vmem_limit_bytes=VMEM_LIMIT)


def _dot(a, b):
    return jnp.dot(a, b, preferred_element_type=F32)


def _dot_nt(a, b):
    return lax.dot_general(a, b, (((1,), (1,)), ((), ())), preferred_element_type=F32)


def _dot_tn(a, b):
    return lax.dot_general(a, b, (((0,), (0,)), ((), ())), preferred_element_type=F32)


def _layernorm(x):
    mu = jnp.mean(x, axis=-1, keepdims=True)
    xc = x - mu
    var = jnp.mean(xc * xc, axis=-1, keepdims=True)
    return xc * lax.rsqrt(var + LN_EPS)


def _ada_kernel(c_ref, w_ref, b_ref, o_ref):
    c = c_ref[...]
    a = c * (1.0 / (1.0 + jnp.exp(-c)))
    a_hi = a.astype(BF16)
    a_lo = (a - a_hi.astype(F32)).astype(BF16)
    w = w_ref[...]
    w_hi = w.astype(BF16)
    w_lo = (w - w_hi.astype(F32)).astype(BF16)
    o_ref[...] = _dot(a_hi, w_hi) + _dot(a_hi, w_lo) + _dot(a_lo, w_hi) + b_ref[...]


def _ada(cc, w_ada, b_ada):
    depth, d, n = w_ada.shape
    tn = 1536
    return pl.pallas_call(
        _ada_kernel,
        grid=(depth, n // tn),
        in_specs=[
            pl.BlockSpec((cc.shape[0], d), lambda l, j: (0, 0)),
            pl.BlockSpec((None, d, tn), lambda l, j: (l, 0, j)),
            pl.BlockSpec((None, 1, tn), lambda l, j: (l, 0, j)),
        ],
        out_specs=pl.BlockSpec((None, cc.shape[0], tn), lambda l, j: (l, 0, j)),
        out_shape=jax.ShapeDtypeStruct((depth, cc.shape[0], n), F32),
        compiler_params=_cparams("arbitrary", "arbitrary"),
        name="ada",
    )(cc, w_ada, b_ada.reshape(depth, 1, n))


def _swap16(x, lane_lo):
    up = pltpu.roll(x, 16, 1)
    dn = pltpu.roll(x, LANES - 16, 1)
    return jnp.where(lane_lo, dn, up)


def _with_ones(v):
    ones = jnp.ones((v.shape[0], HEAD_DIM), F32)
    parts = []
    for hd in range(v.shape[1] // HEAD_DIM):
        parts += [v[:, hd * HEAD_DIM:(hd + 1) * HEAD_DIM], ones]
    return jnp.concatenate(parts, axis=1).astype(BF16)


def _inproj_kernel(x_ref, mod_ref, w_ref, qg_ref, kg_ref, cos_ref, sin_ref, g512_ref, g128_ref, dft_ref,
                   qa_ref, ka_ref, va_ref, qb_ref, kb_ref, vb_ref, uc_ref, us_ref):
    mod = mod_ref[...]
    sh = mod[:, 0:D_MODEL]
    sc = mod[:, D_MODEL:2 * D_MODEL]
    h = (_layernorm(x_ref[...]) * (1.0 + sc) + sh).astype(BF16)

    cos = cos_ref[...]
    sin = sin_ref[...]
    lane_lo = (lax.broadcasted_iota(jnp.int32, cos.shape, 1) % 32) < 16

    def norm_rope(v, g_mat, gain):
        ms = _dot((v * v).astype(BF16), g_mat)
        vn = v * lax.rsqrt(ms + RMS_EPS) * gain
        outs = []
        for j in range(v.shape[1] // LANES):
            t = vn[:, j * LANES:(j + 1) * LANES]
            outs.append(t * cos + _swap16(t, lane_lo) * sin)
        return outs[0] if len(outs) == 1 else jnp.concatenate(outs, axis=1)

    q = _dot(h, w_ref[:, 0:512])
    qa_ref[...] = norm_rope(q, g512_ref[...], qg_ref[...]).astype(BF16)
    kv = _dot(h, w_ref[:, 512:768])
    ka_ref[...] = norm_rope(kv[:, 0:128], g128_ref[...], kg_ref[...]).astype(BF16)
    va_ref[...] = _with_ones(kv[:, 128:256])
    qb_ref[...] = (_dot(h, w_ref[:, 768:1280]) * Q_SCALE).astype(BF16)
    kb_ref[...] = _dot(h, w_ref[:, 1280:1792]).astype(BF16)
    vb_ref[...] = _with_ones(_dot(h, w_ref[:, 1792:2304]))
    u = _dot(h, w_ref[:, 2304:2816]).astype(BF16)
    dft = dft_ref[...]
    for g in range(C_GROUPS):
        z = _dot(u[:, g * C_GROUP_W:(g + 1) * C_GROUP_W], dft)
        uc_ref[:, g * C_GROUP_W:(g + 1) * C_GROUP_W] = z[:, 0:C_GROUP_W].astype(BF16)
        us_ref[:, g * C_GROUP_W:(g + 1) * C_GROUP_W] = z[:, C_GROUP_W:2 * C_GROUP_W].astype(BF16)


def _inproj(x2, mod3, w_qkvu, q_gain, k_gain, cos_t, sin_t, g512, g128, dft_c, *, batch, seq, tm):
    rows = batch * seq
    nl = seq // tm
    tok = lambda w: pl.BlockSpec((tm, w), lambda i: (i, 0))
    full = lambda a: pl.BlockSpec(a.shape, lambda i: (0,) * a.ndim)
    fo = pl.BlockSpec((tm, 512), lambda i: (i % nl, i // nl))
    o_tok = lambda w: jax.ShapeDtypeStruct((rows, w), BF16)
    return pl.pallas_call(
        _inproj_kernel,
        grid=(rows // tm,),
        in_specs=[
            tok(D_MODEL),
            pl.BlockSpec((None, 1, mod3.shape[2]), lambda i: (i // nl, 0, 0)),
            full(w_qkvu), full(q_gain), full(k_gain),
            pl.BlockSpec((tm, LANES), lambda i: (i % nl, 0)),
            pl.BlockSpec((tm, LANES), lambda i: (i % nl, 0)),
            full(g512), full(g128), full(dft_c),
        ],
        out_specs=[tok(512), tok(128), tok(256), tok(512), tok(512), tok(1024), fo, fo],
        out_shape=[o_tok(512), o_tok(128), o_tok(256), o_tok(512), o_tok(512), o_tok(1024),
                   jax.ShapeDtypeStruct((seq, batch * 512), BF16),
                   jax.ShapeDtypeStruct((seq, batch * 512), BF16)],
        compiler_params=_cparams("arbitrary"),
        name="inproj",
    )(x2, mod3, w_qkvu, q_gain, k_gain, cos_t, sin_t, g512, g128, dft_c)


def _attend_t_multi(qps, chunk_lists):
    n = len(qps)
    m = [None] * n
    acc = [None] * n
    units = [(i, c) for c in range(len(chunk_lists[0])) for i in range(n)]

    def scores(u):
        i, c = u
        k, _, bias_t = chunk_lists[i][c]
        s = _dot_nt(k, qps[i])
        return s if bias_t is None else s + bias_t

    s_next = scores(units[0])
    for idx, (i, c) in enumerate(units):
        s = s_next
        if idx + 1 < len(units):
            s_next = scores(units[idx + 1])
        v1 = chunk_lists[i][c][1]
        mc = jnp.max(s, axis=0, keepdims=True)
        if m[i] is None:
            m[i] = mc
            acc[i] = _dot_tn(v1, jnp.exp2(s - mc).astype(BF16))
        else:
            m_new = jnp.maximum(m[i], mc)
            acc[i] = jnp.exp2(m[i] - m_new) * acc[i] + _dot_tn(v1, jnp.exp2(s - m_new).astype(BF16))
            m[i] = m_new
    return [a[0:HEAD_DIM, :] / a[HEAD_DIM:HEAD_DIM + 1, :] for a in acc]


def _place_head(q128, src_half, dst_half, lane_hi):
    x = q128.astype(F32)
    if src_half != dst_half:
        x = pltpu.roll(x, 64, 1)
    keep = lane_hi if dst_half == 1 else jnp.logical_not(lane_hi)
    return jnp.where(keep, x, 0.0).astype(BF16)


def _gqa_tile_t(q, chunks_fn, tq, lane_hi):
    qps = []
    for g in range(A_KV_HEADS):
        parts = []
        for j in range(A_GROUP):
            hd = A_GROUP * g + j
            parts.append(_place_head(q[:, LANES * (hd // 2):LANES * (hd // 2 + 1)], hd % 2, g, lane_hi))
        qps.append(jnp.concatenate(parts, axis=0))
    outs = _attend_t_multi(qps, [chunks_fn(g) for g in range(A_KV_HEADS)])
    heads_t = [o_t[:, j * tq:(j + 1) * tq] for o_t in outs for j in range(A_GROUP)]
    return jnp.concatenate(heads_t, axis=0).T


def _mha_pair_t(q128, chunks_fn, lane_hi):
    outs = _attend_t_multi([_place_head(q128, hh, hh, lane_hi) for hh in range(2)],
                           [chunks_fn(hh) for hh in range(2)])
    return jnp.concatenate(outs, axis=0).T


def _attn_a_kernel(q_ref, k_ref, v_ref, kc_ref, vc_ref, o_ref, *, tq, chunk_bounds):
    lane_hi = lax.broadcasted_iota(jnp.int32, (tq, LANES), 1) >= 64

    def chunks(g):
        sl = slice(g * LANES, (g + 1) * LANES)
        return ([(kc_ref[...], vc_ref[:, sl], None)]
                + [(k_ref[s:e, :], v_ref[s:e, sl], None) for s, e in chunk_bounds])

    o_ref[...] = _gqa_tile_t(q_ref[...], chunks, tq, lane_hi).astype(BF16)


def _attn_a(qa, ka, va1, kac, vac1, *, batch, seq, ctx_len, tq, tk):
    bounds = [(i, min(i + tk, seq)) for i in range(0, seq, tk)]
    nq = seq // tq
    return pl.pallas_call(
        functools.partial(_attn_a_kernel, tq=tq, chunk_bounds=bounds),
        grid=(batch, nq),
        in_specs=[pl.BlockSpec((tq, 512), lambda b, i: (b * nq + i, 0)),
                  pl.BlockSpec((seq, LANES), lambda b, i: (b, 0)),
                  pl.BlockSpec((seq, 2 * LANES), lambda b, i: (b, 0)),
                  pl.BlockSpec((ctx_len, LANES), lambda b, i: (b, 0)),
                  pl.BlockSpec((ctx_len, 2 * LANES), lambda b, i: (b, 0))],
        out_specs=pl.BlockSpec((tq, 512), lambda b, i: (b * nq + i, 0)),
        out_shape=jax.ShapeDtypeStruct((batch * seq, 512), BF16),
        compiler_params=_cparams("arbitrary", "arbitrary"),
        name="attn_a",
    )(qa, ka, va1, kac, vac1)


def _attn_b_kernel(q_ref, k_ref, v_ref, kc_ref, vc_ref, bias_ref, o_ref, *, rows_n, tk):
    nq = Q_ROWS * GRID_W
    nw = WIN_ROWS * GRID_W
    n_groups = rows_n // Q_ROWS
    lane_hi = lax.broadcasted_iota(jnp.int32, (nq, LANES), 1) >= 64
    kc = kc_ref[...]

    def body(t, carry):
        qps, chunk_lists, q0s = [], [], []
        for u in range(B_UNROLL):
            g = t * B_UNROLL + u
            w0 = jnp.clip(Q_ROWS * g - (WIN_ROWS - Q_ROWS) // 2, 0, rows_n - WIN_ROWS)
            var = jnp.where(g == 0, 0, jnp.where(g == n_groups - 1, 2, 1))
            q0 = pl.multiple_of(g * nq, nq)
            k0 = pl.multiple_of(w0 * GRID_W, GRID_W)
            q128 = q_ref[pl.ds(q0, nq), :]
            q0s.append(q0)
            for hh in range(2):
                sl = slice(hh * LANES, (hh + 1) * LANES)
                qps.append(_place_head(q128, hh, hh, lane_hi))
                chunk_lists.append(
                    [(k_ref[pl.ds(k0 + s, tk), :], v_ref[pl.ds(k0 + s, tk), sl], bias_ref[var, hh, s:s + tk, :])
                     for s in range(0, nw, tk)] + [(kc, vc_ref[:, sl], None)])
        outs = _attend_t_multi(qps, chunk_lists)
        for u in range(B_UNROLL):
            o_ref[pl.ds(q0s[u], nq), :] = jnp.concatenate(outs[2 * u:2 * u + 2], axis=0).T.astype(BF16)
        return carry

    lax.fori_loop(0, n_groups // B_UNROLL, body, 0)


def _attn_b(qb, kb, vb1, kbc, vbc1, bias_t, *, batch, seq, ctx_len, tk):
    rows_n = seq // GRID_W
    nq = Q_ROWS * GRID_W
    nw = WIN_ROWS * GRID_W
    lat = lambda w: pl.BlockSpec((seq, w), lambda hp, b: (b, hp))
    ctx = lambda w: pl.BlockSpec((ctx_len, w), lambda hp, b: (b, hp))
    return pl.pallas_call(
        functools.partial(_attn_b_kernel, rows_n=rows_n, tk=tk),
        grid=(B_HEADS // 2, batch),
        in_specs=[lat(LANES), lat(LANES), lat(2 * LANES), ctx(LANES), ctx(2 * LANES),
                  pl.BlockSpec((3, 2, nw, nq), lambda hp, b: (0, hp, 0, 0))],
        out_specs=lat(LANES),
        out_shape=jax.ShapeDtypeStruct((batch * seq, 512), BF16),
        compiler_params=_cparams("arbitrary", "arbitrary"),
        name="attn_b",
    )(qb, kb, vb1, kbc, vbc1, bias_t)


def _neighbourhood_bias_t(rpb, rows_n):
    kc = NA_COLS
    cols = np.arange(GRID_W)
    c0 = np.clip(cols - kc // 2, 0, GRID_W - kc)
    col_ok = (cols[None, :] >= c0[:, None]) & (cols[None, :] < c0[:, None] + kc)
    off = GRID_W - NA_COLS
    rp = jnp.pad(rpb * LOG2E, ((0, 0), (0, 0), (off, off)))
    t = jnp.stack([rp[:, :, GRID_W - 1 - c:2 * GRID_W - 1 - c] for c in range(GRID_W)], axis=3)
    t = jnp.where(col_ok.T[None, None], t, NEG)
    neg = jnp.full((rpb.shape[0], GRID_W, GRID_W), NEG, F32)
    n_groups = rows_n // Q_ROWS
    variants = []
    for g in (0, min(1, n_groups - 1), n_groups - 1):
        w0 = int(np.clip(Q_ROWS * g - (WIN_ROWS - Q_ROWS) // 2, 0, rows_n - WIN_ROWS))
        k_blocks = []
        for i in range(WIN_ROWS):
            kr = w0 + i
            q_blocks = []
            for a in range(Q_ROWS):
                r = Q_ROWS * g + a
                r0 = int(np.clip(r - NA_ROWS // 2, 0, rows_n - NA_ROWS))
                q_blocks.append(t[:, kr - r + NA_ROWS - 1] if r0 <= kr < r0 + NA_ROWS else neg)
            k_blocks.append(jnp.concatenate(q_blocks, axis=2))
        variants.append(jnp.concatenate(k_blocks, axis=1))
    return jnp.stack(variants, axis=0)


def _attn_ctx_kernel(qa_ref, ka_ref, va_ref, qb_ref, kb_ref, vb_ref, oa_ref, ob_ref, *, lc):
    lane_hi = lax.broadcasted_iota(jnp.int32, (lc, LANES), 1) >= 64
    ka = ka_ref[...]
    oa_ref[...] = _gqa_tile_t(
        qa_ref[...], lambda g: [(ka, va_ref[:, g * LANES:(g + 1) * LANES], None)], lc, lane_hi).astype(BF16)
    for hp in range(B_HEADS // 2):
        sl = slice(hp * LANES, (hp + 1) * LANES)
        k = kb_ref[:, sl]
        chunks = lambda hh: [(k, vb_ref[:, (2 * hp + hh) * LANES:(2 * hp + hh + 1) * LANES], None)]
        ob_ref[:, sl] = _mha_pair_t(qb_ref[:, sl], chunks, lane_hi).astype(BF16)


def _attn_ctx(qac, kac, vac1, qbc, kbc, vbc1, *, batch, ctx_len):
    t = lambda w: pl.BlockSpec((ctx_len, w), lambda b: (b, 0))
    o = jax.ShapeDtypeStruct((batch * ctx_len, 512), BF16)
    return pl.pallas_call(
        functools.partial(_attn_ctx_kernel, lc=ctx_len),
        grid=(batch,),
        in_specs=[t(512), t(128), t(256), t(512), t(512), t(1024)],
        out_specs=[t(512), t(512)],
        out_shape=[o, o],
        compiler_params=_cparams("arbitrary"),
        name="attn_ctx",
    )(qac, kac, vac1, qbc, kbc, vbc1)


def _dft_kernel(ac_ref, as_ref, bc_ref, bs_ref, uc_ref, us_ref, o_ref, acc_ref, *, bm):
    i = pl.program_id(0)

    @pl.when(pl.program_id(2) == 0)
    def _():
        acc_ref[...] = jnp.zeros_like(acc_ref)

    bc = bc_ref[...]
    bs = bs_ref[...]
    c_rows, s_rows = [], []
    for r in range(bm // GRID_W):
        a_c = ac_ref[pl.ds(i * (bm // GRID_W) + r, 1), :]
        a_s = as_ref[pl.ds(i * (bm // GRID_W) + r, 1), :]
        c_rows.append((a_c * bc - a_s * bs).astype(BF16))
        s_rows.append((a_s * bc + a_c * bs).astype(BF16))
    c_t = jnp.concatenate(c_rows, axis=0)
    s_t = jnp.concatenate(s_rows, axis=0)
    acc_ref[...] += _dot(c_t, uc_ref[...]) - _dot(s_t, us_ref[...])

    @pl.when(pl.program_id(2) == pl.num_programs(2) - 1)
    def _():
        o_ref[...] = acc_ref[...].astype(BF16)


def _dft(tabs, uc, us):
    a_c, a_s, b_c, b_s = tabs
    n = uc.shape[0]
    width = uc.shape[1]
    bm = min(n, 1024)
    bn = min(width, 1024)
    bk = min(n, 1024)
    a_spec = pl.BlockSpec((a_c.shape[0], bk), lambda i, j, k: (0, k))
    b_spec = pl.BlockSpec((GRID_W, bk), lambda i, j, k: (0, k))
    u_spec = pl.BlockSpec((bk, bn), lambda i, j, k: (k, j))
    return pl.pallas_call(
        functools.partial(_dft_kernel, bm=bm),
        grid=(n // bm, width // bn, n // bk),
        in_specs=[a_spec, a_spec, b_spec, b_spec, u_spec, u_spec],
        out_specs=pl.BlockSpec((bm, bn), lambda i, j, k: (i, j)),
        out_shape=jax.ShapeDtypeStruct((n, width), BF16),
        scratch_shapes=[pltpu.VMEM((bm, bn), F32)],
        compiler_params=_cparams("arbitrary", "arbitrary", "arbitrary"),
        name="dft",
    )(a_c, a_s, b_c, b_s, uc, us)


def _dft_factor_tables(n):
    r = n // GRID_W
    k = lax.iota(jnp.int32, n)
    n1 = lax.iota(jnp.int32, r)
    n2 = lax.iota(jnp.int32, GRID_W)
    ang_a = ((n1[:, None] * k[None, :]) % r).astype(F32) * F32(2.0 * np.pi / r)
    ang_b = ((n2[:, None] * k[None, :]) % n).astype(F32) * F32(2.0 * np.pi / n)
    pad = ((0, (-r) % 8), (0, 0))
    scale = F32(n ** -0.5)
    return (jnp.pad(jnp.cos(ang_a), pad), jnp.pad(jnp.sin(ang_a), pad),
            jnp.cos(ang_b) * scale, jnp.sin(ang_b) * scale)


def _merge_kernel(x_ref, mod_ref, oa_ref, ob_ref, oc_ref, wg_ref, wb_ref, wo_ref, g_ref, b_ref, o_ref, *, alpha):
    x = x_ref[...]
    mod = mod_ref[...]
    sh = mod[:, 0:D_MODEL]
    sc = mod[:, D_MODEL:2 * D_MODEL]
    gate = mod[:, 2 * D_MODEL:3 * D_MODEL]
    h = (_layernorm(x) * (1.0 + sc) + sh).astype(BF16)
    y = None
    for i, br_ref in enumerate((oa_ref, ob_ref, oc_ref)):
        gz = _dot(h, wg_ref[:, i * D_MODEL:(i + 1) * D_MODEL])
        t = _dot(br_ref[...], wb_ref[i]) * (1.0 / (1.0 + jnp.exp(-gz)))
        y = t if y is None else y + t
    yo = _dot(y.astype(BF16), wo_ref[...])
    o_ref[...] = _layernorm(alpha * x + gate * yo) * g_ref[...] + b_ref[...]


def _merge(x2, mod3, oa, ob, oc, w_gate, w_branch, w_out, ln_g, ln_b, *, batch, seq, tm, alpha):
    rows = batch * seq
    nl = seq // tm
    tok = lambda w: pl.BlockSpec((tm, w), lambda i: (i, 0))
    full = lambda a: pl.BlockSpec(a.shape, lambda i: (0,) * a.ndim)
    return pl.pallas_call(
        functools.partial(_merge_kernel, alpha=alpha),
        grid=(rows // tm,),
        in_specs=[
            tok(D_MODEL),
            pl.BlockSpec((None, 1, mod3.shape[2]), lambda i: (i // nl, 0, 0)),
            tok(512), tok(512),
            pl.BlockSpec((tm, 512), lambda i: (i % nl, i // nl)),
            full(w_gate), full(w_branch), full(w_out), full(ln_g), full(ln_b),
        ],
        out_specs=tok(D_MODEL),
        out_shape=jax.ShapeDtypeStruct((rows, D_MODEL), F32),
        compiler_params=_cparams("arbitrary"),
        name="merge",
    )(x2, mod3, oa, ob, oc, w_gate, w_branch, w_out, ln_g, ln_b)


FF_CHUNKS = ((0, 768), (768, 1536), (1536, 2304), (2304, 2816))


def _ffn_kernel(x_ref, mod_ref, wgu_ref, wd_ref, g_ref, b_ref, o_ref, *, alpha):
    x = x_ref[...]
    mod = mod_ref[...]
    sh = mod[:, 3 * D_MODEL:4 * D_MODEL]
    sc = mod[:, 4 * D_MODEL:5 * D_MODEL]
    gate = mod[:, 5 * D_MODEL:6 * D_MODEL]
    h = (_layernorm(x) * (1.0 + sc) + sh).astype(BF16)
    f = None
    for s, e in FF_CHUNKS:
        g = _dot(h, wgu_ref[:, s:e])
        u = _dot(h, wgu_ref[:, D_FF + s:D_FF + e])
        a = (g * (1.0 / (1.0 + jnp.exp(-g))) * u).astype(BF16)
        t = _dot(a, wd_ref[s:e, :])
        f = t if f is None else f + t
    o_ref[...] = _layernorm(alpha * x + gate * f) * g_ref[...] + b_ref[...]


def _ffn(x2, mod3, w_gu, w_down, ln_g, ln_b, *, batch, seq, tm, alpha):
    rows = batch * seq
    nl = seq // tm
    tok = pl.BlockSpec((tm, D_MODEL), lambda i: (i, 0))
    full = lambda a: pl.BlockSpec(a.shape, lambda i: (0,) * a.ndim)
    return pl.pallas_call(
        functools.partial(_ffn_kernel, alpha=alpha),
        grid=(rows // tm,),
        in_specs=[tok, pl.BlockSpec((None, 1, mod3.shape[2]), lambda i: (i // nl, 0, 0)),
                  full(w_gu), full(w_down), full(ln_g), full(ln_b)],
        out_specs=tok,
        out_shape=jax.ShapeDtypeStruct((rows, D_MODEL), F32),
        compiler_params=_cparams("arbitrary"),
        name="ffn",
    )(x2, mod3, w_gu, w_down, ln_g, ln_b)


def _rope_tables(seq):
    quarter = HEAD_DIM // 4
    pos = jnp.arange(seq, dtype=jnp.int32)
    rows = (pos // GRID_W).astype(F32)
    cols = (pos % GRID_W).astype(F32)
    freqs = ROPE_THETA ** (-jnp.arange(quarter, dtype=F32) / quarter)
    ar = rows[:, None] * freqs
    ac = cols[:, None] * freqs
    cos64 = jnp.concatenate([jnp.cos(ar), jnp.cos(ar), jnp.cos(ac), jnp.cos(ac)], axis=1)
    sin64 = jnp.concatenate([-jnp.sin(ar), jnp.sin(ar), -jnp.sin(ac), jnp.sin(ac)], axis=1)
    return jnp.tile(cos64, (1, 2)), jnp.tile(sin64, (1, 2))


def _head_mean_matrix(width):
    idx = np.arange(width) // HEAD_DIM
    return jnp.asarray((idx[:, None] == idx[None, :]).astype(np.float32) / HEAD_DIM).astype(BF16)


def _channel_dft_matrix():
    idx = np.arange(C_GROUP_W)
    ang = (2.0 * np.pi / C_GROUP_W) * ((idx[:, None] * idx[None, :]) % C_GROUP_W)
    scale = C_GROUP_W ** -0.5
    mat = np.concatenate([np.cos(ang), np.sin(ang)], axis=1) * scale
    return jnp.asarray(mat, dtype=F32).astype(BF16)


def kernel(x, c, ctx, c_ctx, w_ada, b_ada, w_in, q_norm, k_norm, rpb, w_branch, w_out,
           ln1_g, ln1_b, w_gu, w_down, ln2_g, ln2_b):
    batch, seq, d = x.shape
    ctx_len = ctx.shape[1]
    depth = w_ada.shape[0]
    assert d == D_MODEL and seq % (B_UNROLL * Q_ROWS * GRID_W) == 0 and seq // GRID_W >= WIN_ROWS
    assert ctx_len % LANES == 0
    alpha = (2.0 * depth) ** 0.25
    tm = 512
    tmc = min(ctx_len, 512)

    pad = (-(batch + 1)) % 8
    cc = jnp.concatenate([c, c_ctx[None, :], jnp.zeros((pad, d), F32)], axis=0)
    mods = _ada(cc, w_ada, b_ada)

    cos_t, sin_t = _rope_tables(seq)
    cos_c = jnp.ones((ctx_len, LANES), F32)
    sin_c = jnp.zeros((ctx_len, LANES), F32)
    g512 = _head_mean_matrix(512)
    g128 = _head_mean_matrix(128)
    dft_c = _channel_dft_matrix()
    tabs_l = _dft_factor_tables(seq)
    tabs_c = _dft_factor_tables(ctx_len)

    xl = x.reshape(batch * seq, d)
    xc = ctx.reshape(batch * ctx_len, d)
    for l in range(depth):
        with_ctx = l < depth - 1
        w_qkvu = w_in[l, :, :N_QKVU].astype(BF16)
        w_gate = w_in[l, :, N_QKVU:].astype(BF16)
        wb = w_branch[l].astype(BF16)
        wo = w_out[l].astype(BF16)
        wgu = w_gu[l].astype(BF16)
        wd = w_down[l].astype(BF16)
        q_gain = jnp.tile(q_norm[l] * Q_SCALE, A_HEADS)[None, :]
        k_gain = jnp.tile(k_norm[l], A_KV_HEADS)[None, :]
        mod_l = mods[l, :batch][:, None, :]
        mod_c = jnp.broadcast_to(mods[l, batch][None, None, :], (batch, 1, 6 * d))
        ln1 = (ln1_g[l][None, :], ln1_b[l][None, :])
        ln2 = (ln2_g[l][None, :], ln2_b[l][None, :])

        qa, ka, va1, qb, kb, vb1, uc, us = _inproj(
            xl, mod_l, w_qkvu, q_gain, k_gain, cos_t, sin_t, g512, g128, dft_c, batch=batch, seq=seq, tm=tm)
        qac, kac, vac1, qbc, kbc, vbc1, ucc, usc = _inproj(
            xc, mod_c, w_qkvu, q_gain, k_gain, cos_c, sin_c, g512, g128, dft_c, batch=batch, seq=ctx_len, tm=tmc)

        oa = _attn_a(qa, ka, va1, kac, vac1, batch=batch, seq=seq, ctx_len=ctx_len, tq=256, tk=256)
        bias_t = _neighbourhood_bias_t(rpb[l], seq // GRID_W)
        ob = _attn_b(qb, kb, vb1, kbc, vbc1, bias_t, batch=batch, seq=seq, ctx_len=ctx_len, tk=512)
        oc = _dft(tabs_l, uc, us)

        x1 = _merge(xl, mod_l, oa, ob, oc, w_gate, wb, wo, *ln1, batch=batch, seq=seq, tm=tm, alpha=alpha)
        xl = _ffn(x1, mod_l, wgu, wd, *ln2, batch=batch, seq=seq, tm=tm, alpha=alpha)

        if with_ctx:
            oac, obc = _attn_ctx(qac, kac, vac1, qbc, kbc, vbc1, batch=batch, ctx_len=ctx_len)
            occ = _dft(tabs_c, ucc, usc)
            xc1 = _merge(xc, mod_c, oac, obc, occ, w_gate, wb, wo, *ln1,
                         batch=batch, seq=ctx_len, tm=tmc, alpha=alpha)
            xc = _ffn(xc1, mod_c, wgu, wd, *ln2, batch=batch, seq=ctx_len, tm=tmc, alpha=alpha)
    return xl.reshape(batch, seq, d)
```

```python
import functools
import math

import numpy as np
import jax
import jax.numpy as jnp
from jax import lax
from jax.experimental import pallas as pl
from jax.experimental.pallas import tpu as pltpu

F32 = jnp.float32
BF16 = jnp.bfloat16

D_MODEL = 1024
GRID_W = 64
HEAD_DIM = 64
A_HEADS = 8
A_KV_HEADS = 2
A_GROUP = A_HEADS // A_KV_HEADS
B_HEADS = 8
NA_ROWS = 8
NA_COLS = 16
C_GROUPS = 4
C_GROUP_W = 128
BRANCH_W = 512
N_BRANCH = 3
D_FF = 2816
ROPE_THETA = 10000.0
LN_EPS = 1e-6
RMS_EPS = 1e-6
N_QKVU = 2816
Q_ROWS = 8
WIN_ROWS = 16
B_UNROLL = 2
NEG = -1e30
LOG2E = math.log2(math.e)
Q_SCALE = HEAD_DIM ** -0.5 * LOG2E
LANES = 128
VMEM_LIMIT = 56 * 1024 * 1024


def _cparams(*sem):
    return pltpu.CompilerParams(dimension_semantics=sem, vmem_limit_bytes=VMEM_LIMIT)


def _resident_spec(a):
    return pl.BlockSpec(a.shape, lambda *_: (0,) * a.ndim, pipeline_mode=pl.Buffered(1))


def _dot(a, b):
    return jnp.dot(a, b, preferred_element_type=F32)


def _dot_nt(a, b):
    return lax.dot_general(a, b, (((1,), (1,)), ((), ())), preferred_element_type=F32)


def _dot_tn(a, b):
    return lax.dot_general(a, b, (((0,), (0,)), ((), ())), preferred_element_type=F32)


def _layernorm(x):
    mu = jnp.mean(x, axis=-1, keepdims=True)
    xc = x - mu
    var = jnp.mean(xc * xc, axis=-1, keepdims=True)
    return xc * lax.rsqrt(var + LN_EPS)


def _ada_kernel(c_ref, w_ref, b_ref, o_ref):
    c = c_ref[...]
    a = c * (1.0 / (1.0 + jnp.exp(-c)))
    a_hi = a.astype(BF16)
    a_lo = (a - a_hi.astype(F32)).astype(BF16)
    w = w_ref[...]
    w_hi = w.astype(BF16)
    w_lo = (w - w_hi.astype(F32)).astype(BF16)
    o_ref[...] = _dot(a_hi, w_hi) + _dot(a_hi, w_lo) + _dot(a_lo, w_hi) + b_ref[...]


def _ada(cc, w_ada, b_ada):
    depth, d, n = w_ada.shape
    tn = 1536
    return pl.pallas_call(
        _ada_kernel,
        grid=(depth, n // tn),
        in_specs=[
            pl.BlockSpec((cc.shape[0], d), lambda l, j: (0, 0)),
            pl.BlockSpec((None, d, tn), lambda l, j: (l, 0, j)),
            pl.BlockSpec((None, 1, tn), lambda l, j: (l, 0, j)),
        ],
        out_specs=pl.BlockSpec((None, cc.shape[0], tn), lambda l, j: (l, 0, j)),
        out_shape=jax.ShapeDtypeStruct((depth, cc.shape[0], n), F32),
        compiler_params=_cparams("arbitrary", "arbitrary"),
        name="ada",
    )(cc, w_ada, b_ada.reshape(depth, 1, n))


def _swap16(x, lane_lo):
    up = pltpu.roll(x, 16, 1)
    dn = pltpu.roll(x, LANES - 16, 1)
    return jnp.where(lane_lo, dn, up)


def _with_ones(v):
    ones = jnp.ones((v.shape[0], HEAD_DIM), F32)
    parts = []
    for hd in range(v.shape[1] // HEAD_DIM):
        parts += [v[:, hd * HEAD_DIM:(hd + 1) * HEAD_DIM], ones]
    return jnp.concatenate(parts, axis=1).astype(BF16)


def _inproj_kernel(x_ref, mod_ref, w_ref, qg_ref, kg_ref, cos_ref, sin_ref, g512_ref, g128_ref, dft_ref,
                   qa_ref, ka_ref, va_ref, qb_ref, kb_ref, vb_ref, uc_ref, us_ref):
    mod = mod_ref[...]
    sh = mod[:, 0:D_MODEL]
    sc = mod[:, D_MODEL:2 * D_MODEL]
    h = (_layernorm(x_ref[...]) * (1.0 + sc) + sh).astype(BF16)

    cos = cos_ref[...]
    sin = sin_ref[...]
    lane_lo = (lax.broadcasted_iota(jnp.int32, cos.shape, 1) % 32) < 16

    def norm_rope(v, g_mat, gain):
        ms = _dot((v * v).astype(BF16), g_mat)
        vn = v * lax.rsqrt(ms + RMS_EPS) * gain
        outs = []
        for j in range(v.shape[1] // LANES):
            t = vn[:, j * LANES:(j + 1) * LANES]
            outs.append(t * cos + _swap16(t, lane_lo) * sin)
        return outs[0] if len(outs) == 1 else jnp.concatenate(outs, axis=1)

    q = _dot(h, w_ref[:, 0:512])
    qa_ref[...] = norm_rope(q, g512_ref[...], qg_ref[...]).astype(BF16)
    kv = _dot(h, w_ref[:, 512:768])
    ka_ref[...] = norm_rope(kv[:, 0:128], g128_ref[...], kg_ref[...]).astype(BF16)
    va_ref[...] = _with_ones(kv[:, 128:256])
    qb_ref[...] = (_dot(h, w_ref[:, 768:1280]) * Q_SCALE).astype(BF16)
    kb_ref[...] = _dot(h, w_ref[:, 1280:1792]).astype(BF16)
    vb_ref[...] = _with_ones(_dot(h, w_ref[:, 1792:2304]))
    u = _dot(h, w_ref[:, 2304:2816]).astype(BF16)
    dft = dft_ref[...]
    for g in range(C_GROUPS):
        z = _dot(u[:, g * C_GROUP_W:(g + 1) * C_GROUP_W], dft)
        uc_ref[:, g * C_GROUP_W:(g + 1) * C_GROUP_W] = z[:, 0:C_GROUP_W].astype(BF16)
        us_ref[:, g * C_GROUP_W:(g + 1) * C_GROUP_W] = z[:, C_GROUP_W:2 * C_GROUP_W].astype(BF16)


def _inproj(x2, mod3, w_qkvu, q_gain, k_gain, cos_t, sin_t, g512, g128, dft_c, *, batch, seq, tm):
    rows = batch * seq
    nl = seq // tm
    tok = lambda w: pl.BlockSpec((tm, w), lambda i: (i, 0))
    full = _resident_spec
    fo = pl.BlockSpec((tm, 512), lambda i: (i % nl, i // nl))
    o_tok = lambda w: jax.ShapeDtypeStruct((rows, w), BF16)
    return pl.pallas_call(
        _inproj_kernel,
        grid=(rows // tm,),
        in_specs=[
            tok(D_MODEL),
            pl.BlockSpec((None, 1, mod3.shape[2]), lambda i: (i // nl, 0, 0)),
            full(w_qkvu), full(q_gain), full(k_gain),
            pl.BlockSpec((tm, LANES), lambda i: (i % nl, 0)),
            pl.BlockSpec((tm, LANES), lambda i: (i % nl, 0)),
            full(g512), full(g128), full(dft_c),
        ],
        out_specs=[tok(512), tok(128), tok(256), tok(512), tok(512), tok(1024), fo, fo],
        out_shape=[o_tok(512), o_tok(128), o_tok(256), o_tok(512), o_tok(512), o_tok(1024),
                   jax.ShapeDtypeStruct((seq, batch * 512), BF16),
                   jax.ShapeDtypeStruct((seq, batch * 512), BF16)],
        compiler_params=_cparams("arbitrary"),
        name="inproj",
    )(x2, mod3, w_qkvu, q_gain, k_gain, cos_t, sin_t, g512, g128, dft_c)


def _attend_t_multi(qps, chunk_lists):
    n = len(qps)
    m = [None] * n
    acc = [None] * n
    units = [(i, c) for c in range(len(chunk_lists[0])) for i in range(n)]

    def scores(u):
        i, c = u
        k, _, bias_t = chunk_lists[i][c]
        s = _dot_nt(k, qps[i])
        return s if bias_t is None else s + bias_t

    s_next = scores(units[0])
    for idx, (i, c) in enumerate(units):
        s = s_next
        if idx + 1 < len(units):
            s_next = scores(units[idx + 1])
        v1 = chunk_lists[i][c][1]
        mc = jnp.max(s, axis=0, keepdims=True)
        if m[i] is None:
            m[i] = mc
            acc[i] = _dot_tn(v1, jnp.exp2(s - mc).astype(BF16))
        else:
            m_new = jnp.maximum(m[i], mc)
            acc[i] = jnp.exp2(m[i] - m_new) * acc[i] + _dot_tn(v1, jnp.exp2(s - m_new).astype(BF16))
            m[i] = m_new
    return [a[0:HEAD_DIM, :] / a[HEAD_DIM:HEAD_DIM + 1, :] for a in acc]


def _place_head(q128, src_half, dst_half, lane_hi):
    x = q128.astype(F32)
    if src_half != dst_half:
        x = pltpu.roll(x, 64, 1)
    keep = lane_hi if dst_half == 1 else jnp.logical_not(lane_hi)
    return jnp.where(keep, x, 0.0).astype(BF16)


def _gqa_tile_t(q, chunks_fn, tq, lane_hi):
    qps = []
    for g in range(A_KV_HEADS):
        parts = []
        for j in range(A_GROUP):
            hd = A_GROUP * g + j
            parts.append(_place_head(q[:, LANES * (hd // 2):LANES * (hd // 2 + 1)], hd % 2, g, lane_hi))
        qps.append(jnp.concatenate(parts, axis=0))
    outs = _attend_t_multi(qps, [chunks_fn(g) for g in range(A_KV_HEADS)])
    heads_t = [o_t[:, j * tq:(j + 1) * tq] for o_t in outs for j in range(A_GROUP)]
    return jnp.concatenate(heads_t, axis=0).T


def _mha_pair_t(q128, chunks_fn, lane_hi):
    outs = _attend_t_multi([_place_head(q128, hh, hh, lane_hi) for hh in range(2)],
                           [chunks_fn(hh) for hh in range(2)])
    return jnp.concatenate(outs, axis=0).T


def _attn_a_kernel(q_ref, k_ref, v_ref, kc_ref, vc_ref, o_ref, *, tq, chunk_bounds):
    lane_hi = lax.broadcasted_iota(jnp.int32, (tq, LANES), 1) >= 64

    def chunks(g):
        sl = slice(g * LANES, (g + 1) * LANES)
        return ([(kc_ref[...], vc_ref[:, sl], None)]
                + [(k_ref[s:e, :], v_ref[s:e, sl], None) for s, e in chunk_bounds])

    o_ref[...] = _gqa_tile_t(q_ref[...], chunks, tq, lane_hi).astype(BF16)


def _attn_a(qa, ka, va1, kac, vac1, *, batch, seq, ctx_len, tq, tk):
    bounds = [(i, min(i + tk, seq)) for i in range(0, seq, tk)]
    nq = seq // tq
    return pl.pallas_call(
        functools.partial(_attn_a_kernel, tq=tq, chunk_bounds=bounds),
        grid=(batch, nq),
        in_specs=[pl.BlockSpec((tq, 512), lambda b, i: (b * nq + i, 0)),
                  pl.BlockSpec((seq, LANES), lambda b, i: (b, 0)),
                  pl.BlockSpec((seq, 2 * LANES), lambda b, i: (b, 0)),
                  pl.BlockSpec((ctx_len, LANES), lambda b, i: (b, 0)),
                  pl.BlockSpec((ctx_len, 2 * LANES), lambda b, i: (b, 0))],
        out_specs=pl.BlockSpec((tq, 512), lambda b, i: (b * nq + i, 0)),
        out_shape=jax.ShapeDtypeStruct((batch * seq, 512), BF16),
        compiler_params=_cparams("arbitrary", "arbitrary"),
        name="attn_a",
    )(qa, ka, va1, kac, vac1)


def _attn_b_kernel(q_ref, k_ref, v_ref, kc_ref, vc_ref, bias_ref, o_ref, *, rows_n, tk):
    nq = Q_ROWS * GRID_W
    nw = WIN_ROWS * GRID_W
    n_groups = rows_n // Q_ROWS
    lane_hi = lax.broadcasted_iota(jnp.int32, (nq, LANES), 1) >= 64
    kc = kc_ref[...]

    def body(t, carry):
        qps, chunk_lists, q0s = [], [], []
        for u in range(B_UNROLL):
            g = t * B_UNROLL + u
            w0 = jnp.clip(Q_ROWS * g - (WIN_ROWS - Q_ROWS) // 2, 0, rows_n - WIN_ROWS)
            var = jnp.where(g == 0, 0, jnp.where(g == n_groups - 1, 2, 1))
            q0 = pl.multiple_of(g * nq, nq)
            k0 = pl.multiple_of(w0 * GRID_W, GRID_W)
            q128 = q_ref[pl.ds(q0, nq), :]
            q0s.append(q0)
            for hh in range(2):
                sl = slice(hh * LANES, (hh + 1) * LANES)
                qps.append(_place_head(q128, hh, hh, lane_hi))
                chunk_lists.append(
                    [(k_ref[pl.ds(k0 + s, tk), :], v_ref[pl.ds(k0 + s, tk), sl], bias_ref[var, hh, s:s + tk, :])
                     for s in range(0, nw, tk)] + [(kc, vc_ref[:, sl], None)])
        outs = _attend_t_multi(qps, chunk_lists)
        for u in range(B_UNROLL):
            o_ref[pl.ds(q0s[u], nq), :] = jnp.concatenate(outs[2 * u:2 * u + 2], axis=0).T.astype(BF16)
        return carry

    lax.fori_loop(0, n_groups // B_UNROLL, body, 0)


def _attn_b(qb, kb, vb1, kbc, vbc1, bias_t, *, batch, seq, ctx_len, tk):
    rows_n = seq // GRID_W
    nq = Q_ROWS * GRID_W
    nw = WIN_ROWS * GRID_W
    lat = lambda w: pl.BlockSpec((seq, w), lambda hp, b: (b, hp))
    ctx = lambda w: pl.BlockSpec((ctx_len, w), lambda hp, b: (b, hp))
    return pl.pallas_call(
        functools.partial(_attn_b_kernel, rows_n=rows_n, tk=tk),
        grid=(B_HEADS // 2, batch),
        in_specs=[lat(LANES), lat(LANES), lat(2 * LANES), ctx(LANES), ctx(2 * LANES),
                  pl.BlockSpec((3, 2, nw, nq), lambda hp, b: (0, hp, 0, 0))],
        out_specs=lat(LANES),
        out_shape=jax.ShapeDtypeStruct((batch * seq, 512), BF16),
        compiler_params=_cparams("arbitrary", "arbitrary"),
        name="attn_b",
    )(qb, kb, vb1, kbc, vbc1, bias_t)


def _neighbourhood_bias_t(rpb, rows_n):
    kc = NA_COLS
    cols = np.arange(GRID_W)
    c0 = np.clip(cols - kc // 2, 0, GRID_W - kc)
    col_ok = (cols[None, :] >= c0[:, None]) & (cols[None, :] < c0[:, None] + kc)
    off = GRID_W - NA_COLS
    rp = jnp.pad(rpb * LOG2E, ((0, 0), (0, 0), (off, off)))
    t = jnp.stack([rp[:, :, GRID_W - 1 - c:2 * GRID_W - 1 - c] for c in range(GRID_W)], axis=3)
    t = jnp.where(col_ok.T[None, None], t, NEG)
    neg = jnp.full((rpb.shape[0], GRID_W, GRID_W), NEG, F32)
    n_groups = rows_n // Q_ROWS
    variants = []
    for g in (0, min(1, n_groups - 1), n_groups - 1):
        w0 = int(np.clip(Q_ROWS * g - (WIN_ROWS - Q_ROWS) // 2, 0, rows_n - WIN_ROWS))
        k_blocks = []
        for i in range(WIN_ROWS):
            kr = w0 + i
            q_blocks = []
            for a in range(Q_ROWS):
                r = Q_ROWS * g + a
                r0 = int(np.clip(r - NA_ROWS // 2, 0, rows_n - NA_ROWS))
                q_blocks.append(t[:, kr - r + NA_ROWS - 1] if r0 <= kr < r0 + NA_ROWS else neg)
            k_blocks.append(jnp.concatenate(q_blocks, axis=2))
        variants.append(jnp.concatenate(k_blocks, axis=1))
    return jnp.stack(variants, axis=0)


def _attn_ctx_kernel(qa_ref, ka_ref, va_ref, qb_ref, kb_ref, vb_ref, oa_ref, ob_ref, *, lc):
    lane_hi = lax.broadcasted_iota(jnp.int32, (lc, LANES), 1) >= 64
    ka = ka_ref[...]
    oa_ref[...] = _gqa_tile_t(
        qa_ref[...], lambda g: [(ka, va_ref[:, g * LANES:(g + 1) * LANES], None)], lc, lane_hi).astype(BF16)
    for hp in range(B_HEADS // 2):
        sl = slice(hp * LANES, (hp + 1) * LANES)
        k = kb_ref[:, sl]
        chunks = lambda hh: [(k, vb_ref[:, (2 * hp + hh) * LANES:(2 * hp + hh + 1) * LANES], None)]
        ob_ref[:, sl] = _mha_pair_t(qb_ref[:, sl], chunks, lane_hi).astype(BF16)


def _attn_ctx(qac, kac, vac1, qbc, kbc, vbc1, *, batch, ctx_len):
    t = lambda w: pl.BlockSpec((ctx_len, w), lambda b: (b, 0))
    o = jax.ShapeDtypeStruct((batch * ctx_len, 512), BF16)
    return pl.pallas_call(
        functools.partial(_attn_ctx_kernel, lc=ctx_len),
        grid=(batch,),
        in_specs=[t(512), t(128), t(256), t(512), t(512), t(1024)],
        out_specs=[t(512), t(512)],
        out_shape=[o, o],
        compiler_params=_cparams("arbitrary"),
        name="attn_ctx",
    )(qac, kac, vac1, qbc, kbc, vbc1)


def _dft_kernel(ac_ref, as_ref, bc_ref, bs_ref, uc_ref, us_ref, o_ref, acc_ref, *, bm):
    i = pl.program_id(0)

    @pl.when(pl.program_id(2) == 0)
    def _():
        acc_ref[...] = jnp.zeros_like(acc_ref)

    bc = bc_ref[...]
    bs = bs_ref[...]
    c_rows, s_rows = [], []
    for r in range(bm // GRID_W):
        a_c = ac_ref[pl.ds(i * (bm // GRID_W) + r, 1), :]
        a_s = as_ref[pl.ds(i * (bm // GRID_W) + r, 1), :]
        c_rows.append((a_c * bc - a_s * bs).astype(BF16))
        s_rows.append((a_s * bc + a_c * bs).astype(BF16))
    c_t = jnp.concatenate(c_rows, axis=0)
    s_t = jnp.concatenate(s_rows, axis=0)
    acc_ref[...] += _dot(c_t, uc_ref[...]) - _dot(s_t, us_ref[...])

    @pl.when(pl.program_id(2) == pl.num_programs(2) - 1)
    def _():
        o_ref[...] = acc_ref[...].astype(BF16)


def _dft(tabs, uc, us):
    a_c, a_s, b_c, b_s = tabs
    n = uc.shape[0]
    width = uc.shape[1]
    bm = min(n, 1024)
    bn = min(width, 1024)
    bk = min(n, 1024)
    a_spec = pl.BlockSpec((a_c.shape[0], bk), lambda i, j, k: (0, k))
    b_spec = pl.BlockSpec((GRID_W, bk), lambda i, j, k: (0, k))
    u_spec = pl.BlockSpec((bk, bn), lambda i, j, k: (k, j))
    return pl.pallas_call(
        functools.partial(_dft_kernel, bm=bm),
        grid=(n // bm, width // bn, n // bk),
        in_specs=[a_spec, a_spec, b_spec, b_spec, u_spec, u_spec],
        out_specs=pl.BlockSpec((bm, bn), lambda i, j, k: (i, j)),
        out_shape=jax.ShapeDtypeStruct((n, width), BF16),
        scratch_shapes=[pltpu.VMEM((bm, bn), F32)],
        compiler_params=_cparams("arbitrary", "arbitrary", "arbitrary"),
        name="dft",
    )(a_c, a_s, b_c, b_s, uc, us)


def _dft_factor_tables(n):
    r = n // GRID_W
    k = np.arange(n, dtype=np.int64)
    ang_a = (2.0 * np.pi / r) * ((np.arange(r)[:, None] * k[None, :]) % r)
    ang_b = (2.0 * np.pi / n) * ((np.arange(GRID_W)[:, None] * k[None, :]) % n)
    pad = ((0, (-r) % 8), (0, 0))
    scale = n ** -0.5
    return tuple(jnp.asarray(t, dtype=F32) for t in (
        np.pad(np.cos(ang_a), pad), np.pad(np.sin(ang_a), pad), np.cos(ang_b) * scale, np.sin(ang_b) * scale))


def _dft_a_kernel(m_ref, p_ref, q_ref, r1_ref, r2_ref):
    r = p_ref.shape[0]
    out = _dot(m_ref[...], jnp.concatenate([p_ref[...], q_ref[...]], axis=0))
    r1_ref[...] = out[0:r].astype(BF16)
    r2_ref[...] = out[r:2 * r].astype(BF16)


def _dft_b_kernel(g_ref, r1_ref, r2_ref, o_ref):
    o_ref[...] = _dot(g_ref[...], jnp.concatenate([r1_ref[...], r2_ref[...]], axis=0)).astype(BF16)


def _dft2(mat_a, g_b, uc, us):
    n, width = uc.shape
    r = n // GRID_W
    cols = GRID_W * width
    tn = min(cols, 8192)
    col_tile = pl.BlockSpec((r, tn), lambda j: (0, j))
    mid = jax.ShapeDtypeStruct((r, cols), BF16)
    r1, r2 = pl.pallas_call(
        _dft_a_kernel,
        grid=(cols // tn,),
        in_specs=[_resident_spec(mat_a), col_tile, col_tile],
        out_specs=[col_tile, col_tile],
        out_shape=[mid, mid],
        compiler_params=_cparams("arbitrary"),
        name="dft_a",
    )(mat_a, uc.reshape(r, cols), us.reshape(r, cols))
    by_k1 = pl.BlockSpec((None, GRID_W, width), lambda j: (j, 0, 0))
    out = pl.pallas_call(
        _dft_b_kernel,
        grid=(r,),
        in_specs=[pl.BlockSpec((None, GRID_W, 2 * GRID_W), lambda j: (j, 0, 0)), by_k1, by_k1],
        out_specs=by_k1,
        out_shape=jax.ShapeDtypeStruct((r, GRID_W, width), BF16),
        compiler_params=_cparams("arbitrary"),
        name="dft_b",
    )(g_b, r1.reshape(r, GRID_W, width), r2.reshape(r, GRID_W, width))
    return jnp.swapaxes(out, 0, 1).reshape(n, width)


def _dft2_tables(n):
    r = n // GRID_W
    i_r = np.arange(r, dtype=np.int64)
    ang = (2.0 * np.pi / r) * ((i_r[:, None] * i_r[None, :]) % r)
    c, s = np.cos(ang), np.sin(ang)
    mat_a = np.block([[c, -s], [s, c]])
    k = i_r[:, None, None] + r * np.arange(GRID_W)[None, :, None]
    ang_b = (2.0 * np.pi / n) * ((k * np.arange(GRID_W)[None, None, :]) % n)
    g_b = np.concatenate([np.cos(ang_b), -np.sin(ang_b)], axis=2) * n ** -0.5
    return jnp.asarray(mat_a, dtype=F32).astype(BF16), jnp.asarray(g_b, dtype=F32).astype(BF16)


def _merge_kernel(x_ref, mod_ref, oa_ref, ob_ref, oc_ref, wg_ref, wb_ref, wo_ref, g_ref, b_ref, o_ref, *, alpha):
    x = x_ref[...]
    mod = mod_ref[...]
    sh = mod[:, 0:D_MODEL]
    sc = mod[:, D_MODEL:2 * D_MODEL]
    gate = mod[:, 2 * D_MODEL:3 * D_MODEL]
    h = (_layernorm(x) * (1.0 + sc) + sh).astype(BF16)
    y = None
    for i, br_ref in enumerate((oa_ref, ob_ref, oc_ref)):
        gz = _dot(h, wg_ref[:, i * D_MODEL:(i + 1) * D_MODEL])
        t = _dot(br_ref[...], wb_ref[i]) * (1.0 / (1.0 + jnp.exp(-gz)))
        y = t if y is None else y + t
    yo = _dot(y.astype(BF16), wo_ref[...])
    o_ref[...] = _layernorm(alpha * x + gate * yo) * g_ref[...] + b_ref[...]


def _merge(x2, mod3, oa, ob, oc, w_gate, w_branch, w_out, ln_g, ln_b, *, batch, seq, tm, alpha):
    rows = batch * seq
    nl = seq // tm
    tok = lambda w: pl.BlockSpec((tm, w), lambda i: (i, 0))
    full = _resident_spec
    return pl.pallas_call(
        functools.partial(_merge_kernel, alpha=alpha),
        grid=(rows // tm,),
        in_specs=[
            tok(D_MODEL),
            pl.BlockSpec((None, 1, mod3.shape[2]), lambda i: (i // nl, 0, 0)),
            tok(512), tok(512),
            pl.BlockSpec((tm, 512), lambda i: (i % nl, i // nl)),
            full(w_gate), full(w_branch), full(w_out), full(ln_g), full(ln_b),
        ],
        out_specs=tok(D_MODEL),
        out_shape=jax.ShapeDtypeStruct((rows, D_MODEL), F32),
        compiler_params=_cparams("arbitrary"),
        name="merge",
    )(x2, mod3, oa, ob, oc, w_gate, w_branch, w_out, ln_g, ln_b)


FF_CHUNKS = ((0, 768), (768, 1536), (1536, 2304), (2304, 2816))


def _ffn_kernel(x_ref, mod_ref, wgu_ref, wd_ref, g_ref, b_ref, o_ref, *, alpha):
    x = x_ref[...]
    mod = mod_ref[...]
    sh = mod[:, 3 * D_MODEL:4 * D_MODEL]
    sc = mod[:, 4 * D_MODEL:5 * D_MODEL]
    gate = mod[:, 5 * D_MODEL:6 * D_MODEL]
    h = (_layernorm(x) * (1.0 + sc) + sh).astype(BF16)
    f = None
    for s, e in FF_CHUNKS:
        g = _dot(h, wgu_ref[:, s:e])
        u = _dot(h, wgu_ref[:, D_FF + s:D_FF + e])
        a = (g * (1.0 / (1.0 + jnp.exp(-g))) * u).astype(BF16)
        t = _dot(a, wd_ref[s:e, :])
        f = t if f is None else f + t
    o_ref[...] = _layernorm(alpha * x + gate * f) * g_ref[...] + b_ref[...]


def _ffn(x2, mod3, w_gu, w_down, ln_g, ln_b, *, batch, seq, tm, alpha):
    rows = batch * seq
    nl = seq // tm
    tok = pl.BlockSpec((tm, D_MODEL), lambda i: (i, 0))
    full = _resident_spec
    return pl.pallas_call(
        functools.partial(_ffn_kernel, alpha=alpha),
        grid=(rows // tm,),
        in_specs=[tok, pl.BlockSpec((None, 1, mod3.shape[2]), lambda i: (i // nl, 0, 0)),
                  full(w_gu), full(w_down), full(ln_g), full(ln_b)],
        out_specs=tok,
        out_shape=jax.ShapeDtypeStruct((rows, D_MODEL), F32),
        compiler_params=_cparams("arbitrary"),
        name="ffn",
    )(x2, mod3, w_gu, w_down, ln_g, ln_b)


def _rope_tables(seq):
    quarter = HEAD_DIM // 4
    pos = np.arange(seq)
    freqs = ROPE_THETA ** (-np.arange(quarter, dtype=np.float64) / quarter)
    ar = (pos // GRID_W)[:, None] * freqs
    ac = (pos % GRID_W)[:, None] * freqs
    cos64 = np.concatenate([np.cos(ar), np.cos(ar), np.cos(ac), np.cos(ac)], axis=1)
    sin64 = np.concatenate([-np.sin(ar), np.sin(ar), -np.sin(ac), np.sin(ac)], axis=1)
    return jnp.asarray(np.tile(cos64, (1, 2)), dtype=F32), jnp.asarray(np.tile(sin64, (1, 2)), dtype=F32)


def _head_mean_matrix(width):
    idx = np.arange(width) // HEAD_DIM
    return jnp.asarray((idx[:, None] == idx[None, :]).astype(np.float32) / HEAD_DIM).astype(BF16)


def _channel_dft_matrix():
    idx = np.arange(C_GROUP_W)
    ang = (2.0 * np.pi / C_GROUP_W) * ((idx[:, None] * idx[None, :]) % C_GROUP_W)
    scale = C_GROUP_W ** -0.5
    mat = np.concatenate([np.cos(ang), np.sin(ang)], axis=1) * scale
    return jnp.asarray(mat, dtype=F32).astype(BF16)


def kernel(x, c, ctx, c_ctx, w_ada, b_ada, w_in, q_norm, k_norm, rpb, w_branch, w_out,
           ln1_g, ln1_b, w_gu, w_down, ln2_g, ln2_b):
    batch, seq, d = x.shape
    ctx_len = ctx.shape[1]
    depth = w_ada.shape[0]
    assert d == D_MODEL and seq % (B_UNROLL * Q_ROWS * GRID_W) == 0 and seq // GRID_W >= WIN_ROWS
    assert ctx_len % LANES == 0
    alpha = (2.0 * depth) ** 0.25
    tm = 512
    tmc = min(ctx_len, 512)

    pad = (-(batch + 1)) % 8
    cc = jnp.concatenate([c, c_ctx[None, :], jnp.zeros((pad, d), F32)], axis=0)
    mods = _ada(cc, w_ada, b_ada)

    cos_t, sin_t = _rope_tables(seq)
    cos_c = jnp.ones((ctx_len, LANES), F32)
    sin_c = jnp.zeros((ctx_len, LANES), F32)
    g512 = _head_mean_matrix(512)
    g128 = _head_mean_matrix(128)
    dft_c = _channel_dft_matrix()
    tabs_l = _dft2_tables(seq)
    tabs_c = _dft_factor_tables(ctx_len)

    xl = x.reshape(batch * seq, d)
    xc = ctx.reshape(batch * ctx_len, d)
    for l in range(depth):
        with_ctx = l < depth - 1
        w_qkvu = w_in[l, :, :N_QKVU].astype(BF16)
        w_gate = w_in[l, :, N_QKVU:].astype(BF16)
        wb = w_branch[l].astype(BF16)
        wo = w_out[l].astype(BF16)
        wgu = w_gu[l].astype(BF16)
        wd = w_down[l].astype(BF16)
        q_gain = jnp.tile(q_norm[l] * Q_SCALE, A_HEADS)[None, :]
        k_gain = jnp.tile(k_norm[l], A_KV_HEADS)[None, :]
        mod_l = mods[l, :batch][:, None, :]
        mod_c = jnp.broadcast_to(mods[l, batch][None, None, :], (batch, 1, 6 * d))
        ln1 = (ln1_g[l][None, :], ln1_b[l][None, :])
        ln2 = (ln2_g[l][None, :], ln2_b[l][None, :])

        qa, ka, va1, qb, kb, vb1, uc, us = _inproj(
            xl, mod_l, w_qkvu, q_gain, k_gain, cos_t, sin_t, g512, g128, dft_c, batch=batch, seq=seq, tm=tm)
        qac, kac, vac1, qbc, kbc, vbc1, ucc, usc = _inproj(
            xc, mod_c, w_qkvu, q_gain, k_gain, cos_c, sin_c, g512, g128, dft_c, batch=batch, seq=ctx_len, tm=tmc)

        oa = _attn_a(qa, ka, va1, kac, vac1, batch=batch, seq=seq, ctx_len=ctx_len, tq=256, tk=256)
        bias_t = _neighbourhood_bias_t(rpb[l], seq // GRID_W)
        ob = _attn_b(qb, kb, vb1, kbc, vbc1, bias_t, batch=batch, seq=seq, ctx_len=ctx_len, tk=512)
        oc = _dft2(*tabs_l, uc, us)

        x1 = _merge(xl, mod_l, oa, ob, oc, w_gate, wb, wo, *ln1, batch=batch, seq=seq, tm=tm, alpha=alpha)
        xl = _ffn(x1, mod_l, wgu, wd, *ln2, batch=batch, seq=seq, tm=tm, alpha=alpha)

        if with_ctx:
            oac, obc = _attn_ctx(qac, kac, vac1, qbc, kbc, vbc1, batch=batch, ctx_len=ctx_len)
            occ = _dft(tabs_c, ucc, usc)
            xc1 = _merge(xc, mod_c, oac, obc, occ, w_gate, wb, wo, *ln1,
                         batch=batch, seq=ctx_len, tm=tmc, alpha=alpha)
            xc = _ffn(xc1, mod_c, wgu, wd, *ln2, batch=batch, seq=ctx_len, tm=tmc, alpha=alpha)
    return xl.reshape(batch, seq, d)
```

```python
import functools
import math

import numpy as np
import jax
import jax.numpy as jnp
from jax import lax
from jax.experimental import pallas as pl
from jax.experimental.pallas import tpu as pltpu

F32 = jnp.float32
BF16 = jnp.bfloat16

D_MODEL = 1024
GRID_W = 64
HEAD_DIM = 64
A_HEADS = 8
A_KV_HEADS = 2
A_GROUP = A_HEADS // A_KV_HEADS
B_HEADS = 8
NA_ROWS = 8
NA_COLS = 16
C_GROUPS = 4
C_GROUP_W = 128
BRANCH_W = 512
N_BRANCH = 3
D_FF = 2816
ROPE_THETA = 10000.0
LN_EPS = 1e-6
RMS_EPS = 1e-6
N_QKVU = 2816
Q_ROWS = 8
WIN_ROWS = 16
B_UNROLL = 4
NEG = -1e30
LOG2E = math.log2(math.e)
Q_SCALE = HEAD_DIM ** -0.5 * LOG2E
LANES = 128
VMEM_LIMIT = 56 * 1024 * 1024


def _cparams(*sem):
    return pltpu.CompilerParams(dimension_semantics=sem, vmem_limit_bytes=VMEM_LIMIT)


def _resident_spec(a):
    return pl.BlockSpec(a.shape, lambda *_: (0,) * a.ndim, pipeline_mode=pl.Buffered(1))


def _dot(a, b):
    return jnp.dot(a, b, preferred_element_type=F32)


def _dot_nt(a, b):
    return lax.dot_general(a, b, (((1,), (1,)), ((), ())), preferred_element_type=F32)


def _dot_tn(a, b):
    return lax.dot_general(a, b, (((0,), (0,)), ((), ())), preferred_element_type=F32)


def _layernorm(x):
    mu = jnp.mean(x, axis=-1, keepdims=True)
    xc = x - mu
    var = jnp.mean(xc * xc, axis=-1, keepdims=True)
    return xc * lax.rsqrt(var + LN_EPS)


def _ada_kernel(c_ref, w_ref, b_ref, o_ref):
    c = c_ref[...]
    a = c * (1.0 / (1.0 + jnp.exp(-c)))
    a_hi = a.astype(BF16)
    a_lo = (a - a_hi.astype(F32)).astype(BF16)
    w = w_ref[...]
    w_hi = w.astype(BF16)
    w_lo = (w - w_hi.astype(F32)).astype(BF16)
    o_ref[...] = _dot(a_hi, w_hi) + _dot(a_hi, w_lo) + _dot(a_lo, w_hi) + b_ref[...]


def _ada(cc, w_ada, b_ada):
    depth, d, n = w_ada.shape
    tn = 1536
    return pl.pallas_call(
        _ada_kernel,
        grid=(depth, n // tn),
        in_specs=[
            pl.BlockSpec((cc.shape[0], d), lambda l, j: (0, 0)),
            pl.BlockSpec((None, d, tn), lambda l, j: (l, 0, j)),
            pl.BlockSpec((None, 1, tn), lambda l, j: (l, 0, j)),
        ],
        out_specs=pl.BlockSpec((None, cc.shape[0], tn), lambda l, j: (l, 0, j)),
        out_shape=jax.ShapeDtypeStruct((depth, cc.shape[0], n), F32),
        compiler_params=_cparams("arbitrary", "arbitrary"),
        name="ada",
    )(cc, w_ada, b_ada.reshape(depth, 1, n))


def _swap16(x, lane_lo):
    up = pltpu.roll(x, 16, 1)
    dn = pltpu.roll(x, LANES - 16, 1)
    return jnp.where(lane_lo, dn, up)


def _with_ones(v):
    ones = jnp.ones((v.shape[0], HEAD_DIM), F32)
    parts = []
    for hd in range(v.shape[1] // HEAD_DIM):
        parts += [v[:, hd * HEAD_DIM:(hd + 1) * HEAD_DIM], ones]
    return jnp.concatenate(parts, axis=1).astype(BF16)


def _inproj_kernel(x_ref, mod_ref, w_ref, qg_ref, kg_ref, cos_ref, sin_ref, g512_ref, g128_ref, dft_ref,
                   qa_ref, ka_ref, va_ref, qb_ref, kb_ref, vb_ref, uc_ref, us_ref):
    mod = mod_ref[...]
    sh = mod[:, 0:D_MODEL]
    sc = mod[:, D_MODEL:2 * D_MODEL]
    h = (_layernorm(x_ref[...]) * (1.0 + sc) + sh).astype(BF16)

    cos = cos_ref[...]
    sin = sin_ref[...]
    lane_lo = (lax.broadcasted_iota(jnp.int32, cos.shape, 1) % 32) < 16

    def norm_rope(v, g_mat, gain):
        ms = _dot((v * v).astype(BF16), g_mat)
        vn = v * lax.rsqrt(ms + RMS_EPS) * gain
        outs = []
        for j in range(v.shape[1] // LANES):
            t = vn[:, j * LANES:(j + 1) * LANES]
            outs.append(t * cos + _swap16(t, lane_lo) * sin)
        return outs[0] if len(outs) == 1 else jnp.concatenate(outs, axis=1)

    q = _dot(h, w_ref[:, 0:512])
    qa_ref[...] = norm_rope(q, g512_ref[...], qg_ref[...]).astype(BF16)
    kv = _dot(h, w_ref[:, 512:768])
    ka_ref[...] = norm_rope(kv[:, 0:128], g128_ref[...], kg_ref[...]).astype(BF16)
    va_ref[...] = _with_ones(kv[:, 128:256])
    qb_ref[...] = (_dot(h, w_ref[:, 768:1280]) * Q_SCALE).astype(BF16)
    kb_ref[...] = _dot(h, w_ref[:, 1280:1792]).astype(BF16)
    vb_ref[...] = _with_ones(_dot(h, w_ref[:, 1792:2304]))
    u = _dot(h, w_ref[:, 2304:2816]).astype(BF16)
    dft = dft_ref[...]
    for g in range(C_GROUPS):
        z = _dot(u[:, g * C_GROUP_W:(g + 1) * C_GROUP_W], dft)
        uc_ref[:, g * C_GROUP_W:(g + 1) * C_GROUP_W] = z[:, 0:C_GROUP_W].astype(BF16)
        us_ref[:, g * C_GROUP_W:(g + 1) * C_GROUP_W] = z[:, C_GROUP_W:2 * C_GROUP_W].astype(BF16)


def _inproj(x2, mod3, w_qkvu, q_gain, k_gain, cos_t, sin_t, g512, g128, dft_c, *, batch, seq, tm):
    rows = batch * seq
    nl = seq // tm
    tok = lambda w: pl.BlockSpec((tm, w), lambda i: (i, 0))
    full = _resident_spec
    fo = pl.BlockSpec((tm, 512), lambda i: (i % nl, i // nl))
    o_tok = lambda w: jax.ShapeDtypeStruct((rows, w), BF16)
    return pl.pallas_call(
        _inproj_kernel,
        grid=(rows // tm,),
        in_specs=[
            tok(D_MODEL),
            pl.BlockSpec((None, 1, mod3.shape[2]), lambda i: (i // nl, 0, 0)),
            full(w_qkvu), full(q_gain), full(k_gain),
            pl.BlockSpec((tm, LANES), lambda i: (i % nl, 0)),
            pl.BlockSpec((tm, LANES), lambda i: (i % nl, 0)),
            full(g512), full(g128), full(dft_c),
        ],
        out_specs=[tok(512), tok(128), tok(256), tok(512), tok(512), tok(1024), fo, fo],
        out_shape=[o_tok(512), o_tok(128), o_tok(256), o_tok(512), o_tok(512), o_tok(1024),
                   jax.ShapeDtypeStruct((seq, batch * 512), BF16),
                   jax.ShapeDtypeStruct((seq, batch * 512), BF16)],
        compiler_params=_cparams("arbitrary"),
        name="inproj",
    )(x2, mod3, w_qkvu, q_gain, k_gain, cos_t, sin_t, g512, g128, dft_c)


def _attend_t_multi(qps, chunk_lists):
    n = len(qps)
    m = [None] * n
    acc = [None] * n
    units = [(i, c) for c in range(len(chunk_lists[0])) for i in range(n)]

    def scores(u):
        i, c = u
        k, _, bias_t = chunk_lists[i][c]
        s = _dot_nt(k, qps[i])
        return s if bias_t is None else s + bias_t

    s_next = scores(units[0])
    for idx, (i, c) in enumerate(units):
        s = s_next
        if idx + 1 < len(units):
            s_next = scores(units[idx + 1])
        v1 = chunk_lists[i][c][1]
        mc = jnp.max(s, axis=0, keepdims=True)
        if m[i] is None:
            m[i] = mc
            acc[i] = _dot_tn(v1, jnp.exp2(s - mc).astype(BF16))
        else:
            m_new = jnp.maximum(m[i], mc)
            acc[i] = jnp.exp2(m[i] - m_new) * acc[i] + _dot_tn(v1, jnp.exp2(s - m_new).astype(BF16))
            m[i] = m_new
    return [a[0:HEAD_DIM, :] / a[HEAD_DIM:HEAD_DIM + 1, :] for a in acc]


def _place_head(q128, src_half, dst_half, lane_hi):
    x = q128.astype(F32)
    if src_half != dst_half:
        x = pltpu.roll(x, 64, 1)
    keep = lane_hi if dst_half == 1 else jnp.logical_not(lane_hi)
    return jnp.where(keep, x, 0.0).astype(BF16)


def _gqa_tile_t(q, chunks_fn, tq, lane_hi):
    qps = []
    for g in range(A_KV_HEADS):
        parts = []
        for j in range(A_GROUP):
            hd = A_GROUP * g + j
            parts.append(_place_head(q[:, LANES * (hd // 2):LANES * (hd // 2 + 1)], hd % 2, g, lane_hi))
        qps.append(jnp.concatenate(parts, axis=0))
    outs = _attend_t_multi(qps, [chunks_fn(g) for g in range(A_KV_HEADS)])
    heads_t = [o_t[:, j * tq:(j + 1) * tq] for o_t in outs for j in range(A_GROUP)]
    return jnp.concatenate(heads_t, axis=0).T


def _mha_pair_t(q128, chunks_fn, lane_hi):
    outs = _attend_t_multi([_place_head(q128, hh, hh, lane_hi) for hh in range(2)],
                           [chunks_fn(hh) for hh in range(2)])
    return jnp.concatenate(outs, axis=0).T


def _attn_a_kernel(q_ref, k_ref, v_ref, kc_ref, vc_ref, o_ref, *, tq, chunk_bounds):
    lane_hi = lax.broadcasted_iota(jnp.int32, (tq, LANES), 1) >= 64

    def chunks(g):
        sl = slice(g * LANES, (g + 1) * LANES)
        return ([(kc_ref[...], vc_ref[:, sl], None)]
                + [(k_ref[s:e, :], v_ref[s:e, sl], None) for s, e in chunk_bounds])

    o_ref[...] = _gqa_tile_t(q_ref[...], chunks, tq, lane_hi).astype(BF16)


def _attn_a(qa, ka, va1, kac, vac1, *, batch, seq, ctx_len, tq, tk):
    bounds = [(i, min(i + tk, seq)) for i in range(0, seq, tk)]
    nq = seq // tq
    return pl.pallas_call(
        functools.partial(_attn_a_kernel, tq=tq, chunk_bounds=bounds),
        grid=(batch, nq),
        in_specs=[pl.BlockSpec((tq, 512), lambda b, i: (b * nq + i, 0)),
                  pl.BlockSpec((seq, LANES), lambda b, i: (b, 0)),
                  pl.BlockSpec((seq, 2 * LANES), lambda b, i: (b, 0)),
                  pl.BlockSpec((ctx_len, LANES), lambda b, i: (b, 0)),
                  pl.BlockSpec((ctx_len, 2 * LANES), lambda b, i: (b, 0))],
        out_specs=pl.BlockSpec((tq, 512), lambda b, i: (b * nq + i, 0)),
        out_shape=jax.ShapeDtypeStruct((batch * seq, 512), BF16),
        compiler_params=_cparams("arbitrary", "arbitrary"),
        name="attn_a",
    )(qa, ka, va1, kac, vac1)


def _attn_b_kernel(q_ref, k_ref, v_ref, kc_ref, vc_ref, bias_ref, o_ref, *, rows_n, tk):
    nq = Q_ROWS * GRID_W
    nw = WIN_ROWS * GRID_W
    n_groups = rows_n // Q_ROWS
    lane_hi = lax.broadcasted_iota(jnp.int32, (nq, LANES), 1) >= 64
    kc = kc_ref[...]

    def body(t, carry):
        qps, chunk_lists, q0s = [], [], []
        for u in range(B_UNROLL):
            g = t * B_UNROLL + u
            w0 = jnp.clip(Q_ROWS * g - (WIN_ROWS - Q_ROWS) // 2, 0, rows_n - WIN_ROWS)
            var = jnp.where(g == 0, 0, jnp.where(g == n_groups - 1, 2, 1))
            q0 = pl.multiple_of(g * nq, nq)
            k0 = pl.multiple_of(w0 * GRID_W, GRID_W)
            q128 = q_ref[pl.ds(q0, nq), :]
            q0s.append(q0)
            for hh in range(2):
                sl = slice(hh * LANES, (hh + 1) * LANES)
                qps.append(_place_head(q128, hh, hh, lane_hi))
                chunk_lists.append(
                    [(k_ref[pl.ds(k0 + s, tk), :], v_ref[pl.ds(k0 + s, tk), sl], bias_ref[var, hh, s:s + tk, :])
                     for s in range(0, nw, tk)] + [(kc, vc_ref[:, sl], None)])
        outs = _attend_t_multi(qps, chunk_lists)
        for u in range(B_UNROLL):
            o_ref[pl.ds(q0s[u], nq), :] = jnp.concatenate(outs[2 * u:2 * u + 2], axis=0).T.astype(BF16)
        return carry

    lax.fori_loop(0, n_groups // B_UNROLL, body, 0)


def _attn_b(qb, kb, vb1, kbc, vbc1, bias_t, *, batch, seq, ctx_len, tk):
    rows_n = seq // GRID_W
    nq = Q_ROWS * GRID_W
    nw = WIN_ROWS * GRID_W
    lat = lambda w: pl.BlockSpec((seq, w), lambda hp, b: (b, hp))
    ctx = lambda w: pl.BlockSpec((ctx_len, w), lambda hp, b: (b, hp))
    return pl.pallas_call(
        functools.partial(_attn_b_kernel, rows_n=rows_n, tk=tk),
        grid=(B_HEADS // 2, batch),
        in_specs=[lat(LANES), lat(LANES), lat(2 * LANES), ctx(LANES), ctx(2 * LANES),
                  pl.BlockSpec((3, 2, nw, nq), lambda hp, b: (0, hp, 0, 0))],
        out_specs=lat(LANES),
        out_shape=jax.ShapeDtypeStruct((batch * seq, 512), BF16),
        compiler_params=_cparams("arbitrary", "arbitrary"),
        name="attn_b",
    )(qb, kb, vb1, kbc, vbc1, bias_t)


def _neighbourhood_bias_t(rpb, rows_n):
    kc = NA_COLS
    cols = np.arange(GRID_W)
    c0 = np.clip(cols - kc // 2, 0, GRID_W - kc)
    col_ok = (cols[None, :] >= c0[:, None]) & (cols[None, :] < c0[:, None] + kc)
    heads, n_dr, n_dc = rpb.shape
    dc = cols[:, None] - cols[None, :] + NA_COLS - 1
    onehot = (np.arange(n_dc)[:, None, None] == dc[None]).astype(np.float32)
    t = jnp.einsum("hdj,jxy->hdxy", rpb * LOG2E, jnp.asarray(onehot), precision=lax.Precision.HIGHEST)
    t = jnp.where(col_ok.T[None, None], t, NEG)
    t = jnp.concatenate([t, jnp.full((heads, 1, GRID_W, GRID_W), NEG, F32)], axis=1)
    n_groups = rows_n // Q_ROWS
    idx = np.full((3, WIN_ROWS, Q_ROWS), n_dr, np.int32)
    for v, g in enumerate((0, min(1, n_groups - 1), n_groups - 1)):
        w0 = int(np.clip(Q_ROWS * g - (WIN_ROWS - Q_ROWS) // 2, 0, rows_n - WIN_ROWS))
        for i in range(WIN_ROWS):
            for a in range(Q_ROWS):
                r = Q_ROWS * g + a
                r0 = int(np.clip(r - NA_ROWS // 2, 0, rows_n - NA_ROWS))
                if r0 <= w0 + i < r0 + NA_ROWS:
                    idx[v, i, a] = w0 + i - r + NA_ROWS - 1
    blocks = jnp.take(t, jnp.asarray(idx.reshape(-1)), axis=1)
    blocks = blocks.reshape(heads, 3, WIN_ROWS, Q_ROWS, GRID_W, GRID_W)
    return blocks.transpose(1, 0, 2, 4, 3, 5).reshape(3, heads, WIN_ROWS * GRID_W, Q_ROWS * GRID_W)


def _attn_ctx_kernel(qa_ref, ka_ref, va_ref, qb_ref, kb_ref, vb_ref, oa_ref, ob_ref, *, lc):
    lane_hi = lax.broadcasted_iota(jnp.int32, (lc, LANES), 1) >= 64
    ka = ka_ref[...]
    oa_ref[...] = _gqa_tile_t(
        qa_ref[...], lambda g: [(ka, va_ref[:, g * LANES:(g + 1) * LANES], None)], lc, lane_hi).astype(BF16)
    for hp in range(B_HEADS // 2):
        sl = slice(hp * LANES, (hp + 1) * LANES)
        k = kb_ref[:, sl]
        chunks = lambda hh: [(k, vb_ref[:, (2 * hp + hh) * LANES:(2 * hp + hh + 1) * LANES], None)]
        ob_ref[:, sl] = _mha_pair_t(qb_ref[:, sl], chunks, lane_hi).astype(BF16)


def _attn_ctx(qac, kac, vac1, qbc, kbc, vbc1, *, batch, ctx_len):
    t = lambda w: pl.BlockSpec((ctx_len, w), lambda b: (b, 0))
    o = jax.ShapeDtypeStruct((batch * ctx_len, 512), BF16)
    return pl.pallas_call(
        functools.partial(_attn_ctx_kernel, lc=ctx_len),
        grid=(batch,),
        in_specs=[t(512), t(128), t(256), t(512), t(512), t(1024)],
        out_specs=[t(512), t(512)],
        out_shape=[o, o],
        compiler_params=_cparams("arbitrary"),
        name="attn_ctx",
    )(qac, kac, vac1, qbc, kbc, vbc1)


def _dft_kernel(ac_ref, as_ref, bc_ref, bs_ref, uc_ref, us_ref, o_ref, ct_ref, st_ref, *, bm):
    i = pl.program_id(0)

    @pl.when(pl.program_id(1) == 0)
    def _():
        bc = bc_ref[...]
        bs = bs_ref[...]
        for r in range(bm // GRID_W):
            a_c = ac_ref[pl.ds(i * (bm // GRID_W) + r, 1), :]
            a_s = as_ref[pl.ds(i * (bm // GRID_W) + r, 1), :]
            ct_ref[r * GRID_W:(r + 1) * GRID_W, :] = (a_c * bc - a_s * bs).astype(BF16)
            st_ref[r * GRID_W:(r + 1) * GRID_W, :] = (a_s * bc + a_c * bs).astype(BF16)

    o_ref[...] = (_dot(ct_ref[...], uc_ref[...]) - _dot(st_ref[...], us_ref[...])).astype(BF16)


def _dft(tabs, uc, us):
    n, width = uc.shape
    bm = min(n, 1024)
    bn = min(width, 512)
    u_spec = pl.BlockSpec((n, bn), lambda i, j: (0, j))
    return pl.pallas_call(
        functools.partial(_dft_kernel, bm=bm),
        grid=(n // bm, width // bn),
        in_specs=[_resident_spec(t) for t in tabs] + [u_spec, u_spec],
        out_specs=pl.BlockSpec((bm, bn), lambda i, j: (i, j)),
        out_shape=jax.ShapeDtypeStruct((n, width), BF16),
        scratch_shapes=[pltpu.VMEM((bm, n), BF16), pltpu.VMEM((bm, n), BF16)],
        compiler_params=_cparams("arbitrary", "arbitrary"),
        name="dft",
    )(*tabs, uc, us)


def _dft_factor_tables(n):
    r = n // GRID_W
    k = np.arange(n, dtype=np.int64)
    ang_a = (2.0 * np.pi / r) * ((np.arange(r)[:, None] * k[None, :]) % r)
    ang_b = (2.0 * np.pi / n) * ((np.arange(GRID_W)[:, None] * k[None, :]) % n)
    pad = ((0, (-r) % 8), (0, 0))
    scale = n ** -0.5
    return tuple(jnp.asarray(t, dtype=F32) for t in (
        np.pad(np.cos(ang_a), pad), np.pad(np.sin(ang_a), pad), np.cos(ang_b) * scale, np.sin(ang_b) * scale))


def _merge_kernel(x_ref, mod_ref, oa_ref, ob_ref, oc_ref, wg_ref, wb_ref, wo_ref, g_ref, b_ref, o_ref, *, alpha):
    x = x_ref[...]
    mod = mod_ref[...]
    sh = mod[:, 0:D_MODEL]
    sc = mod[:, D_MODEL:2 * D_MODEL]
    gate = mod[:, 2 * D_MODEL:3 * D_MODEL]
    h = (_layernorm(x) * (1.0 + sc) + sh).astype(BF16)
    y = None
    for i, br_ref in enumerate((oa_ref, ob_ref, oc_ref)):
        gz = _dot(h, wg_ref[:, i * D_MODEL:(i + 1) * D_MODEL])
        t = _dot(br_ref[...], wb_ref[i]) * (1.0 / (1.0 + jnp.exp(-gz)))
        y = t if y is None else y + t
    yo = _dot(y.astype(BF16), wo_ref[...])
    o_ref[...] = _layernorm(alpha * x + gate * yo) * g_ref[...] + b_ref[...]


def _merge(x2, mod3, oa, ob, oc, w_gate, w_branch, w_out, ln_g, ln_b, *, batch, seq, tm, alpha):
    rows = batch * seq
    nl = seq // tm
    tok = lambda w: pl.BlockSpec((tm, w), lambda i: (i, 0))
    full = _resident_spec
    return pl.pallas_call(
        functools.partial(_merge_kernel, alpha=alpha),
        grid=(rows // tm,),
        in_specs=[
            tok(D_MODEL),
            pl.BlockSpec((None, 1, mod3.shape[2]), lambda i: (i // nl, 0, 0)),
            tok(512), tok(512),
            pl.BlockSpec((tm, 512), lambda i: (i % nl, i // nl)),
            full(w_gate), full(w_branch), full(w_out), full(ln_g), full(ln_b),
        ],
        out_specs=tok(D_MODEL),
        out_shape=jax.ShapeDtypeStruct((rows, D_MODEL), F32),
        compiler_params=_cparams("arbitrary"),
        name="merge",
    )(x2, mod3, oa, ob, oc, w_gate, w_branch, w_out, ln_g, ln_b)


FF_CHUNKS = ((0, 768), (768, 1536), (1536, 2304), (2304, 2816))


def _ffn_kernel(x_ref, mod_ref, wgu_ref, wd_ref, g_ref, b_ref, o_ref, *, alpha):
    x = x_ref[...]
    mod = mod_ref[...]
    sh = mod[:, 3 * D_MODEL:4 * D_MODEL]
    sc = mod[:, 4 * D_MODEL:5 * D_MODEL]
    gate = mod[:, 5 * D_MODEL:6 * D_MODEL]
    h = (_layernorm(x) * (1.0 + sc) + sh).astype(BF16)
    f = None
    for s, e in FF_CHUNKS:
        g = _dot(h, wgu_ref[:, s:e])
        u = _dot(h, wgu_ref[:, D_FF + s:D_FF + e])
        a = (g * (1.0 / (1.0 + jnp.exp(-g))) * u).astype(BF16)
        t = _dot(a, wd_ref[s:e, :])
        f = t if f is None else f + t
    o_ref[...] = _layernorm(alpha * x + gate * f) * g_ref[...] + b_ref[...]


def _ffn(x2, mod3, w_gu, w_down, ln_g, ln_b, *, batch, seq, tm, alpha):
    rows = batch * seq
    nl = seq // tm
    tok = pl.BlockSpec((tm, D_MODEL), lambda i: (i, 0))
    full = _resident_spec
    return pl.pallas_call(
        functools.partial(_ffn_kernel, alpha=alpha),
        grid=(rows // tm,),
        in_specs=[tok, pl.BlockSpec((None, 1, mod3.shape[2]), lambda i: (i // nl, 0, 0)),
                  full(w_gu), full(w_down), full(ln_g), full(ln_b)],
        out_specs=tok,
        out_shape=jax.ShapeDtypeStruct((rows, D_MODEL), F32),
        compiler_params=_cparams("arbitrary"),
        name="ffn",
    )(x2, mod3, w_gu, w_down, ln_g, ln_b)


def _rope_tables(seq):
    quarter = HEAD_DIM // 4
    pos = np.arange(seq)
    freqs = ROPE_THETA ** (-np.arange(quarter, dtype=np.float64) / quarter)
    ar = (pos // GRID_W)[:, None] * freqs
    ac = (pos % GRID_W)[:, None] * freqs
    cos64 = np.concatenate([np.cos(ar), np.cos(ar), np.cos(ac), np.cos(ac)], axis=1)
    sin64 = np.concatenate([-np.sin(ar), np.sin(ar), -np.sin(ac), np.sin(ac)], axis=1)
    return jnp.asarray(np.tile(cos64, (1, 2)), dtype=F32), jnp.asarray(np.tile(sin64, (1, 2)), dtype=F32)


def _head_mean_matrix(width):
    idx = np.arange(width) // HEAD_DIM
    return jnp.asarray((idx[:, None] == idx[None, :]).astype(np.float32) / HEAD_DIM).astype(BF16)


def _channel_dft_matrix():
    idx = np.arange(C_GROUP_W)
    ang = (2.0 * np.pi / C_GROUP_W) * ((idx[:, None] * idx[None, :]) % C_GROUP_W)
    scale = C_GROUP_W ** -0.5
    mat = np.concatenate([np.cos(ang), np.sin(ang)], axis=1) * scale
    return jnp.asarray(mat, dtype=F32).astype(BF16)


def kernel(x, c, ctx, c_ctx, w_ada, b_ada, w_in, q_norm, k_norm, rpb, w_branch, w_out,
           ln1_g, ln1_b, w_gu, w_down, ln2_g, ln2_b):
    batch, seq, d = x.shape
    ctx_len = ctx.shape[1]
    depth = w_ada.shape[0]
    assert d == D_MODEL and seq % (B_UNROLL * Q_ROWS * GRID_W) == 0 and seq // GRID_W >= WIN_ROWS
    assert ctx_len % LANES == 0
    alpha = (2.0 * depth) ** 0.25
    tm = 512
    tmc = min(ctx_len, 512)

    pad = (-(batch + 1)) % 8
    cc = jnp.concatenate([c, c_ctx[None, :], jnp.zeros((pad, d), F32)], axis=0)
    mods = _ada(cc, w_ada, b_ada)

    cos_t, sin_t = _rope_tables(seq)
    cos_c = jnp.ones((ctx_len, LANES), F32)
    sin_c = jnp.zeros((ctx_len, LANES), F32)
    g512 = _head_mean_matrix(512)
    g128 = _head_mean_matrix(128)
    dft_c = _channel_dft_matrix()
    tabs_l = _dft_factor_tables(seq)
    tabs_c = _dft_factor_tables(ctx_len)

    xl = x.reshape(batch * seq, d)
    xc = ctx.reshape(batch * ctx_len, d)
    for l in range(depth):
        with_ctx = l < depth - 1
        w_qkvu = w_in[l, :, :N_QKVU].astype(BF16)
        w_gate = w_in[l, :, N_QKVU:].astype(BF16)
        wb = w_branch[l].astype(BF16)
        wo = w_out[l].astype(BF16)
        wgu = w_gu[l].astype(BF16)
        wd = w_down[l].astype(BF16)
        q_gain = jnp.tile(q_norm[l] * Q_SCALE, A_HEADS)[None, :]
        k_gain = jnp.tile(k_norm[l], A_KV_HEADS)[None, :]
        mod_l = mods[l, :batch][:, None, :]
        mod_c = jnp.broadcast_to(mods[l, batch][None, None, :], (batch, 1, 6 * d))
        ln1 = (ln1_g[l][None, :], ln1_b[l][None, :])
        ln2 = (ln2_g[l][None, :], ln2_b[l][None, :])

        qa, ka, va1, qb, kb, vb1, uc, us = _inproj(
            xl, mod_l, w_qkvu, q_gain, k_gain, cos_t, sin_t, g512, g128, dft_c, batch=batch, seq=seq, tm=tm)
        qac, kac, vac1, qbc, kbc, vbc1, ucc, usc = _inproj(
            xc, mod_c, w_qkvu, q_gain, k_gain, cos_c, sin_c, g512, g128, dft_c, batch=batch, seq=ctx_len, tm=tmc)

        oa = _attn_a(qa, ka, va1, kac, vac1, batch=batch, seq=seq, ctx_len=ctx_len, tq=256, tk=256)
        bias_t = _neighbourhood_bias_t(rpb[l], seq // GRID_W)
        ob = _attn_b(qb, kb, vb1, kbc, vbc1, bias_t, batch=batch, seq=seq, ctx_len=ctx_len, tk=512)
        oc = _dft(tabs_l, uc, us)

        x1 = _merge(xl, mod_l, oa, ob, oc, w_gate, wb, wo, *ln1, batch=batch, seq=seq, tm=tm, alpha=alpha)
        xl = _ffn(x1, mod_l, wgu, wd, *ln2, batch=batch, seq=seq, tm=tm, alpha=alpha)

        if with_ctx:
            oac, obc = _attn_ctx(qac, kac, vac1, qbc, kbc, vbc1, batch=batch, ctx_len=ctx_len)
            occ = _dft(tabs_c, ucc, usc)
            xc1 = _merge(xc, mod_c, oac, obc, occ, w_gate, wb, wo, *ln1,
                         batch=batch, seq=ctx_len, tm=tmc, alpha=alpha)
            xc = _ffn(xc1, mod_c, wgu, wd, *ln2, batch=batch, seq=ctx_len, tm=tmc, alpha=alpha)
    return xl.reshape(batch, seq, d)
```

```python
import functools
import math

import numpy as np
import jax
import jax.numpy as jnp
from jax import lax
from jax.experimental import pallas as pl
from jax.experimental.pallas import tpu as pltpu

F32 = jnp.float32
BF16 = jnp.bfloat16

D_MODEL = 1024
GRID_W = 64
HEAD_DIM = 64
A_HEADS = 8
A_KV_HEADS = 2
A_GROUP = A_HEADS // A_KV_HEADS
B_HEADS = 8
NA_ROWS = 8
NA_COLS = 16
C_GROUPS = 4
C_GROUP_W = 128
BRANCH_W = 512
N_BRANCH = 3
D_FF = 2816
ROPE_THETA = 10000.0
LN_EPS = 1e-6
RMS_EPS = 1e-6
N_QKVU = 2816
Q_ROWS = 8
WIN_ROWS = 16
B_UNROLL = 4
ROW_BLOCK = 256
NEG = -1e30
LOG2E = math.log2(math.e)
Q_SCALE = HEAD_DIM ** -0.5 * LOG2E
LANES = 128
VMEM_LIMIT = 56 * 1024 * 1024


def _cparams(*sem):
    return pltpu.CompilerParams(dimension_semantics=sem, vmem_limit_bytes=VMEM_LIMIT)


def _resident_spec(a):
    return pl.BlockSpec(a.shape, lambda *_: (0,) * a.ndim, pipeline_mode=pl.Buffered(1))


def _dot(a, b):
    return jnp.dot(a, b, preferred_element_type=F32)


def _dot_nt(a, b):
    return lax.dot_general(a, b, (((1,), (1,)), ((), ())), preferred_element_type=F32)


def _dot_tn(a, b):
    return lax.dot_general(a, b, (((0,), (0,)), ((), ())), preferred_element_type=F32)


def _row_blocks(rows):
    return [slice(r, r + ROW_BLOCK) for r in range(0, rows, ROW_BLOCK)]


def _layernorm(x):
    mu = jnp.mean(x, axis=-1, keepdims=True)
    xc = x - mu
    var = jnp.mean(xc * xc, axis=-1, keepdims=True)
    return xc * lax.rsqrt(var + LN_EPS)


def _ada_kernel(c_ref, w_ref, b_ref, o_ref):
    c = c_ref[...]
    a = c * (1.0 / (1.0 + jnp.exp(-c)))
    a_hi = a.astype(BF16)
    a_lo = (a - a_hi.astype(F32)).astype(BF16)
    w = w_ref[...]
    w_hi = w.astype(BF16)
    w_lo = (w - w_hi.astype(F32)).astype(BF16)
    o_ref[...] = _dot(a_hi, w_hi) + _dot(a_hi, w_lo) + _dot(a_lo, w_hi) + b_ref[...]


def _ada(cc, w_ada, b_ada):
    depth, d, n = w_ada.shape
    tn = 1536
    return pl.pallas_call(
        _ada_kernel,
        grid=(depth, n // tn),
        in_specs=[
            pl.BlockSpec((cc.shape[0], d), lambda l, j: (0, 0)),
            pl.BlockSpec((None, d, tn), lambda l, j: (l, 0, j)),
            pl.BlockSpec((None, 1, tn), lambda l, j: (l, 0, j)),
        ],
        out_specs=pl.BlockSpec((None, cc.shape[0], tn), lambda l, j: (l, 0, j)),
        out_shape=jax.ShapeDtypeStruct((depth, cc.shape[0], n), F32),
        compiler_params=_cparams("arbitrary", "arbitrary"),
        name="ada",
    )(cc, w_ada, b_ada.reshape(depth, 1, n))


def _swap16(x, lane_lo):
    up = pltpu.roll(x, 16, 1)
    dn = pltpu.roll(x, LANES - 16, 1)
    return jnp.where(lane_lo, dn, up)


def _with_ones(v):
    ones = jnp.ones((v.shape[0], HEAD_DIM), F32)
    parts = []
    for hd in range(v.shape[1] // HEAD_DIM):
        parts += [v[:, hd * HEAD_DIM:(hd + 1) * HEAD_DIM], ones]
    return jnp.concatenate(parts, axis=1).astype(BF16)


def _inproj_kernel(x_ref, mod_ref, w_ref, qg_ref, kg_ref, cos_ref, sin_ref, g512_ref, g128_ref, dft_ref,
                   qa_ref, ka_ref, va_ref, qb_ref, kb_ref, vb_ref, uc_ref, us_ref):
    mod = mod_ref[...]
    sh = mod[:, 0:D_MODEL]
    sc = mod[:, D_MODEL:2 * D_MODEL]
    h = (_layernorm(x_ref[...]) * (1.0 + sc) + sh).astype(BF16)

    cos = cos_ref[...]
    sin = sin_ref[...]
    lane_lo = (lax.broadcasted_iota(jnp.int32, cos.shape, 1) % 32) < 16

    def norm_rope(v, ms, gain):
        vn = v * lax.rsqrt(ms + RMS_EPS) * gain
        outs = []
        for j in range(v.shape[1] // LANES):
            t = vn[:, j * LANES:(j + 1) * LANES]
            outs.append(t * cos + _swap16(t, lane_lo) * sin)
        return outs[0] if len(outs) == 1 else jnp.concatenate(outs, axis=1)

    q = _dot(h, w_ref[:, 0:512])
    kv = _dot(h, w_ref[:, 512:768])
    qb_ref[...] = (_dot(h, w_ref[:, 768:1280]) * Q_SCALE).astype(BF16)
    kb_ref[...] = _dot(h, w_ref[:, 1280:1792]).astype(BF16)
    k = kv[:, 0:128]
    ms_q = _dot((q * q).astype(BF16), g512_ref[...])
    ms_k = _dot((k * k).astype(BF16), g128_ref[...])
    vb_ref[...] = _with_ones(_dot(h, w_ref[:, 1792:2304]))
    u = _dot(h, w_ref[:, 2304:2816]).astype(BF16)
    qa_ref[...] = norm_rope(q, ms_q, qg_ref[...]).astype(BF16)
    ka_ref[...] = norm_rope(k, ms_k, kg_ref[...]).astype(BF16)
    va_ref[...] = _with_ones(kv[:, 128:256])
    dft = dft_ref[...]
    for g in range(C_GROUPS):
        z = _dot(u[:, g * C_GROUP_W:(g + 1) * C_GROUP_W], dft)
        uc_ref[:, g * C_GROUP_W:(g + 1) * C_GROUP_W] = z[:, 0:C_GROUP_W].astype(BF16)
        us_ref[:, g * C_GROUP_W:(g + 1) * C_GROUP_W] = z[:, C_GROUP_W:2 * C_GROUP_W].astype(BF16)


def _inproj(x2, mod3, w_qkvu, q_gain, k_gain, cos_t, sin_t, g512, g128, dft_c, *, batch, seq, tm):
    rows = batch * seq
    nl = seq // tm
    tok = lambda w: pl.BlockSpec((tm, w), lambda i: (i, 0))
    full = _resident_spec
    fo = pl.BlockSpec((tm, 512), lambda i: (i % nl, i // nl))
    o_tok = lambda w: jax.ShapeDtypeStruct((rows, w), BF16)
    return pl.pallas_call(
        _inproj_kernel,
        grid=(rows // tm,),
        in_specs=[
            tok(D_MODEL),
            pl.BlockSpec((None, 1, mod3.shape[2]), lambda i: (i // nl, 0, 0)),
            full(w_qkvu), full(q_gain), full(k_gain),
            pl.BlockSpec((tm, LANES), lambda i: (i % nl, 0)),
            pl.BlockSpec((tm, LANES), lambda i: (i % nl, 0)),
            full(g512), full(g128), full(dft_c),
        ],
        out_specs=[tok(512), tok(128), tok(256), tok(512), tok(512), tok(1024), fo, fo],
        out_shape=[o_tok(512), o_tok(128), o_tok(256), o_tok(512), o_tok(512), o_tok(1024),
                   jax.ShapeDtypeStruct((seq, batch * 512), BF16),
                   jax.ShapeDtypeStruct((seq, batch * 512), BF16)],
        compiler_params=_cparams("arbitrary"),
        name="inproj",
    )(x2, mod3, w_qkvu, q_gain, k_gain, cos_t, sin_t, g512, g128, dft_c)


def _attend_t_multi(qps, chunk_lists):
    n = len(qps)
    m = [None] * n
    acc = [None] * n
    units = [(i, c) for c in range(len(chunk_lists[0])) for i in range(n)]

    def scores(u):
        i, c = u
        k, _, bias_t = chunk_lists[i][c]
        s = _dot_nt(k, qps[i])
        return s if bias_t is None else s + bias_t

    s_next = scores(units[0])
    for idx, (i, c) in enumerate(units):
        s = s_next
        if idx + 1 < len(units):
            s_next = scores(units[idx + 1])
        v1 = chunk_lists[i][c][1]
        mc = jnp.max(s, axis=0, keepdims=True)
        if m[i] is None:
            m[i] = mc
            acc[i] = _dot_tn(v1, jnp.exp2(s - mc).astype(BF16))
        else:
            m_new = jnp.maximum(m[i], mc)
            acc[i] = jnp.exp2(m[i] - m_new) * acc[i] + _dot_tn(v1, jnp.exp2(s - m_new).astype(BF16))
            m[i] = m_new
    return [a[0:HEAD_DIM, :] / a[HEAD_DIM:HEAD_DIM + 1, :] for a in acc]


def _place_head(q128, src_half, dst_half, lane_hi):
    x = q128.astype(F32)
    if src_half != dst_half:
        x = pltpu.roll(x, 64, 1)
    keep = lane_hi if dst_half == 1 else jnp.logical_not(lane_hi)
    return jnp.where(keep, x, 0.0).astype(BF16)


def _gqa_tile_t(q, chunks_fn, tq, lane_hi):
    qps = []
    for g in range(A_KV_HEADS):
        parts = []
        for j in range(A_GROUP):
            hd = A_GROUP * g + j
            parts.append(_place_head(q[:, LANES * (hd // 2):LANES * (hd // 2 + 1)], hd % 2, g, lane_hi))
        qps.append(jnp.concatenate(parts, axis=0))
    outs = _attend_t_multi(qps, [chunks_fn(g) for g in range(A_KV_HEADS)])
    heads_t = [o_t[:, j * tq:(j + 1) * tq] for o_t in outs for j in range(A_GROUP)]
    return jnp.concatenate(heads_t, axis=0).T


def _mha_pair_t(q128, chunks_fn, lane_hi):
    outs = _attend_t_multi([_place_head(q128, hh, hh, lane_hi) for hh in range(2)],
                           [chunks_fn(hh) for hh in range(2)])
    return jnp.concatenate(outs, axis=0).T


def _attn_a_kernel(q_ref, k_ref, v_ref, kc_ref, vc_ref, o_ref, *, tq, chunk_bounds):
    lane_hi = lax.broadcasted_iota(jnp.int32, (tq, LANES), 1) >= 64

    def chunks(g):
        sl = slice(g * LANES, (g + 1) * LANES)
        return ([(kc_ref[...], vc_ref[:, sl], None)]
                + [(k_ref[s:e, :], v_ref[s:e, sl], None) for s, e in chunk_bounds])

    o_ref[...] = _gqa_tile_t(q_ref[...], chunks, tq, lane_hi).astype(BF16)


def _attn_a(qa, ka, va1, kac, vac1, *, batch, seq, ctx_len, tq, tk):
    bounds = [(i, min(i + tk, seq)) for i in range(0, seq, tk)]
    nq = seq // tq
    return pl.pallas_call(
        functools.partial(_attn_a_kernel, tq=tq, chunk_bounds=bounds),
        grid=(batch, nq),
        in_specs=[pl.BlockSpec((tq, 512), lambda b, i: (b * nq + i, 0)),
                  pl.BlockSpec((seq, LANES), lambda b, i: (b, 0)),
                  pl.BlockSpec((seq, 2 * LANES), lambda b, i: (b, 0)),
                  pl.BlockSpec((ctx_len, LANES), lambda b, i: (b, 0)),
                  pl.BlockSpec((ctx_len, 2 * LANES), lambda b, i: (b, 0))],
        out_specs=pl.BlockSpec((tq, 512), lambda b, i: (b * nq + i, 0)),
        out_shape=jax.ShapeDtypeStruct((batch * seq, 512), BF16),
        compiler_params=_cparams("arbitrary", "arbitrary"),
        name="attn_a",
    )(qa, ka, va1, kac, vac1)


def _attn_b_kernel(q_ref, k_ref, v_ref, kc_ref, vc_ref, bias_ref, o_ref, *, rows_n, tk):
    nq = Q_ROWS * GRID_W
    nw = WIN_ROWS * GRID_W
    n_groups = rows_n // Q_ROWS
    lane_hi = lax.broadcasted_iota(jnp.int32, (nq, LANES), 1) >= 64
    kc = kc_ref[...]

    def body(t, carry):
        qps, chunk_lists, q0s = [], [], []
        for u in range(B_UNROLL):
            g = t * B_UNROLL + u
            w0 = jnp.clip(Q_ROWS * g - (WIN_ROWS - Q_ROWS) // 2, 0, rows_n - WIN_ROWS)
            var = jnp.where(g == 0, 0, jnp.where(g == n_groups - 1, 2, 1))
            q0 = pl.multiple_of(g * nq, nq)
            k0 = pl.multiple_of(w0 * GRID_W, GRID_W)
            q128 = q_ref[pl.ds(q0, nq), :]
            q0s.append(q0)
            for hh in range(2):
                sl = slice(hh * LANES, (hh + 1) * LANES)
                qps.append(_place_head(q128, hh, hh, lane_hi))
                chunk_lists.append(
                    [(k_ref[pl.ds(k0 + s, tk), :], v_ref[pl.ds(k0 + s, tk), sl], bias_ref[var, hh, s:s + tk, :])
                     for s in range(0, nw, tk)] + [(kc, vc_ref[:, sl], None)])
        outs = _attend_t_multi(qps, chunk_lists)
        for u in range(B_UNROLL):
            o_ref[pl.ds(q0s[u], nq), :] = jnp.concatenate(outs[2 * u:2 * u + 2], axis=0).T.astype(BF16)
        return carry

    lax.fori_loop(0, n_groups // B_UNROLL, body, 0)


def _attn_b(qb, kb, vb1, kbc, vbc1, bias_t, *, batch, seq, ctx_len, tk):
    rows_n = seq // GRID_W
    nq = Q_ROWS * GRID_W
    nw = WIN_ROWS * GRID_W
    lat = lambda w: pl.BlockSpec((seq, w), lambda hp, b: (b, hp))
    ctx = lambda w: pl.BlockSpec((ctx_len, w), lambda hp, b: (b, hp))
    return pl.pallas_call(
        functools.partial(_attn_b_kernel, rows_n=rows_n, tk=tk),
        grid=(B_HEADS // 2, batch),
        in_specs=[lat(LANES), lat(LANES), lat(2 * LANES), ctx(LANES), ctx(2 * LANES),
                  pl.BlockSpec((3, 2, nw, nq), lambda hp, b: (0, hp, 0, 0))],
        out_specs=lat(LANES),
        out_shape=jax.ShapeDtypeStruct((batch * seq, 512), BF16),
        compiler_params=_cparams("arbitrary", "arbitrary"),
        name="attn_b",
    )(qb, kb, vb1, kbc, vbc1, bias_t)


def _neighbourhood_bias_t(rpb, rows_n):
    kc = NA_COLS
    cols = np.arange(GRID_W)
    c0 = np.clip(cols - kc // 2, 0, GRID_W - kc)
    col_ok = (cols[None, :] >= c0[:, None]) & (cols[None, :] < c0[:, None] + kc)
    heads, n_dr, n_dc = rpb.shape
    dc = cols[:, None] - cols[None, :] + NA_COLS - 1
    onehot = (np.arange(n_dc)[:, None, None] == dc[None]).astype(np.float32)
    t = jnp.einsum("hdj,jxy->hdxy", rpb * LOG2E, jnp.asarray(onehot), precision=lax.Precision.HIGHEST)
    t = jnp.where(col_ok.T[None, None], t, NEG)
    pad = Q_ROWS
    e_max = n_dr - 1 + 2 * pad
    t = jnp.pad(t, ((0, 0), (pad, pad), (0, 0), (0, 0)))[:, ::-1]
    rev = t.transpose(0, 2, 1, 3).reshape(heads, GRID_W, (e_max + 1) * GRID_W)
    n_groups = rows_n // Q_ROWS
    valid = np.zeros((3, 1, WIN_ROWS, 1, Q_ROWS, 1), bool)
    variants = []
    for v, g in enumerate((0, min(1, n_groups - 1), n_groups - 1)):
        w0 = int(np.clip(Q_ROWS * g - (WIN_ROWS - Q_ROWS) // 2, 0, rows_n - WIN_ROWS))
        strips = []
        for i in range(WIN_ROWS):
            d0 = w0 + i - Q_ROWS * g + NA_ROWS - 1
            e0 = e_max - pad - d0
            strips.append(rev[:, :, e0 * GRID_W:(e0 + Q_ROWS) * GRID_W])
            for a in range(Q_ROWS):
                r0 = int(np.clip(Q_ROWS * g + a - NA_ROWS // 2, 0, rows_n - NA_ROWS))
                valid[v, 0, i, 0, a, 0] = r0 <= w0 + i < r0 + NA_ROWS
        variants.append(jnp.stack(strips, axis=1))
    valid = np.repeat(valid.reshape(3, 1, WIN_ROWS, 1, Q_ROWS), GRID_W, axis=4)
    table = jnp.where(jnp.asarray(valid), jnp.stack(variants, axis=0), NEG)
    return table.reshape(3, heads, WIN_ROWS * GRID_W, Q_ROWS * GRID_W)


def _attn_ctx_kernel(qa_ref, ka_ref, va_ref, qb_ref, kb_ref, vb_ref, oa_ref, ob_ref, *, lc):
    lane_hi = lax.broadcasted_iota(jnp.int32, (lc, LANES), 1) >= 64
    ka = ka_ref[...]
    oa_ref[...] = _gqa_tile_t(
        qa_ref[...], lambda g: [(ka, va_ref[:, g * LANES:(g + 1) * LANES], None)], lc, lane_hi).astype(BF16)
    for hp in range(B_HEADS // 2):
        sl = slice(hp * LANES, (hp + 1) * LANES)
        k = kb_ref[:, sl]
        chunks = lambda hh: [(k, vb_ref[:, (2 * hp + hh) * LANES:(2 * hp + hh + 1) * LANES], None)]
        ob_ref[:, sl] = _mha_pair_t(qb_ref[:, sl], chunks, lane_hi).astype(BF16)


def _attn_ctx(qac, kac, vac1, qbc, kbc, vbc1, *, batch, ctx_len):
    t = lambda w: pl.BlockSpec((ctx_len, w), lambda b: (b, 0))
    o = jax.ShapeDtypeStruct((batch * ctx_len, 512), BF16)
    return pl.pallas_call(
        functools.partial(_attn_ctx_kernel, lc=ctx_len),
        grid=(batch,),
        in_specs=[t(512), t(128), t(256), t(512), t(512), t(1024)],
        out_specs=[t(512), t(512)],
        out_shape=[o, o],
        compiler_params=_cparams("arbitrary"),
        name="attn_ctx",
    )(qac, kac, vac1, qbc, kbc, vbc1)


def _dft_kernel(ac_ref, as_ref, bc_ref, bs_ref, uc_ref, us_ref, o_ref, ct_ref, st_ref, *, bm):
    i = pl.program_id(0)

    @pl.when(pl.program_id(1) == 0)
    def _():
        bc = bc_ref[...]
        bs = bs_ref[...]
        for r in range(bm // GRID_W):
            a_c = ac_ref[pl.ds(i * (bm // GRID_W) + r, 1), :]
            a_s = as_ref[pl.ds(i * (bm // GRID_W) + r, 1), :]
            ct_ref[r * GRID_W:(r + 1) * GRID_W, :] = (a_c * bc - a_s * bs).astype(BF16)
            st_ref[r * GRID_W:(r + 1) * GRID_W, :] = (a_s * bc + a_c * bs).astype(BF16)

    o_ref[...] = (_dot(ct_ref[...], uc_ref[...]) - _dot(st_ref[...], us_ref[...])).astype(BF16)


def _dft(tabs, uc, us):
    n, width = uc.shape
    bm = min(n, 1024)
    bn = min(width, 512)
    u_spec = pl.BlockSpec((n, bn), lambda i, j: (0, j))
    return pl.pallas_call(
        functools.partial(_dft_kernel, bm=bm),
        grid=(n // bm, width // bn),
        in_specs=[_resident_spec(t) for t in tabs] + [u_spec, u_spec],
        out_specs=pl.BlockSpec((bm, bn), lambda i, j: (i, j)),
        out_shape=jax.ShapeDtypeStruct((n, width), BF16),
        scratch_shapes=[pltpu.VMEM((bm, n), BF16), pltpu.VMEM((bm, n), BF16)],
        compiler_params=_cparams("arbitrary", "arbitrary"),
        name="dft",
    )(*tabs, uc, us)


def _dft_factor_tables(n):
    r = n // GRID_W
    k = np.arange(n, dtype=np.int64)
    ang_a = (2.0 * np.pi / r) * ((np.arange(r)[:, None] * k[None, :]) % r)
    ang_b = (2.0 * np.pi / n) * ((np.arange(GRID_W)[:, None] * k[None, :]) % n)
    pad = ((0, (-r) % 8), (0, 0))
    scale = n ** -0.5
    return tuple(jnp.asarray(t, dtype=F32) for t in (
        np.pad(np.cos(ang_a), pad), np.pad(np.sin(ang_a), pad), np.cos(ang_b) * scale, np.sin(ang_b) * scale))


def _merge_kernel(x_ref, mod_ref, oa_ref, ob_ref, oc_ref, wg_ref, wb_ref, wo_ref, g_ref, b_ref, o_ref, *, alpha):
    mod = mod_ref[...]
    sh = mod[:, 0:D_MODEL]
    sc = mod[:, D_MODEL:2 * D_MODEL]
    gate = mod[:, 2 * D_MODEL:3 * D_MODEL]
    blocks = _row_blocks(x_ref.shape[0])
    xs = [x_ref[b, :] for b in blocks]
    hs = [(_layernorm(x) * (1.0 + sc) + sh).astype(BF16) for x in xs]
    ys = [None] * len(blocks)
    for i, br_ref in enumerate((oa_ref, ob_ref, oc_ref)):
        for r, b in enumerate(blocks):
            gz = _dot(hs[r], wg_ref[:, i * D_MODEL:(i + 1) * D_MODEL])
            t = _dot(br_ref[b, :], wb_ref[i]) * (1.0 / (1.0 + jnp.exp(-gz)))
            ys[r] = t if ys[r] is None else ys[r] + t
    for r, b in enumerate(blocks):
        yo = _dot(ys[r].astype(BF16), wo_ref[...])
        o_ref[b, :] = _layernorm(alpha * xs[r] + gate * yo) * g_ref[...] + b_ref[...]


def _merge(x2, mod3, oa, ob, oc, w_gate, w_branch, w_out, ln_g, ln_b, *, batch, seq, tm, alpha):
    rows = batch * seq
    nl = seq // tm
    tok = lambda w: pl.BlockSpec((tm, w), lambda i: (i, 0))
    full = _resident_spec
    return pl.pallas_call(
        functools.partial(_merge_kernel, alpha=alpha),
        grid=(rows // tm,),
        in_specs=[
            tok(D_MODEL),
            pl.BlockSpec((None, 1, mod3.shape[2]), lambda i: (i // nl, 0, 0)),
            tok(512), tok(512),
            pl.BlockSpec((tm, 512), lambda i: (i % nl, i // nl)),
            full(w_gate), full(w_branch), full(w_out), full(ln_g), full(ln_b),
        ],
        out_specs=tok(D_MODEL),
        out_shape=jax.ShapeDtypeStruct((rows, D_MODEL), F32),
        compiler_params=_cparams("arbitrary"),
        name="merge",
    )(x2, mod3, oa, ob, oc, w_gate, w_branch, w_out, ln_g, ln_b)


FF_CHUNKS = ((0, 768), (768, 1536), (1536, 2304), (2304, 2816))


def _ffn_kernel(x_ref, mod_ref, wgu_ref, wd_ref, g_ref, b_ref, o_ref, *, alpha):
    mod = mod_ref[...]
    sh = mod[:, 3 * D_MODEL:4 * D_MODEL]
    sc = mod[:, 4 * D_MODEL:5 * D_MODEL]
    gate = mod[:, 5 * D_MODEL:6 * D_MODEL]
    blocks = _row_blocks(x_ref.shape[0])
    xs = [x_ref[b, :] for b in blocks]
    hs = [(_layernorm(x) * (1.0 + sc) + sh).astype(BF16) for x in xs]
    fs = [None] * len(blocks)
    for s, e in FF_CHUNKS:
        for i in range(len(blocks)):
            g = _dot(hs[i], wgu_ref[:, s:e])
            u = _dot(hs[i], wgu_ref[:, D_FF + s:D_FF + e])
            a = (g * (1.0 / (1.0 + jnp.exp(-g))) * u).astype(BF16)
            t = _dot(a, wd_ref[s:e, :])
            fs[i] = t if fs[i] is None else fs[i] + t
    for i, b in enumerate(blocks):
        o_ref[b, :] = _layernorm(alpha * xs[i] + gate * fs[i]) * g_ref[...] + b_ref[...]


def _ffn(x2, mod3, w_gu, w_down, ln_g, ln_b, *, batch, seq, tm, alpha):
    rows = batch * seq
    nl = seq // tm
    tok = pl.BlockSpec((tm, D_MODEL), lambda i: (i, 0))
    full = _resident_spec
    return pl.pallas_call(
        functools.partial(_ffn_kernel, alpha=alpha),
        grid=(rows // tm,),
        in_specs=[tok, pl.BlockSpec((None, 1, mod3.shape[2]), lambda i: (i // nl, 0, 0)),
                  full(w_gu), full(w_down), full(ln_g), full(ln_b)],
        out_specs=tok,
        out_shape=jax.ShapeDtypeStruct((rows, D_MODEL), F32),
        compiler_params=_cparams("arbitrary"),
        name="ffn",
    )(x2, mod3, w_gu, w_down, ln_g, ln_b)


def _rope_tables(seq):
    quarter = HEAD_DIM // 4
    pos = np.arange(seq)
    freqs = ROPE_THETA ** (-np.arange(quarter, dtype=np.float64) / quarter)
    ar = (pos // GRID_W)[:, None] * freqs
    ac = (pos % GRID_W)[:, None] * freqs
    cos64 = np.concatenate([np.cos(ar), np.cos(ar), np.cos(ac), np.cos(ac)], axis=1)
    sin64 = np.concatenate([-np.sin(ar), np.sin(ar), -np.sin(ac), np.sin(ac)], axis=1)
    return jnp.asarray(np.tile(cos64, (1, 2)), dtype=F32), jnp.asarray(np.tile(sin64, (1, 2)), dtype=F32)


def _head_mean_matrix(width):
    idx = np.arange(width) // HEAD_DIM
    return jnp.asarray((idx[:, None] == idx[None, :]).astype(np.float32) / HEAD_DIM).astype(BF16)


def _channel_dft_matrix():
    idx = np.arange(C_GROUP_W)
    ang = (2.0 * np.pi / C_GROUP_W) * ((idx[:, None] * idx[None, :]) % C_GROUP_W)
    scale = C_GROUP_W ** -0.5
    mat = np.concatenate([np.cos(ang), np.sin(ang)], axis=1) * scale
    return jnp.asarray(mat, dtype=F32).astype(BF16)


def kernel(x, c, ctx, c_ctx, w_ada, b_ada, w_in, q_norm, k_norm, rpb, w_branch, w_out,
           ln1_g, ln1_b, w_gu, w_down, ln2_g, ln2_b):
    batch, seq, d = x.shape
    ctx_len = ctx.shape[1]
    depth = w_ada.shape[0]
    assert d == D_MODEL and seq % (B_UNROLL * Q_ROWS * GRID_W) == 0 and seq // GRID_W >= WIN_ROWS
    assert ctx_len % LANES == 0
    alpha = (2.0 * depth) ** 0.25
    tm = 1024
    tmc = min(ctx_len, 1024)
    assert seq % tm == 0 and tm % ROW_BLOCK == 0 and tmc % ROW_BLOCK == 0

    pad = (-(batch + 1)) % 8
    cc = jnp.concatenate([c, c_ctx[None, :], jnp.zeros((pad, d), F32)], axis=0)
    mods = _ada(cc, w_ada, b_ada)

    cos_t, sin_t = _rope_tables(seq)
    cos_c = jnp.ones((ctx_len, LANES), F32)
    sin_c = jnp.zeros((ctx_len, LANES), F32)
    g512 = _head_mean_matrix(512)
    g128 = _head_mean_matrix(128)
    dft_c = _channel_dft_matrix()
    tabs_l = _dft_factor_tables(seq)
    tabs_c = _dft_factor_tables(ctx_len)

    xl = x.reshape(batch * seq, d)
    xc = ctx.reshape(batch * ctx_len, d)
    for l in range(depth):
        with_ctx = l < depth - 1
        w_qkvu = w_in[l, :, :N_QKVU].astype(BF16)
        w_gate = w_in[l, :, N_QKVU:].astype(BF16)
        wb = w_branch[l].astype(BF16)
        wo = w_out[l].astype(BF16)
        wgu = w_gu[l].astype(BF16)
        wd = w_down[l].astype(BF16)
        q_gain = jnp.tile(q_norm[l] * Q_SCALE, A_HEADS)[None, :]
        k_gain = jnp.tile(k_norm[l], A_KV_HEADS)[None, :]
        mod_l = mods[l, :batch][:, None, :]
        mod_c = jnp.broadcast_to(mods[l, batch][None, None, :], (batch, 1, 6 * d))
        ln1 = (ln1_g[l][None, :], ln1_b[l][None, :])
        ln2 = (ln2_g[l][None, :], ln2_b[l][None, :])

        qa, ka, va1, qb, kb, vb1, uc, us = _inproj(
            xl, mod_l, w_qkvu, q_gain, k_gain, cos_t, sin_t, g512, g128, dft_c, batch=batch, seq=seq, tm=tm)
        qac, kac, vac1, qbc, kbc, vbc1, ucc, usc = _inproj(
            xc, mod_c, w_qkvu, q_gain, k_gain, cos_c, sin_c, g512, g128, dft_c, batch=batch, seq=ctx_len, tm=tmc)

        oa = _attn_a(qa, ka, va1, kac, vac1, batch=batch, seq=seq, ctx_len=ctx_len, tq=256, tk=256)
        bias_t = _neighbourhood_bias_t(rpb[l], seq // GRID_W)
        ob = _attn_b(qb, kb, vb1, kbc, vbc1, bias_t, batch=batch, seq=seq, ctx_len=ctx_len, tk=512)
        oc = _dft(tabs_l, uc, us)

        x1 = _merge(xl, mod_l, oa, ob, oc, w_gate, wb, wo, *ln1, batch=batch, seq=seq, tm=tm, alpha=alpha)
        xl = _ffn(x1, mod_l, wgu, wd, *ln2, batch=batch, seq=seq, tm=tm, alpha=alpha)

        if with_ctx:
            oac, obc = _attn_ctx(qac, kac, vac1, qbc, kbc, vbc1, batch=batch, ctx_len=ctx_len)
            occ = _dft(tabs_c, ucc, usc)
            xc1 = _merge(xc, mod_c, oac, obc, occ, w_gate, wb, wo, *ln1,
                         batch=batch, seq=ctx_len, tm=tmc, alpha=alpha)
            xc = _ffn(xc1, mod_c, wgu, wd, *ln2, batch=batch, seq=ctx_len, tm=tmc, alpha=alpha)
    return xl.reshape(batch, seq, d)
```

```python
import functools
import math

import numpy as np
import jax
import jax.numpy as jnp
from jax import lax
from jax.experimental import pallas as pl
from jax.experimental.pallas import tpu as pltpu

F32 = jnp.float32
BF16 = jnp.bfloat16

D_MODEL = 1024
GRID_W = 64
HEAD_DIM = 64
A_HEADS = 8
A_KV_HEADS = 2
A_GROUP = A_HEADS // A_KV_HEADS
B_HEADS = 8
NA_ROWS = 8
NA_COLS = 16
C_GROUPS = 4
C_GROUP_W = 128
BRANCH_W = 512
N_BRANCH = 3
D_FF = 2816
ROPE_THETA = 10000.0
LN_EPS = 1e-6
RMS_EPS = 1e-6
N_QKVU = 2816
Q_ROWS = 8
WIN_ROWS = 16
B_UNROLL = 4
ROW_BLOCK = 256
MAX_SHIFT_BOUND = 40.0
NEG = -1e30
LOG2E = math.log2(math.e)
Q_SCALE = HEAD_DIM ** -0.5 * LOG2E
LANES = 128
VMEM_LIMIT = 56 * 1024 * 1024


def _cparams(*sem):
    return pltpu.CompilerParams(dimension_semantics=sem, vmem_limit_bytes=VMEM_LIMIT)


def _resident_spec(a):
    return pl.BlockSpec(a.shape, lambda *_: (0,) * a.ndim, pipeline_mode=pl.Buffered(1))


def _dot(a, b):
    return jnp.dot(a, b, preferred_element_type=F32)


def _dot_nt(a, b):
    return lax.dot_general(a, b, (((1,), (1,)), ((), ())), preferred_element_type=F32)


def _dot_tn(a, b):
    return lax.dot_general(a, b, (((0,), (0,)), ((), ())), preferred_element_type=F32)


def _row_blocks(rows):
    return [slice(r, r + ROW_BLOCK) for r in range(0, rows, ROW_BLOCK)]


def _layernorm(x):
    mu = jnp.mean(x, axis=-1, keepdims=True)
    xc = x - mu
    var = jnp.mean(xc * xc, axis=-1, keepdims=True)
    return xc * lax.rsqrt(var + LN_EPS)


def _ada_kernel(c_ref, w_ref, b_ref, o_ref):
    c = c_ref[...]
    a = c * (1.0 / (1.0 + jnp.exp(-c)))
    a_hi = a.astype(BF16)
    a_lo = (a - a_hi.astype(F32)).astype(BF16)
    w = w_ref[...]
    w_hi = w.astype(BF16)
    w_lo = (w - w_hi.astype(F32)).astype(BF16)
    o_ref[...] = _dot(a_hi, w_hi) + _dot(a_hi, w_lo) + _dot(a_lo, w_hi) + b_ref[...]


def _ada(cc, w_ada, b_ada):
    depth, d, n = w_ada.shape
    tn = 1536
    return pl.pallas_call(
        _ada_kernel,
        grid=(depth, n // tn),
        in_specs=[
            pl.BlockSpec((cc.shape[0], d), lambda l, j: (0, 0)),
            pl.BlockSpec((None, d, tn), lambda l, j: (l, 0, j)),
            pl.BlockSpec((None, 1, tn), lambda l, j: (l, 0, j)),
        ],
        out_specs=pl.BlockSpec((None, cc.shape[0], tn), lambda l, j: (l, 0, j)),
        out_shape=jax.ShapeDtypeStruct((depth, cc.shape[0], n), F32),
        compiler_params=_cparams("arbitrary", "arbitrary"),
        name="ada",
    )(cc, w_ada, b_ada.reshape(depth, 1, n))


def _swap16(x, lane_lo):
    up = pltpu.roll(x, 16, 1)
    dn = pltpu.roll(x, LANES - 16, 1)
    return jnp.where(lane_lo, dn, up)


def _with_ones(v):
    ones = jnp.ones((v.shape[0], HEAD_DIM), F32)
    parts = []
    for hd in range(v.shape[1] // HEAD_DIM):
        parts += [v[:, hd * HEAD_DIM:(hd + 1) * HEAD_DIM], ones]
    return jnp.concatenate(parts, axis=1).astype(BF16)


def _inproj_kernel(x_ref, mod_ref, w_ref, qg_ref, kg_ref, cos_ref, sin_ref, g512_ref, g128_ref, dft_ref,
                   qa_ref, ka_ref, va_ref, qb_ref, kb_ref, vb_ref, uc_ref, us_ref):
    mod = mod_ref[...]
    sh = mod[:, 0:D_MODEL]
    sc = mod[:, D_MODEL:2 * D_MODEL]
    h = (_layernorm(x_ref[...]) * (1.0 + sc) + sh).astype(BF16)

    cos = cos_ref[...]
    sin = sin_ref[...]
    lane_lo = (lax.broadcasted_iota(jnp.int32, cos.shape, 1) % 32) < 16

    def norm_rope(v, ms, gain):
        vn = v * lax.rsqrt(ms + RMS_EPS) * gain
        outs = []
        for j in range(v.shape[1] // LANES):
            t = vn[:, j * LANES:(j + 1) * LANES]
            outs.append(t * cos + _swap16(t, lane_lo) * sin)
        return outs[0] if len(outs) == 1 else jnp.concatenate(outs, axis=1)

    q = _dot(h, w_ref[:, 0:512])
    kv = _dot(h, w_ref[:, 512:768])
    qb_ref[...] = (_dot(h, w_ref[:, 768:1280]) * Q_SCALE).astype(BF16)
    kb_ref[...] = _dot(h, w_ref[:, 1280:1792]).astype(BF16)
    k = kv[:, 0:128]
    ms_q = _dot((q * q).astype(BF16), g512_ref[...])
    ms_k = _dot((k * k).astype(BF16), g128_ref[...])
    vb_ref[...] = _with_ones(_dot(h, w_ref[:, 1792:2304]))
    u = _dot(h, w_ref[:, 2304:2816]).astype(BF16)
    qa_ref[...] = norm_rope(q, ms_q, qg_ref[...]).astype(BF16)
    ka_ref[...] = norm_rope(k, ms_k, kg_ref[...]).astype(BF16)
    va_ref[...] = _with_ones(kv[:, 128:256])
    dft = dft_ref[...]
    for g in range(C_GROUPS):
        z = _dot(u[:, g * C_GROUP_W:(g + 1) * C_GROUP_W], dft)
        uc_ref[:, g * C_GROUP_W:(g + 1) * C_GROUP_W] = z[:, 0:C_GROUP_W].astype(BF16)
        us_ref[:, g * C_GROUP_W:(g + 1) * C_GROUP_W] = z[:, C_GROUP_W:2 * C_GROUP_W].astype(BF16)


def _inproj(x2, mod3, w_qkvu, q_gain, k_gain, cos_t, sin_t, g512, g128, dft_c, *, batch, seq, tm):
    rows = batch * seq
    nl = seq // tm
    tok = lambda w: pl.BlockSpec((tm, w), lambda i: (i, 0))
    full = _resident_spec
    fo = pl.BlockSpec((tm, 512), lambda i: (i % nl, i // nl))
    o_tok = lambda w: jax.ShapeDtypeStruct((rows, w), BF16)
    return pl.pallas_call(
        _inproj_kernel,
        grid=(rows // tm,),
        in_specs=[
            tok(D_MODEL),
            pl.BlockSpec((None, 1, mod3.shape[2]), lambda i: (i // nl, 0, 0)),
            full(w_qkvu), full(q_gain), full(k_gain),
            pl.BlockSpec((tm, LANES), lambda i: (i % nl, 0)),
            pl.BlockSpec((tm, LANES), lambda i: (i % nl, 0)),
            full(g512), full(g128), full(dft_c),
        ],
        out_specs=[tok(512), tok(128), tok(256), tok(512), tok(512), tok(1024), fo, fo],
        out_shape=[o_tok(512), o_tok(128), o_tok(256), o_tok(512), o_tok(512), o_tok(1024),
                   jax.ShapeDtypeStruct((seq, batch * 512), BF16),
                   jax.ShapeDtypeStruct((seq, batch * 512), BF16)],
        compiler_params=_cparams("arbitrary"),
        name="inproj",
    )(x2, mod3, w_qkvu, q_gain, k_gain, cos_t, sin_t, g512, g128, dft_c)


def _attend_t_multi(qps, chunk_lists, shift=None):
    n = len(qps)
    m = [None] * n
    acc = [None] * n
    units = [(i, c) for c in range(len(chunk_lists[0])) for i in range(n)]

    def scores(u):
        i, c = u
        k, _, bias_t = chunk_lists[i][c]
        s = _dot_nt(k, qps[i])
        return s if bias_t is None else s + bias_t

    s_next = scores(units[0])
    for idx, (i, c) in enumerate(units):
        s = s_next
        if idx + 1 < len(units):
            s_next = scores(units[idx + 1])
        v1 = chunk_lists[i][c][1]
        if shift is not None:
            d = _dot_tn(v1, jnp.exp2(s - shift).astype(BF16))
            acc[i] = d if acc[i] is None else acc[i] + d
            continue
        mc = jnp.max(s, axis=0, keepdims=True)
        if m[i] is None:
            m[i] = mc
            acc[i] = _dot_tn(v1, jnp.exp2(s - mc).astype(BF16))
        else:
            m_new = jnp.maximum(m[i], mc)
            acc[i] = jnp.exp2(m[i] - m_new) * acc[i] + _dot_tn(v1, jnp.exp2(s - m_new).astype(BF16))
            m[i] = m_new
    return [a[0:HEAD_DIM, :] / a[HEAD_DIM:HEAD_DIM + 1, :] for a in acc]


def _place_head(q128, src_half, dst_half, lane_hi):
    x = q128.astype(F32)
    if src_half != dst_half:
        x = pltpu.roll(x, 64, 1)
    keep = lane_hi if dst_half == 1 else jnp.logical_not(lane_hi)
    return jnp.where(keep, x, 0.0).astype(BF16)


def _gqa_tile_t(q, chunks_fn, tq, lane_hi, shift=None):
    qps = []
    for g in range(A_KV_HEADS):
        parts = []
        for j in range(A_GROUP):
            hd = A_GROUP * g + j
            parts.append(_place_head(q[:, LANES * (hd // 2):LANES * (hd // 2 + 1)], hd % 2, g, lane_hi))
        qps.append(jnp.concatenate(parts, axis=0))
    outs = _attend_t_multi(qps, [chunks_fn(g) for g in range(A_KV_HEADS)], shift)
    heads_t = [o_t[:, j * tq:(j + 1) * tq] for o_t in outs for j in range(A_GROUP)]
    return jnp.concatenate(heads_t, axis=0).T


def _mha_pair_t(q128, chunks_fn, lane_hi):
    outs = _attend_t_multi([_place_head(q128, hh, hh, lane_hi) for hh in range(2)],
                           [chunks_fn(hh) for hh in range(2)])
    return jnp.concatenate(outs, axis=0).T


def _attn_a_kernel(bound_ref, q_ref, k_ref, v_ref, kc_ref, vc_ref, o_ref, *, tq, chunk_bounds):
    lane_hi = lax.broadcasted_iota(jnp.int32, (tq, LANES), 1) >= 64

    def chunks(g):
        sl = slice(g * LANES, (g + 1) * LANES)
        return ([(kc_ref[...], vc_ref[:, sl], None)]
                + [(k_ref[s:e, :], v_ref[s:e, sl], None) for s, e in chunk_bounds])

    bound = bound_ref[0]
    o_ref[...] = lax.cond(
        bound <= MAX_SHIFT_BOUND,
        lambda: _gqa_tile_t(q_ref[...], chunks, tq, lane_hi, shift=bound),
        lambda: _gqa_tile_t(q_ref[...], chunks, tq, lane_hi),
    ).astype(BF16)


def _attn_a(score_bound, qa, ka, va1, kac, vac1, *, batch, seq, ctx_len, tq, tk):
    bounds = [(i, min(i + tk, seq)) for i in range(0, seq, tk)]
    nq = seq // tq
    return pl.pallas_call(
        functools.partial(_attn_a_kernel, tq=tq, chunk_bounds=bounds),
        grid=(batch, nq),
        in_specs=[pl.BlockSpec(memory_space=pltpu.SMEM),
                  pl.BlockSpec((tq, 512), lambda b, i: (b * nq + i, 0)),
                  pl.BlockSpec((seq, LANES), lambda b, i: (b, 0)),
                  pl.BlockSpec((seq, 2 * LANES), lambda b, i: (b, 0)),
                  pl.BlockSpec((ctx_len, LANES), lambda b, i: (b, 0)),
                  pl.BlockSpec((ctx_len, 2 * LANES), lambda b, i: (b, 0))],
        out_specs=pl.BlockSpec((tq, 512), lambda b, i: (b * nq + i, 0)),
        out_shape=jax.ShapeDtypeStruct((batch * seq, 512), BF16),
        compiler_params=_cparams("arbitrary", "arbitrary"),
        name="attn_a",
    )(score_bound, qa, ka, va1, kac, vac1)


def _attn_b_kernel(q_ref, k_ref, v_ref, kc_ref, vc_ref, bias_ref, o_ref, *, rows_n, tk):
    nq = Q_ROWS * GRID_W
    nw = WIN_ROWS * GRID_W
    n_groups = rows_n // Q_ROWS
    lane_hi = lax.broadcasted_iota(jnp.int32, (nq, LANES), 1) >= 64
    kc = kc_ref[...]

    def body(t, carry):
        qps, chunk_lists, q0s = [], [], []
        for u in range(B_UNROLL):
            g = t * B_UNROLL + u
            w0 = jnp.clip(Q_ROWS * g - (WIN_ROWS - Q_ROWS) // 2, 0, rows_n - WIN_ROWS)
            var = jnp.where(g == 0, 0, jnp.where(g == n_groups - 1, 2, 1))
            q0 = pl.multiple_of(g * nq, nq)
            k0 = pl.multiple_of(w0 * GRID_W, GRID_W)
            q128 = q_ref[pl.ds(q0, nq), :]
            q0s.append(q0)
            for hh in range(2):
                sl = slice(hh * LANES, (hh + 1) * LANES)
                qps.append(_place_head(q128, hh, hh, lane_hi))
                chunk_lists.append(
                    [(k_ref[pl.ds(k0 + s, tk), :], v_ref[pl.ds(k0 + s, tk), sl], bias_ref[var, hh, s:s + tk, :])
                     for s in range(0, nw, tk)] + [(kc, vc_ref[:, sl], None)])
        outs = _attend_t_multi(qps, chunk_lists)
        for u in range(B_UNROLL):
            o_ref[pl.ds(q0s[u], nq), :] = jnp.concatenate(outs[2 * u:2 * u + 2], axis=0).T.astype(BF16)
        return carry

    lax.fori_loop(0, n_groups // B_UNROLL, body, 0)


def _attn_b(qb, kb, vb1, kbc, vbc1, bias_t, *, batch, seq, ctx_len, tk):
    rows_n = seq // GRID_W
    nq = Q_ROWS * GRID_W
    nw = WIN_ROWS * GRID_W
    lat = lambda w: pl.BlockSpec((seq, w), lambda hp, b: (b, hp))
    ctx = lambda w: pl.BlockSpec((ctx_len, w), lambda hp, b: (b, hp))
    return pl.pallas_call(
        functools.partial(_attn_b_kernel, rows_n=rows_n, tk=tk),
        grid=(B_HEADS // 2, batch),
        in_specs=[lat(LANES), lat(LANES), lat(2 * LANES), ctx(LANES), ctx(2 * LANES),
                  pl.BlockSpec((3, 2, nw, nq), lambda hp, b: (0, hp, 0, 0))],
        out_specs=lat(LANES),
        out_shape=jax.ShapeDtypeStruct((batch * seq, 512), BF16),
        compiler_params=_cparams("arbitrary", "arbitrary"),
        name="attn_b",
    )(qb, kb, vb1, kbc, vbc1, bias_t)


def _neighbourhood_bias_t(rpb, rows_n):
    kc = NA_COLS
    cols = np.arange(GRID_W)
    c0 = np.clip(cols - kc // 2, 0, GRID_W - kc)
    col_ok = (cols[None, :] >= c0[:, None]) & (cols[None, :] < c0[:, None] + kc)
    heads, n_dr, n_dc = rpb.shape
    dc = cols[:, None] - cols[None, :] + NA_COLS - 1
    onehot = (np.arange(n_dc)[:, None, None] == dc[None]).astype(np.float32)
    t = jnp.einsum("hdj,jxy->hdxy", rpb * LOG2E, jnp.asarray(onehot), precision=lax.Precision.HIGHEST)
    t = jnp.where(col_ok.T[None, None], t, NEG)
    pad = Q_ROWS
    e_max = n_dr - 1 + 2 * pad
    t = jnp.pad(t, ((0, 0), (pad, pad), (0, 0), (0, 0)))[:, ::-1]
    rev = t.transpose(0, 2, 1, 3).reshape(heads, GRID_W, (e_max + 1) * GRID_W)
    n_groups = rows_n // Q_ROWS
    valid = np.zeros((3, 1, WIN_ROWS, 1, Q_ROWS, 1), bool)
    variants = []
    for v, g in enumerate((0, min(1, n_groups - 1), n_groups - 1)):
        w0 = int(np.clip(Q_ROWS * g - (WIN_ROWS - Q_ROWS) // 2, 0, rows_n - WIN_ROWS))
        strips = []
        for i in range(WIN_ROWS):
            d0 = w0 + i - Q_ROWS * g + NA_ROWS - 1
            e0 = e_max - pad - d0
            strips.append(rev[:, :, e0 * GRID_W:(e0 + Q_ROWS) * GRID_W])
            for a in range(Q_ROWS):
                r0 = int(np.clip(Q_ROWS * g + a - NA_ROWS // 2, 0, rows_n - NA_ROWS))
                valid[v, 0, i, 0, a, 0] = r0 <= w0 + i < r0 + NA_ROWS
        variants.append(jnp.stack(strips, axis=1))
    valid = np.repeat(valid.reshape(3, 1, WIN_ROWS, 1, Q_ROWS), GRID_W, axis=4)
    table = jnp.where(jnp.asarray(valid), jnp.stack(variants, axis=0), NEG)
    return table.reshape(3, heads, WIN_ROWS * GRID_W, Q_ROWS * GRID_W)


def _attn_ctx_kernel(qa_ref, ka_ref, va_ref, qb_ref, kb_ref, vb_ref, oa_ref, ob_ref, *, lc):
    lane_hi = lax.broadcasted_iota(jnp.int32, (lc, LANES), 1) >= 64
    ka = ka_ref[...]
    oa_ref[...] = _gqa_tile_t(
        qa_ref[...], lambda g: [(ka, va_ref[:, g * LANES:(g + 1) * LANES], None)], lc, lane_hi).astype(BF16)
    for hp in range(B_HEADS // 2):
        sl = slice(hp * LANES, (hp + 1) * LANES)
        k = kb_ref[:, sl]
        chunks = lambda hh: [(k, vb_ref[:, (2 * hp + hh) * LANES:(2 * hp + hh + 1) * LANES], None)]
        ob_ref[:, sl] = _mha_pair_t(qb_ref[:, sl], chunks, lane_hi).astype(BF16)


def _attn_ctx(qac, kac, vac1, qbc, kbc, vbc1, *, batch, ctx_len):
    t = lambda w: pl.BlockSpec((ctx_len, w), lambda b: (b, 0))
    o = jax.ShapeDtypeStruct((batch * ctx_len, 512), BF16)
    return pl.pallas_call(
        functools.partial(_attn_ctx_kernel, lc=ctx_len),
        grid=(batch,),
        in_specs=[t(512), t(128), t(256), t(512), t(512), t(1024)],
        out_specs=[t(512), t(512)],
        out_shape=[o, o],
        compiler_params=_cparams("arbitrary"),
        name="attn_ctx",
    )(qac, kac, vac1, qbc, kbc, vbc1)


def _dft_kernel(ac_ref, as_ref, bc_ref, bs_ref, uc_ref, us_ref, o_ref, ct_ref, st_ref, *, bm):
    i = pl.program_id(0)

    @pl.when(pl.program_id(1) == 0)
    def _():
        bc = bc_ref[...]
        bs = bs_ref[...]
        for r in range(bm // GRID_W):
            a_c = ac_ref[pl.ds(i * (bm // GRID_W) + r, 1), :]
            a_s = as_ref[pl.ds(i * (bm // GRID_W) + r, 1), :]
            ct_ref[r * GRID_W:(r + 1) * GRID_W, :] = (a_c * bc - a_s * bs).astype(BF16)
            st_ref[r * GRID_W:(r + 1) * GRID_W, :] = (a_s * bc + a_c * bs).astype(BF16)

    o_ref[...] = (_dot(ct_ref[...], uc_ref[...]) - _dot(st_ref[...], us_ref[...])).astype(BF16)


def _dft(tabs, uc, us):
    n, width = uc.shape
    bm = min(n, 1024)
    bn = min(width, 512)
    u_spec = pl.BlockSpec((n, bn), lambda i, j: (0, j))
    return pl.pallas_call(
        functools.partial(_dft_kernel, bm=bm),
        grid=(n // bm, width // bn),
        in_specs=[_resident_spec(t) for t in tabs] + [u_spec, u_spec],
        out_specs=pl.BlockSpec((bm, bn), lambda i, j: (i, j)),
        out_shape=jax.ShapeDtypeStruct((n, width), BF16),
        scratch_shapes=[pltpu.VMEM((bm, n), BF16), pltpu.VMEM((bm, n), BF16)],
        compiler_params=_cparams("arbitrary", "arbitrary"),
        name="dft",
    )(*tabs, uc, us)


def _dft_factor_tables(n):
    r = n // GRID_W
    k = np.arange(n, dtype=np.int64)
    ang_a = (2.0 * np.pi / r) * ((np.arange(r)[:, None] * k[None, :]) % r)
    ang_b = (2.0 * np.pi / n) * ((np.arange(GRID_W)[:, None] * k[None, :]) % n)
    pad = ((0, (-r) % 8), (0, 0))
    scale = n ** -0.5
    return tuple(jnp.asarray(t, dtype=F32) for t in (
        np.pad(np.cos(ang_a), pad), np.pad(np.sin(ang_a), pad), np.cos(ang_b) * scale, np.sin(ang_b) * scale))


def _merge_kernel(x_ref, mod_ref, oa_ref, ob_ref, oc_ref, wg_ref, wb_ref, wo_ref, g_ref, b_ref, o_ref, *, alpha):
    mod = mod_ref[...]
    sh = mod[:, 0:D_MODEL]
    sc = mod[:, D_MODEL:2 * D_MODEL]
    gate = mod[:, 2 * D_MODEL:3 * D_MODEL]
    blocks = _row_blocks(x_ref.shape[0])
    xs = [x_ref[b, :] for b in blocks]
    hs = [(_layernorm(x) * (1.0 + sc) + sh).astype(BF16) for x in xs]
    ys = [None] * len(blocks)
    for i, br_ref in enumerate((oa_ref, ob_ref, oc_ref)):
        for r, b in enumerate(blocks):
            gz = _dot(hs[r], wg_ref[:, i * D_MODEL:(i + 1) * D_MODEL])
            t = _dot(br_ref[b, :], wb_ref[i]) * (1.0 / (1.0 + jnp.exp(-gz)))
            ys[r] = t if ys[r] is None else ys[r] + t
    for r, b in enumerate(blocks):
        yo = _dot(ys[r].astype(BF16), wo_ref[...])
        o_ref[b, :] = _layernorm(alpha * xs[r] + gate * yo) * g_ref[...] + b_ref[...]


def _merge(x2, mod3, oa, ob, oc, w_gate, w_branch, w_out, ln_g, ln_b, *, batch, seq, tm, alpha):
    rows = batch * seq
    nl = seq // tm
    tok = lambda w: pl.BlockSpec((tm, w), lambda i: (i, 0))
    full = _resident_spec
    return pl.pallas_call(
        functools.partial(_merge_kernel, alpha=alpha),
        grid=(rows // tm,),
        in_specs=[
            tok(D_MODEL),
            pl.BlockSpec((None, 1, mod3.shape[2]), lambda i: (i // nl, 0, 0)),
            tok(512), tok(512),
            pl.BlockSpec((tm, 512), lambda i: (i % nl, i // nl)),
            full(w_gate), full(w_branch), full(w_out), full(ln_g), full(ln_b),
        ],
        out_specs=tok(D_MODEL),
        out_shape=jax.ShapeDtypeStruct((rows, D_MODEL), F32),
        compiler_params=_cparams("arbitrary"),
        name="merge",
    )(x2, mod3, oa, ob, oc, w_gate, w_branch, w_out, ln_g, ln_b)


FF_CHUNKS = ((0, 768), (768, 1536), (1536, 2304), (2304, 2816))


def _ffn_kernel(x_ref, mod_ref, wgu_ref, wd_ref, g_ref, b_ref, o_ref, *, alpha):
    mod = mod_ref[...]
    sh = mod[:, 3 * D_MODEL:4 * D_MODEL]
    sc = mod[:, 4 * D_MODEL:5 * D_MODEL]
    gate = mod[:, 5 * D_MODEL:6 * D_MODEL]
    blocks = _row_blocks(x_ref.shape[0])
    xs = [x_ref[b, :] for b in blocks]
    hs = [(_layernorm(x) * (1.0 + sc) + sh).astype(BF16) for x in xs]
    fs = [None] * len(blocks)
    for s, e in FF_CHUNKS:
        for i in range(len(blocks)):
            g = _dot(hs[i], wgu_ref[:, s:e])
            u = _dot(hs[i], wgu_ref[:, D_FF + s:D_FF + e])
            a = (g * (1.0 / (1.0 + jnp.exp(-g))) * u).astype(BF16)
            t = _dot(a, wd_ref[s:e, :])
            fs[i] = t if fs[i] is None else fs[i] + t
    for i, b in enumerate(blocks):
        o_ref[b, :] = _layernorm(alpha * xs[i] + gate * fs[i]) * g_ref[...] + b_ref[...]


def _ffn(x2, mod3, w_gu, w_down, ln_g, ln_b, *, batch, seq, tm, alpha):
    rows = batch * seq
    nl = seq // tm
    tok = pl.BlockSpec((tm, D_MODEL), lambda i: (i, 0))
    full = _resident_spec
    return pl.pallas_call(
        functools.partial(_ffn_kernel, alpha=alpha),
        grid=(rows // tm,),
        in_specs=[tok, pl.BlockSpec((None, 1, mod3.shape[2]), lambda i: (i // nl, 0, 0)),
                  full(w_gu), full(w_down), full(ln_g), full(ln_b)],
        out_specs=tok,
        out_shape=jax.ShapeDtypeStruct((rows, D_MODEL), F32),
        compiler_params=_cparams("arbitrary"),
        name="ffn",
    )(x2, mod3, w_gu, w_down, ln_g, ln_b)


def _rope_tables(seq):
    quarter = HEAD_DIM // 4
    pos = np.arange(seq)
    freqs = ROPE_THETA ** (-np.arange(quarter, dtype=np.float64) / quarter)
    ar = (pos // GRID_W)[:, None] * freqs
    ac = (pos % GRID_W)[:, None] * freqs
    cos64 = np.concatenate([np.cos(ar), np.cos(ar), np.cos(ac), np.cos(ac)], axis=1)
    sin64 = np.concatenate([-np.sin(ar), np.sin(ar), -np.sin(ac), np.sin(ac)], axis=1)
    return jnp.asarray(np.tile(cos64, (1, 2)), dtype=F32), jnp.asarray(np.tile(sin64, (1, 2)), dtype=F32)


def _head_mean_matrix(width):
    idx = np.arange(width) // HEAD_DIM
    return jnp.asarray((idx[:, None] == idx[None, :]).astype(np.float32) / HEAD_DIM).astype(BF16)


def _channel_dft_matrix():
    idx = np.arange(C_GROUP_W)
    ang = (2.0 * np.pi / C_GROUP_W) * ((idx[:, None] * idx[None, :]) % C_GROUP_W)
    scale = C_GROUP_W ** -0.5
    mat = np.concatenate([np.cos(ang), np.sin(ang)], axis=1) * scale
    return jnp.asarray(mat, dtype=F32).astype(BF16)


def kernel(x, c, ctx, c_ctx, w_ada, b_ada, w_in, q_norm, k_norm, rpb, w_branch, w_out,
           ln1_g, ln1_b, w_gu, w_down, ln2_g, ln2_b):
    batch, seq, d = x.shape
    ctx_len = ctx.shape[1]
    depth = w_ada.shape[0]
    assert d == D_MODEL and seq % (B_UNROLL * Q_ROWS * GRID_W) == 0 and seq // GRID_W >= WIN_ROWS
    assert ctx_len % LANES == 0
    alpha = (2.0 * depth) ** 0.25
    tm = 1024
    tmc = min(ctx_len, 1024)
    assert seq % tm == 0 and tm % ROW_BLOCK == 0 and tmc % ROW_BLOCK == 0

    pad = (-(batch + 1)) % 8
    cc = jnp.concatenate([c, c_ctx[None, :], jnp.zeros((pad, d), F32)], axis=0)
    mods = _ada(cc, w_ada, b_ada)

    cos_t, sin_t = _rope_tables(seq)
    cos_c = jnp.ones((ctx_len, LANES), F32)
    sin_c = jnp.zeros((ctx_len, LANES), F32)
    g512 = _head_mean_matrix(512)
    g128 = _head_mean_matrix(128)
    dft_c = _channel_dft_matrix()
    tabs_l = _dft_factor_tables(seq)
    tabs_c = _dft_factor_tables(ctx_len)

    xl = x.reshape(batch * seq, d)
    xc = ctx.reshape(batch * ctx_len, d)
    for l in range(depth):
        with_ctx = l < depth - 1
        w_qkvu = w_in[l, :, :N_QKVU].astype(BF16)
        w_gate = w_in[l, :, N_QKVU:].astype(BF16)
        wb = w_branch[l].astype(BF16)
        wo = w_out[l].astype(BF16)
        wgu = w_gu[l].astype(BF16)
        wd = w_down[l].astype(BF16)
        q_gain = jnp.tile(q_norm[l] * Q_SCALE, A_HEADS)[None, :]
        k_gain = jnp.tile(k_norm[l], A_KV_HEADS)[None, :]
        mod_l = mods[l, :batch][:, None, :]
        mod_c = jnp.broadcast_to(mods[l, batch][None, None, :], (batch, 1, 6 * d))
        ln1 = (ln1_g[l][None, :], ln1_b[l][None, :])
        ln2 = (ln2_g[l][None, :], ln2_b[l][None, :])

        qa, ka, va1, qb, kb, vb1, uc, us = _inproj(
            xl, mod_l, w_qkvu, q_gain, k_gain, cos_t, sin_t, g512, g128, dft_c, batch=batch, seq=seq, tm=tm)
        qac, kac, vac1, qbc, kbc, vbc1, ucc, usc = _inproj(
            xc, mod_c, w_qkvu, q_gain, k_gain, cos_c, sin_c, g512, g128, dft_c, batch=batch, seq=ctx_len, tm=tmc)

        score_bound = (1.01 * HEAD_DIM * jnp.max(jnp.abs(q_gain)) * jnp.max(jnp.abs(k_gain))).reshape(1)
        oa = _attn_a(score_bound, qa, ka, va1, kac, vac1, batch=batch, seq=seq, ctx_len=ctx_len, tq=256, tk=256)
        bias_t = _neighbourhood_bias_t(rpb[l], seq // GRID_W)
        ob = _attn_b(qb, kb, vb1, kbc, vbc1, bias_t, batch=batch, seq=seq, ctx_len=ctx_len, tk=512)
        oc = _dft(tabs_l, uc, us)

        x1 = _merge(xl, mod_l, oa, ob, oc, w_gate, wb, wo, *ln1, batch=batch, seq=seq, tm=tm, alpha=alpha)
        xl = _ffn(x1, mod_l, wgu, wd, *ln2, batch=batch, seq=seq, tm=tm, alpha=alpha)

        if with_ctx:
            oac, obc = _attn_ctx(qac, kac, vac1, qbc, kbc, vbc1, batch=batch, ctx_len=ctx_len)
            occ = _dft(tabs_c, ucc, usc)
            xc1 = _merge(xc, mod_c, oac, obc, occ, w_gate, wb, wo, *ln1,
                         batch=batch, seq=ctx_len, tm=tmc, alpha=alpha)
            xc = _ffn(xc1, mod_c, wgu, wd, *ln2, batch=batch, seq=ctx_len, tm=tmc, alpha=alpha)
    return xl.reshape(batch, seq, d)
```

```python
import functools
import math

import numpy as np
import jax
import jax.numpy as jnp
from jax import lax
from jax.experimental import pallas as pl
from jax.experimental.pallas import tpu as pltpu

F32 = jnp.float32
BF16 = jnp.bfloat16

D_MODEL = 1024
GRID_W = 64
HEAD_DIM = 64
A_HEADS = 8
A_KV_HEADS = 2
A_GROUP = A_HEADS // A_KV_HEADS
B_HEADS = 8
NA_ROWS = 8
NA_COLS = 16
C_GROUPS = 4
C_GROUP_W = 128
BRANCH_W = 512
N_BRANCH = 3
D_FF = 2816
ROPE_THETA = 10000.0
LN_EPS = 1e-6
RMS_EPS = 1e-6
N_QKVU = 2816
Q_ROWS = 8
WIN_ROWS = 16
B_UNROLL = 4
ROW_BLOCK = 256
MAX_SHIFT_BOUND = 40.0
NEG = -1e30
LOG2E = math.log2(math.e)
Q_SCALE = HEAD_DIM ** -0.5 * LOG2E
LANES = 128
VMEM_LIMIT = 56 * 1024 * 1024


def _cparams(*sem):
    return pltpu.CompilerParams(dimension_semantics=sem, vmem_limit_bytes=VMEM_LIMIT)


def _resident_spec(a):
    return pl.BlockSpec(a.shape, lambda *_: (0,) * a.ndim, pipeline_mode=pl.Buffered(1))


def _dot(a, b):
    return jnp.dot(a, b, preferred_element_type=F32)


def _dot_nt(a, b):
    return lax.dot_general(a, b, (((1,), (1,)), ((), ())), preferred_element_type=F32)


def _dot_tn(a, b):
    return lax.dot_general(a, b, (((0,), (0,)), ((), ())), preferred_element_type=F32)


def _row_blocks(rows):
    return [slice(r, r + ROW_BLOCK) for r in range(0, rows, ROW_BLOCK)]


def _layernorm(x):
    mu = jnp.mean(x, axis=-1, keepdims=True)
    xc = x - mu
    var = jnp.mean(xc * xc, axis=-1, keepdims=True)
    return xc * lax.rsqrt(var + LN_EPS)


def _ada_kernel(c_ref, w_ref, b_ref, o_ref):
    c = c_ref[...]
    a = c * (1.0 / (1.0 + jnp.exp(-c)))
    a_hi = a.astype(BF16)
    a_lo = (a - a_hi.astype(F32)).astype(BF16)
    w = w_ref[...]
    w_hi = w.astype(BF16)
    w_lo = (w - w_hi.astype(F32)).astype(BF16)
    o_ref[...] = _dot(a_hi, w_hi) + _dot(a_hi, w_lo) + _dot(a_lo, w_hi) + b_ref[...]


def _ada(cc, w_ada, b_ada):
    depth, d, n = w_ada.shape
    tn = 1536
    return pl.pallas_call(
        _ada_kernel,
        grid=(depth, n // tn),
        in_specs=[
            pl.BlockSpec((cc.shape[0], d), lambda l, j: (0, 0)),
            pl.BlockSpec((None, d, tn), lambda l, j: (l, 0, j)),
            pl.BlockSpec((None, 1, tn), lambda l, j: (l, 0, j)),
        ],
        out_specs=pl.BlockSpec((None, cc.shape[0], tn), lambda l, j: (l, 0, j)),
        out_shape=jax.ShapeDtypeStruct((depth, cc.shape[0], n), F32),
        compiler_params=_cparams("arbitrary", "arbitrary"),
        name="ada",
    )(cc, w_ada, b_ada.reshape(depth, 1, n))


def _swap16(x, lane_lo):
    up = pltpu.roll(x, 16, 1)
    dn = pltpu.roll(x, LANES - 16, 1)
    return jnp.where(lane_lo, dn, up)


def _with_ones(v):
    ones = jnp.ones((v.shape[0], HEAD_DIM), F32)
    parts = []
    for hd in range(v.shape[1] // HEAD_DIM):
        parts += [v[:, hd * HEAD_DIM:(hd + 1) * HEAD_DIM], ones]
    return jnp.concatenate(parts, axis=1).astype(BF16)


def _inproj_kernel(x_ref, mod_ref, w_ref, qg_ref, kg_ref, cos_ref, sin_ref, g512_ref, g128_ref, dft_ref,
                   qa_ref, ka_ref, va_ref, qb_ref, kb_ref, vb_ref, uc_ref, us_ref):
    mod = mod_ref[...]
    sh = mod[:, 0:D_MODEL]
    sc = mod[:, D_MODEL:2 * D_MODEL]
    h = (_layernorm(x_ref[...]) * (1.0 + sc) + sh).astype(BF16)

    cos = cos_ref[...]
    sin = sin_ref[...]
    lane_lo = (lax.broadcasted_iota(jnp.int32, cos.shape, 1) % 32) < 16

    def norm_rope(v, ms, gain):
        vn = v * lax.rsqrt(ms + RMS_EPS) * gain
        outs = []
        for j in range(v.shape[1] // LANES):
            t = vn[:, j * LANES:(j + 1) * LANES]
            outs.append(t * cos + _swap16(t, lane_lo) * sin)
        return outs[0] if len(outs) == 1 else jnp.concatenate(outs, axis=1)

    q = _dot(h, w_ref[:, 0:512])
    kv = _dot(h, w_ref[:, 512:768])
    qb_ref[...] = (_dot(h, w_ref[:, 768:1280]) * Q_SCALE).astype(BF16)
    kb_ref[...] = _dot(h, w_ref[:, 1280:1792]).astype(BF16)
    k = kv[:, 0:128]
    ms_q = _dot((q * q).astype(BF16), g512_ref[...])
    ms_k = _dot((k * k).astype(BF16), g128_ref[...])
    vb_ref[...] = _with_ones(_dot(h, w_ref[:, 1792:2304]))
    u = _dot(h, w_ref[:, 2304:2816]).astype(BF16)
    qa_ref[...] = norm_rope(q, ms_q, qg_ref[...]).astype(BF16)
    ka_ref[...] = norm_rope(k, ms_k, kg_ref[...]).astype(BF16)
    va_ref[...] = _with_ones(kv[:, 128:256])
    dft = dft_ref[...]
    for g in range(C_GROUPS):
        z = _dot(u[:, g * C_GROUP_W:(g + 1) * C_GROUP_W], dft)
        uc_ref[:, g * C_GROUP_W:(g + 1) * C_GROUP_W] = z[:, 0:C_GROUP_W].astype(BF16)
        us_ref[:, g * C_GROUP_W:(g + 1) * C_GROUP_W] = z[:, C_GROUP_W:2 * C_GROUP_W].astype(BF16)


def _inproj(x2, mod3, w_qkvu, q_gain, k_gain, cos_t, sin_t, g512, g128, dft_c, *, batch, seq, tm):
    rows = batch * seq
    nl = seq // tm
    tok = lambda w: pl.BlockSpec((tm, w), lambda i: (i, 0))
    full = _resident_spec
    fo = pl.BlockSpec((tm, 512), lambda i: (i % nl, i // nl))
    o_tok = lambda w: jax.ShapeDtypeStruct((rows, w), BF16)
    return pl.pallas_call(
        _inproj_kernel,
        grid=(rows // tm,),
        in_specs=[
            tok(D_MODEL),
            pl.BlockSpec((None, 1, mod3.shape[2]), lambda i: (i // nl, 0, 0)),
            full(w_qkvu), full(q_gain), full(k_gain),
            pl.BlockSpec((tm, LANES), lambda i: (i % nl, 0)),
            pl.BlockSpec((tm, LANES), lambda i: (i % nl, 0)),
            full(g512), full(g128), full(dft_c),
        ],
        out_specs=[tok(512), tok(128), tok(256), tok(512), tok(512), tok(1024), fo, fo],
        out_shape=[o_tok(512), o_tok(128), o_tok(256), o_tok(512), o_tok(512), o_tok(1024),
                   jax.ShapeDtypeStruct((seq, batch * 512), BF16),
                   jax.ShapeDtypeStruct((seq, batch * 512), BF16)],
        compiler_params=_cparams("arbitrary"),
        name="inproj",
    )(x2, mod3, w_qkvu, q_gain, k_gain, cos_t, sin_t, g512, g128, dft_c)


def _attend_t_multi(qps, chunk_lists, shift=None):
    n = len(qps)
    m = [None] * n
    acc = [None] * n
    units = [(i, c) for c in range(len(chunk_lists[0])) for i in range(n)]

    def scores(u):
        i, c = u
        k, _, bias_t = chunk_lists[i][c]
        s = _dot_nt(k, qps[i])
        return s if bias_t is None else s + bias_t

    s_next = scores(units[0])
    for idx, (i, c) in enumerate(units):
        s = s_next
        if idx + 1 < len(units):
            s_next = scores(units[idx + 1])
        v1 = chunk_lists[i][c][1]
        if shift is not None:
            d = _dot_tn(v1, jnp.exp2(s - shift).astype(BF16))
            acc[i] = d if acc[i] is None else acc[i] + d
            continue
        mc = jnp.max(s, axis=0, keepdims=True)
        if m[i] is None:
            m[i] = mc
            acc[i] = _dot_tn(v1, jnp.exp2(s - mc).astype(BF16))
        else:
            m_new = jnp.maximum(m[i], mc)
            acc[i] = jnp.exp2(m[i] - m_new) * acc[i] + _dot_tn(v1, jnp.exp2(s - m_new).astype(BF16))
            m[i] = m_new
    return [a[0:HEAD_DIM, :] / a[HEAD_DIM:HEAD_DIM + 1, :] for a in acc]


def _place_head(q128, src_half, dst_half, lane_hi):
    x = q128.astype(F32)
    if src_half != dst_half:
        x = pltpu.roll(x, 64, 1)
    keep = lane_hi if dst_half == 1 else jnp.logical_not(lane_hi)
    return jnp.where(keep, x, 0.0).astype(BF16)


def _gqa_tile_t(q, chunks_fn, tq, lane_hi, shift=None):
    qps = []
    for g in range(A_KV_HEADS):
        parts = []
        for j in range(A_GROUP):
            hd = A_GROUP * g + j
            parts.append(_place_head(q[:, LANES * (hd // 2):LANES * (hd // 2 + 1)], hd % 2, g, lane_hi))
        qps.append(jnp.concatenate(parts, axis=0))
    outs = _attend_t_multi(qps, [chunks_fn(g) for g in range(A_KV_HEADS)], shift)
    heads_t = [o_t[:, j * tq:(j + 1) * tq] for o_t in outs for j in range(A_GROUP)]
    return jnp.concatenate(heads_t, axis=0).T


def _mha_pair_t(q128, chunks_fn, lane_hi):
    outs = _attend_t_multi([_place_head(q128, hh, hh, lane_hi) for hh in range(2)],
                           [chunks_fn(hh) for hh in range(2)])
    return jnp.concatenate(outs, axis=0).T


def _attn_a_kernel(bound_ref, q_ref, k_ref, v_ref, kc_ref, vc_ref, o_ref, *, tq, chunk_bounds):
    lane_hi = lax.broadcasted_iota(jnp.int32, (tq, LANES), 1) >= 64

    def chunks(g):
        sl = slice(g * LANES, (g + 1) * LANES)
        return ([(kc_ref[...], vc_ref[:, sl], None)]
                + [(k_ref[s:e, :], v_ref[s:e, sl], None) for s, e in chunk_bounds])

    bound = bound_ref[0]
    o_ref[...] = lax.cond(
        bound <= MAX_SHIFT_BOUND,
        lambda: _gqa_tile_t(q_ref[...], chunks, tq, lane_hi, shift=bound),
        lambda: _gqa_tile_t(q_ref[...], chunks, tq, lane_hi),
    ).astype(BF16)


def _attn_a(score_bound, qa, ka, va1, kac, vac1, *, batch, seq, ctx_len, tq, tk):
    bounds = [(i, min(i + tk, seq)) for i in range(0, seq, tk)]
    nq = seq // tq
    return pl.pallas_call(
        functools.partial(_attn_a_kernel, tq=tq, chunk_bounds=bounds),
        grid=(batch, nq),
        in_specs=[pl.BlockSpec(memory_space=pltpu.SMEM),
                  pl.BlockSpec((tq, 512), lambda b, i: (b * nq + i, 0)),
                  pl.BlockSpec((seq, LANES), lambda b, i: (b, 0)),
                  pl.BlockSpec((seq, 2 * LANES), lambda b, i: (b, 0)),
                  pl.BlockSpec((ctx_len, LANES), lambda b, i: (b, 0)),
                  pl.BlockSpec((ctx_len, 2 * LANES), lambda b, i: (b, 0))],
        out_specs=pl.BlockSpec((tq, 512), lambda b, i: (b * nq + i, 0)),
        out_shape=jax.ShapeDtypeStruct((batch * seq, 512), BF16),
        compiler_params=_cparams("arbitrary", "arbitrary"),
        name="attn_a",
    )(score_bound, qa, ka, va1, kac, vac1)


def _attn_b_kernel(q_ref, k_ref, v_ref, kc_ref, vc_ref, bias_ref, o_ref, *, rows_n, tk):
    nq = Q_ROWS * GRID_W
    nw = WIN_ROWS * GRID_W
    n_groups = rows_n // Q_ROWS
    lane_hi = lax.broadcasted_iota(jnp.int32, (nq, LANES), 1) >= 64
    kc = kc_ref[...]

    def body(t, carry):
        qps, chunk_lists, q0s = [], [], []
        for u in range(B_UNROLL):
            g = t * B_UNROLL + u
            w0 = jnp.clip(Q_ROWS * g - (WIN_ROWS - Q_ROWS) // 2, 0, rows_n - WIN_ROWS)
            var = jnp.where(g == 0, 0, jnp.where(g == n_groups - 1, 2, 1))
            q0 = pl.multiple_of(g * nq, nq)
            k0 = pl.multiple_of(w0 * GRID_W, GRID_W)
            q128 = q_ref[pl.ds(q0, nq), :]
            q0s.append(q0)
            for hh in range(2):
                sl = slice(hh * LANES, (hh + 1) * LANES)
                qps.append(_place_head(q128, hh, hh, lane_hi))
                chunk_lists.append(
                    [(k_ref[pl.ds(k0 + s, tk), :], v_ref[pl.ds(k0 + s, tk), sl], bias_ref[var, hh, s:s + tk, :])
                     for s in range(0, nw, tk)] + [(kc, vc_ref[:, sl], None)])
        outs = _attend_t_multi(qps, chunk_lists)
        for u in range(B_UNROLL):
            o_ref[pl.ds(q0s[u], nq), :] = jnp.concatenate(outs[2 * u:2 * u + 2], axis=0).T.astype(BF16)
        return carry

    lax.fori_loop(0, n_groups // B_UNROLL, body, 0)


def _attn_b(qb, kb, vb1, kbc, vbc1, bias_t, *, batch, seq, ctx_len, tk):
    rows_n = seq // GRID_W
    nq = Q_ROWS * GRID_W
    nw = WIN_ROWS * GRID_W
    lat = lambda w: pl.BlockSpec((seq, w), lambda hp, b: (b, hp))
    ctx = lambda w: pl.BlockSpec((ctx_len, w), lambda hp, b: (b, hp))
    return pl.pallas_call(
        functools.partial(_attn_b_kernel, rows_n=rows_n, tk=tk),
        grid=(B_HEADS // 2, batch),
        in_specs=[lat(LANES), lat(LANES), lat(2 * LANES), ctx(LANES), ctx(2 * LANES),
                  pl.BlockSpec((3, 2, nw, nq), lambda hp, b: (0, hp, 0, 0))],
        out_specs=lat(LANES),
        out_shape=jax.ShapeDtypeStruct((batch * seq, 512), BF16),
        compiler_params=_cparams("arbitrary", "arbitrary"),
        name="attn_b",
    )(qb, kb, vb1, kbc, vbc1, bias_t)


def _neighbourhood_bias_t(rpb, rows_n):
    kc = NA_COLS
    cols = np.arange(GRID_W)
    c0 = np.clip(cols - kc // 2, 0, GRID_W - kc)
    col_ok = (cols[None, :] >= c0[:, None]) & (cols[None, :] < c0[:, None] + kc)
    heads, n_dr, n_dc = rpb.shape
    dc = cols[:, None] - cols[None, :] + NA_COLS - 1
    onehot = (np.arange(n_dc)[:, None, None] == dc[None]).astype(np.float32)
    pad = Q_ROWS
    e_max = n_dr - 1 + 2 * pad
    t = jnp.einsum("hdj,jxy->hxdy", rpb[:, ::-1] * LOG2E, jnp.asarray(onehot), precision=lax.Precision.HIGHEST)
    t = jnp.where(col_ok.T[None, :, None, :], t, NEG)
    rev = jnp.pad(t, ((0, 0), (0, 0), (pad, pad), (0, 0))).reshape(heads, GRID_W, (e_max + 1) * GRID_W)
    nq = Q_ROWS * GRID_W
    width = -(-((e_max + 1) * GRID_W + LANES) // LANES) * LANES
    rev = jnp.pad(rev, ((0, 0), (0, 0), (0, width - rev.shape[2])))
    n_groups = rows_n // Q_ROWS
    plans = []
    for g in (0, min(1, n_groups - 1), n_groups - 1):
        w0 = int(np.clip(Q_ROWS * g - (WIN_ROWS - Q_ROWS) // 2, 0, rows_n - WIN_ROWS))
        plan = []
        for i in range(WIN_ROWS):
            d0 = w0 + i - Q_ROWS * g + NA_ROWS - 1
            ok = [a for a in range(Q_ROWS)
                  if int(np.clip(Q_ROWS * g + a - NA_ROWS // 2, 0, rows_n - NA_ROWS)) <= w0 + i
                  < int(np.clip(Q_ROWS * g + a - NA_ROWS // 2, 0, rows_n - NA_ROWS)) + NA_ROWS]
            assert ok == list(range(ok[0], ok[-1] + 1)) if ok else True
            plan.append(((e_max - pad - d0) * GRID_W, (ok[0], ok[-1] + 1) if ok else (0, 0)))
        plans.append(plan)
    return pl.pallas_call(
        functools.partial(_bias_kernel, plans=plans),
        grid=(3, heads),
        in_specs=[pl.BlockSpec((None, GRID_W, width), lambda v, h: (h, 0, 0))],
        out_specs=pl.BlockSpec((None, None, WIN_ROWS * GRID_W, nq), lambda v, h: (v, h, 0, 0)),
        out_shape=jax.ShapeDtypeStruct((3, heads, WIN_ROWS * GRID_W, nq), F32),
        compiler_params=_cparams("arbitrary", "arbitrary"),
        name="nbr_bias",
    )(rev)


def _bias_kernel(rev_ref, o_ref, *, plans):
    nq = o_ref.shape[1]
    q_row = lax.broadcasted_iota(jnp.int32, (GRID_W, nq), 1) // GRID_W
    for v, plan in enumerate(plans):
        @pl.when(pl.program_id(0) == v)
        def _(plan=plan):
            for i, (off, (lo, hi)) in enumerate(plan):
                rows = slice(i * GRID_W, (i + 1) * GRID_W)
                if lo >= hi:
                    o_ref[rows, :] = jnp.full((GRID_W, nq), NEG, F32)
                    continue
                al = off // LANES * LANES
                strip = rev_ref[:, al:al + nq + LANES][:, off - al:off - al + nq]
                o_ref[rows, :] = jnp.where((q_row >= lo) & (q_row < hi), strip, NEG)


def _attn_ctx_kernel(qa_ref, ka_ref, va_ref, qb_ref, kb_ref, vb_ref, oa_ref, ob_ref, *, lc):
    lane_hi = lax.broadcasted_iota(jnp.int32, (lc, LANES), 1) >= 64
    ka = ka_ref[...]
    oa_ref[...] = _gqa_tile_t(
        qa_ref[...], lambda g: [(ka, va_ref[:, g * LANES:(g + 1) * LANES], None)], lc, lane_hi).astype(BF16)
    for hp in range(B_HEADS // 2):
        sl = slice(hp * LANES, (hp + 1) * LANES)
        k = kb_ref[:, sl]
        chunks = lambda hh: [(k, vb_ref[:, (2 * hp + hh) * LANES:(2 * hp + hh + 1) * LANES], None)]
        ob_ref[:, sl] = _mha_pair_t(qb_ref[:, sl], chunks, lane_hi).astype(BF16)


def _attn_ctx(qac, kac, vac1, qbc, kbc, vbc1, *, batch, ctx_len):
    t = lambda w: pl.BlockSpec((ctx_len, w), lambda b: (b, 0))
    o = jax.ShapeDtypeStruct((batch * ctx_len, 512), BF16)
    return pl.pallas_call(
        functools.partial(_attn_ctx_kernel, lc=ctx_len),
        grid=(batch,),
        in_specs=[t(512), t(128), t(256), t(512), t(512), t(1024)],
        out_specs=[t(512), t(512)],
        out_shape=[o, o],
        compiler_params=_cparams("arbitrary"),
        name="attn_ctx",
    )(qac, kac, vac1, qbc, kbc, vbc1)


def _dft_kernel(ac_ref, as_ref, bc_ref, bs_ref, uc_ref, us_ref, o_ref, ct_ref, st_ref, *, bm):
    i = pl.program_id(0)

    @pl.when(pl.program_id(1) == 0)
    def _():
        bc = bc_ref[...]
        bs = bs_ref[...]
        for r in range(bm // GRID_W):
            a_c = ac_ref[pl.ds(i * (bm // GRID_W) + r, 1), :]
            a_s = as_ref[pl.ds(i * (bm // GRID_W) + r, 1), :]
            ct_ref[r * GRID_W:(r + 1) * GRID_W, :] = (a_c * bc - a_s * bs).astype(BF16)
            st_ref[r * GRID_W:(r + 1) * GRID_W, :] = (a_s * bc + a_c * bs).astype(BF16)

    o_ref[...] = (_dot(ct_ref[...], uc_ref[...]) - _dot(st_ref[...], us_ref[...])).astype(BF16)


def _dft(tabs, uc, us):
    n, width = uc.shape
    bm = min(n, 1024)
    bn = min(width, 512)
    u_spec = pl.BlockSpec((n, bn), lambda i, j: (0, j))
    return pl.pallas_call(
        functools.partial(_dft_kernel, bm=bm),
        grid=(n // bm, width // bn),
        in_specs=[_resident_spec(t) for t in tabs] + [u_spec, u_spec],
        out_specs=pl.BlockSpec((bm, bn), lambda i, j: (i, j)),
        out_shape=jax.ShapeDtypeStruct((n, width), BF16),
        scratch_shapes=[pltpu.VMEM((bm, n), BF16), pltpu.VMEM((bm, n), BF16)],
        compiler_params=_cparams("arbitrary", "arbitrary"),
        name="dft",
    )(*tabs, uc, us)


def _dft_factor_tables(n):
    r = n // GRID_W
    k = np.arange(n, dtype=np.int64)
    ang_a = (2.0 * np.pi / r) * ((np.arange(r)[:, None] * k[None, :]) % r)
    ang_b = (2.0 * np.pi / n) * ((np.arange(GRID_W)[:, None] * k[None, :]) % n)
    pad = ((0, (-r) % 8), (0, 0))
    scale = n ** -0.5
    return tuple(jnp.asarray(t, dtype=F32) for t in (
        np.pad(np.cos(ang_a), pad), np.pad(np.sin(ang_a), pad), np.cos(ang_b) * scale, np.sin(ang_b) * scale))


def _merge_kernel(x_ref, mod_ref, oa_ref, ob_ref, oc_ref, wg_ref, wb_ref, wo_ref, g_ref, b_ref, o_ref, *, alpha):
    mod = mod_ref[...]
    sh = mod[:, 0:D_MODEL]
    sc = mod[:, D_MODEL:2 * D_MODEL]
    gate = mod[:, 2 * D_MODEL:3 * D_MODEL]
    blocks = _row_blocks(x_ref.shape[0])
    xs = [x_ref[b, :] for b in blocks]
    hs = [(_layernorm(x) * (1.0 + sc) + sh).astype(BF16) for x in xs]
    ys = [None] * len(blocks)
    for i, br_ref in enumerate((oa_ref, ob_ref, oc_ref)):
        for r, b in enumerate(blocks):
            gz = _dot(hs[r], wg_ref[:, i * D_MODEL:(i + 1) * D_MODEL])
            t = _dot(br_ref[b, :], wb_ref[i]) * (1.0 / (1.0 + jnp.exp(-gz)))
            ys[r] = t if ys[r] is None else ys[r] + t
    for r, b in enumerate(blocks):
        yo = _dot(ys[r].astype(BF16), wo_ref[...])
        o_ref[b, :] = _layernorm(alpha * xs[r] + gate * yo) * g_ref[...] + b_ref[...]


def _merge(x2, mod3, oa, ob, oc, w_gate, w_branch, w_out, ln_g, ln_b, *, batch, seq, tm, alpha):
    rows = batch * seq
    nl = seq // tm
    tok = lambda w: pl.BlockSpec((tm, w), lambda i: (i, 0))
    full = _resident_spec
    return pl.pallas_call(
        functools.partial(_merge_kernel, alpha=alpha),
        grid=(rows // tm,),
        in_specs=[
            tok(D_MODEL),
            pl.BlockSpec((None, 1, mod3.shape[2]), lambda i: (i // nl, 0, 0)),
            tok(512), tok(512),
            pl.BlockSpec((tm, 512), lambda i: (i % nl, i // nl)),
            full(w_gate), full(w_branch), full(w_out), full(ln_g), full(ln_b),
        ],
        out_specs=tok(D_MODEL),
        out_shape=jax.ShapeDtypeStruct((rows, D_MODEL), F32),
        compiler_params=_cparams("arbitrary"),
        name="merge",
    )(x2, mod3, oa, ob, oc, w_gate, w_branch, w_out, ln_g, ln_b)


FF_CHUNKS = ((0, 768), (768, 1536), (1536, 2304), (2304, 2816))


def _ffn_kernel(x_ref, mod_ref, wgu_ref, wd_ref, g_ref, b_ref, o_ref, *, alpha):
    mod = mod_ref[...]
    sh = mod[:, 3 * D_MODEL:4 * D_MODEL]
    sc = mod[:, 4 * D_MODEL:5 * D_MODEL]
    gate = mod[:, 5 * D_MODEL:6 * D_MODEL]
    blocks = _row_blocks(x_ref.shape[0])
    xs = [x_ref[b, :] for b in blocks]
    hs = [(_layernorm(x) * (1.0 + sc) + sh).astype(BF16) for x in xs]
    fs = [None] * len(blocks)
    for s, e in FF_CHUNKS:
        for i in range(len(blocks)):
            g = _dot(hs[i], wgu_ref[:, s:e])
            u = _dot(hs[i], wgu_ref[:, D_FF + s:D_FF + e])
            a = (g * (1.0 / (1.0 + jnp.exp(-g))) * u).astype(BF16)
            t = _dot(a, wd_ref[s:e, :])
            fs[i] = t if fs[i] is None else fs[i] + t
    for i, b in enumerate(blocks):
        o_ref[b, :] = _layernorm(alpha * xs[i] + gate * fs[i]) * g_ref[...] + b_ref[...]


def _ffn(x2, mod3, w_gu, w_down, ln_g, ln_b, *, batch, seq, tm, alpha):
    rows = batch * seq
    nl = seq // tm
    tok = pl.BlockSpec((tm, D_MODEL), lambda i: (i, 0))
    full = _resident_spec
    return pl.pallas_call(
        functools.partial(_ffn_kernel, alpha=alpha),
        grid=(rows // tm,),
        in_specs=[tok, pl.BlockSpec((None, 1, mod3.shape[2]), lambda i: (i // nl, 0, 0)),
                  full(w_gu), full(w_down), full(ln_g), full(ln_b)],
        out_specs=tok,
        out_shape=jax.ShapeDtypeStruct((rows, D_MODEL), F32),
        compiler_params=_cparams("arbitrary"),
        name="ffn",
    )(x2, mod3, w_gu, w_down, ln_g, ln_b)


def _rope_tables(seq):
    quarter = HEAD_DIM // 4
    pos = np.arange(seq)
    freqs = ROPE_THETA ** (-np.arange(quarter, dtype=np.float64) / quarter)
    ar = (pos // GRID_W)[:, None] * freqs
    ac = (pos % GRID_W)[:, None] * freqs
    cos64 = np.concatenate([np.cos(ar), np.cos(ar), np.cos(ac), np.cos(ac)], axis=1)
    sin64 = np.concatenate([-np.sin(ar), np.sin(ar), -np.sin(ac), np.sin(ac)], axis=1)
    return jnp.asarray(np.tile(cos64, (1, 2)), dtype=F32), jnp.asarray(np.tile(sin64, (1, 2)), dtype=F32)


def _head_mean_matrix(width):
    idx = np.arange(width) // HEAD_DIM
    return jnp.asarray((idx[:, None] == idx[None, :]).astype(np.float32) / HEAD_DIM).astype(BF16)


def _channel_dft_matrix():
    idx = np.arange(C_GROUP_W)
    ang = (2.0 * np.pi / C_GROUP_W) * ((idx[:, None] * idx[None, :]) % C_GROUP_W)
    scale = C_GROUP_W ** -0.5
    mat = np.concatenate([np.cos(ang), np.sin(ang)], axis=1) * scale
    return jnp.asarray(mat, dtype=F32).astype(BF16)


def kernel(x, c, ctx, c_ctx, w_ada, b_ada, w_in, q_norm, k_norm, rpb, w_branch, w_out,
           ln1_g, ln1_b, w_gu, w_down, ln2_g, ln2_b):
    batch, seq, d = x.shape
    ctx_len = ctx.shape[1]
    depth = w_ada.shape[0]
    assert d == D_MODEL and seq % (B_UNROLL * Q_ROWS * GRID_W) == 0 and seq // GRID_W >= WIN_ROWS
    assert ctx_len % LANES == 0
    alpha = (2.0 * depth) ** 0.25
    tm = 1024
    tmc = min(ctx_len, 1024)
    assert seq % tm == 0 and tm % ROW_BLOCK == 0 and tmc % ROW_BLOCK == 0

    pad = (-(batch + 1)) % 8
    cc = jnp.concatenate([c, c_ctx[None, :], jnp.zeros((pad, d), F32)], axis=0)
    mods = _ada(cc, w_ada, b_ada)

    cos_t, sin_t = _rope_tables(seq)
    cos_c = jnp.ones((ctx_len, LANES), F32)
    sin_c = jnp.zeros((ctx_len, LANES), F32)
    g512 = _head_mean_matrix(512)
    g128 = _head_mean_matrix(128)
    dft_c = _channel_dft_matrix()
    tabs_l = _dft_factor_tables(seq)
    tabs_c = _dft_factor_tables(ctx_len)

    xl = x.reshape(batch * seq, d)
    xc = ctx.reshape(batch * ctx_len, d)
    for l in range(depth):
        with_ctx = l < depth - 1
        w_qkvu = w_in[l, :, :N_QKVU].astype(BF16)
        w_gate = w_in[l, :, N_QKVU:].astype(BF16)
        wb = w_branch[l].astype(BF16)
        wo = w_out[l].astype(BF16)
        wgu = w_gu[l].astype(BF16)
        wd = w_down[l].astype(BF16)
        q_gain = jnp.tile(q_norm[l] * Q_SCALE, A_HEADS)[None, :]
        k_gain = jnp.tile(k_norm[l], A_KV_HEADS)[None, :]
        mod_l = mods[l, :batch][:, None, :]
        mod_c = jnp.broadcast_to(mods[l, batch][None, None, :], (batch, 1, 6 * d))
        ln1 = (ln1_g[l][None, :], ln1_b[l][None, :])
        ln2 = (ln2_g[l][None, :], ln2_b[l][None, :])

        qa, ka, va1, qb, kb, vb1, uc, us = _inproj(
            xl, mod_l, w_qkvu, q_gain, k_gain, cos_t, sin_t, g512, g128, dft_c, batch=batch, seq=seq, tm=tm)
        qac, kac, vac1, qbc, kbc, vbc1, ucc, usc = _inproj(
            xc, mod_c, w_qkvu, q_gain, k_gain, cos_c, sin_c, g512, g128, dft_c, batch=batch, seq=ctx_len, tm=tmc)

        score_bound = (1.01 * HEAD_DIM * jnp.max(jnp.abs(q_gain)) * jnp.max(jnp.abs(k_gain))).reshape(1)
        oa = _attn_a(score_bound, qa, ka, va1, kac, vac1, batch=batch, seq=seq, ctx_len=ctx_len, tq=256, tk=256)
        bias_t = _neighbourhood_bias_t(rpb[l], seq // GRID_W)
        ob = _attn_b(qb, kb, vb1, kbc, vbc1, bias_t, batch=batch, seq=seq, ctx_len=ctx_len, tk=512)
        oc = _dft(tabs_l, uc, us)

        x1 = _merge(xl, mod_l, oa, ob, oc, w_gate, wb, wo, *ln1, batch=batch, seq=seq, tm=tm, alpha=alpha)
        xl = _ffn(x1, mod_l, wgu, wd, *ln2, batch=batch, seq=seq, tm=tm, alpha=alpha)

        if with_ctx:
            oac, obc = _attn_ctx(qac, kac, vac1, qbc, kbc, vbc1, batch=batch, ctx_len=ctx_len)
            occ = _dft(tabs_c, ucc, usc)
            xc1 = _merge(xc, mod_c, oac, obc, occ, w_gate, wb, wo, *ln1,
                         batch=batch, seq=ctx_len, tm=tmc, alpha=alpha)
            xc = _ffn(xc1, mod_c, wgu, wd, *ln2, batch=batch, seq=ctx_len, tm=tmc, alpha=alpha)
    return xl.reshape(batch, seq, d)
```

```python
import functools
import math

import numpy as np
import jax
import jax.numpy as jnp
from jax import lax
from jax.experimental import pallas as pl
from jax.experimental.pallas import tpu as pltpu

F32 = jnp.float32
BF16 = jnp.bfloat16

D_MODEL = 1024
GRID_W = 64
HEAD_DIM = 64
A_HEADS = 8
A_KV_HEADS = 2
A_GROUP = A_HEADS // A_KV_HEADS
B_HEADS = 8
NA_ROWS = 8
NA_COLS = 16
C_GROUPS = 4
C_GROUP_W = 128
BRANCH_W = 512
N_BRANCH = 3
D_FF = 2816
ROPE_THETA = 10000.0
LN_EPS = 1e-6
RMS_EPS = 1e-6
N_QKVU = 2816
Q_ROWS = 8
WIN_ROWS = 16
MAX_GROUPS_PER_TRIP = 6
ROW_BLOCK = 256
MAX_SHIFT_BOUND = 40.0
NEG = -1e30
LOG2E = math.log2(math.e)
Q_SCALE = HEAD_DIM ** -0.5 * LOG2E
LANES = 128
VMEM_LIMIT = 56 * 1024 * 1024


def _cparams(*sem):
    return pltpu.CompilerParams(dimension_semantics=sem, vmem_limit_bytes=VMEM_LIMIT)


def _resident_spec(a):
    return pl.BlockSpec(a.shape, lambda *_: (0,) * a.ndim, pipeline_mode=pl.Buffered(1))


def _dot(a, b):
    return jnp.dot(a, b, preferred_element_type=F32)


def _dot_nt(a, b):
    return lax.dot_general(a, b, (((1,), (1,)), ((), ())), preferred_element_type=F32)


def _dot_tn(a, b):
    return lax.dot_general(a, b, (((0,), (0,)), ((), ())), preferred_element_type=F32)


def _row_blocks(rows):
    return [slice(r, r + ROW_BLOCK) for r in range(0, rows, ROW_BLOCK)]


def _layernorm(x):
    mu = jnp.mean(x, axis=-1, keepdims=True)
    xc = x - mu
    var = jnp.mean(xc * xc, axis=-1, keepdims=True)
    return xc * lax.rsqrt(var + LN_EPS)


def _ada_kernel(c_ref, w_ref, b_ref, o_ref):
    c = c_ref[...]
    a = c * (1.0 / (1.0 + jnp.exp(-c)))
    a_hi = a.astype(BF16)
    a_lo = (a - a_hi.astype(F32)).astype(BF16)
    w = w_ref[...]
    w_hi = w.astype(BF16)
    w_lo = (w - w_hi.astype(F32)).astype(BF16)
    o_ref[...] = _dot(a_hi, w_hi) + _dot(a_hi, w_lo) + _dot(a_lo, w_hi) + b_ref[...]


def _ada(cc, w_ada, b_ada):
    depth, d, n = w_ada.shape
    tn = 1536
    return pl.pallas_call(
        _ada_kernel,
        grid=(depth, n // tn),
        in_specs=[
            pl.BlockSpec((cc.shape[0], d), lambda l, j: (0, 0)),
            pl.BlockSpec((None, d, tn), lambda l, j: (l, 0, j)),
            pl.BlockSpec((None, 1, tn), lambda l, j: (l, 0, j)),
        ],
        out_specs=pl.BlockSpec((None, cc.shape[0], tn), lambda l, j: (l, 0, j)),
        out_shape=jax.ShapeDtypeStruct((depth, cc.shape[0], n), F32),
        compiler_params=_cparams("arbitrary", "arbitrary"),
        name="ada",
    )(cc, w_ada, b_ada.reshape(depth, 1, n))


def _swap16(x, lane_lo):
    up = pltpu.roll(x, 16, 1)
    dn = pltpu.roll(x, LANES - 16, 1)
    return jnp.where(lane_lo, dn, up)


def _with_ones(v):
    ones = jnp.ones((v.shape[0], HEAD_DIM), F32)
    parts = []
    for hd in range(v.shape[1] // HEAD_DIM):
        parts += [v[:, hd * HEAD_DIM:(hd + 1) * HEAD_DIM], ones]
    return jnp.concatenate(parts, axis=1).astype(BF16)


def _inproj_kernel(x_ref, mod_ref, w_ref, qg_ref, kg_ref, cos_ref, sin_ref, g512_ref, g128_ref, dft_ref,
                   qa_ref, ka_ref, va_ref, qb_ref, kb_ref, vb_ref, uc_ref, us_ref):
    mod = mod_ref[...]
    sh = mod[:, 0:D_MODEL]
    sc = mod[:, D_MODEL:2 * D_MODEL]
    h = (_layernorm(x_ref[...]) * (1.0 + sc) + sh).astype(BF16)

    cos = cos_ref[...]
    sin = sin_ref[...]
    lane_lo = (lax.broadcasted_iota(jnp.int32, cos.shape, 1) % 32) < 16

    def norm_rope(v, ms, gain):
        vn = v * lax.rsqrt(ms + RMS_EPS) * gain
        outs = []
        for j in range(v.shape[1] // LANES):
            t = vn[:, j * LANES:(j + 1) * LANES]
            outs.append(t * cos + _swap16(t, lane_lo) * sin)
        return outs[0] if len(outs) == 1 else jnp.concatenate(outs, axis=1)

    q = _dot(h, w_ref[:, 0:512])
    kv = _dot(h, w_ref[:, 512:768])
    qb_ref[...] = (_dot(h, w_ref[:, 768:1280]) * Q_SCALE).astype(BF16)
    kb_ref[...] = _dot(h, w_ref[:, 1280:1792]).astype(BF16)
    k = kv[:, 0:128]
    ms_q = _dot((q * q).astype(BF16), g512_ref[...])
    ms_k = _dot((k * k).astype(BF16), g128_ref[...])
    vb_ref[...] = _with_ones(_dot(h, w_ref[:, 1792:2304]))
    u = _dot(h, w_ref[:, 2304:2816]).astype(BF16)
    qa_ref[...] = norm_rope(q, ms_q, qg_ref[...]).astype(BF16)
    ka_ref[...] = norm_rope(k, ms_k, kg_ref[...]).astype(BF16)
    va_ref[...] = _with_ones(kv[:, 128:256])
    dft = dft_ref[...]
    for g in range(C_GROUPS):
        z = _dot(u[:, g * C_GROUP_W:(g + 1) * C_GROUP_W], dft)
        uc_ref[:, g * C_GROUP_W:(g + 1) * C_GROUP_W] = z[:, 0:C_GROUP_W].astype(BF16)
        us_ref[:, g * C_GROUP_W:(g + 1) * C_GROUP_W] = z[:, C_GROUP_W:2 * C_GROUP_W].astype(BF16)


def _inproj(x2, mod3, w_qkvu, q_gain, k_gain, cos_t, sin_t, g512, g128, dft_c, *, batch, seq, tm):
    rows = batch * seq
    nl = seq // tm
    tok = lambda w: pl.BlockSpec((tm, w), lambda i: (i, 0))
    full = _resident_spec
    fo = pl.BlockSpec((tm, 512), lambda i: (i % nl, i // nl))
    o_tok = lambda w: jax.ShapeDtypeStruct((rows, w), BF16)
    return pl.pallas_call(
        _inproj_kernel,
        grid=(rows // tm,),
        in_specs=[
            tok(D_MODEL),
            pl.BlockSpec((None, 1, mod3.shape[2]), lambda i: (i // nl, 0, 0)),
            full(w_qkvu), full(q_gain), full(k_gain),
            pl.BlockSpec((tm, LANES), lambda i: (i % nl, 0)),
            pl.BlockSpec((tm, LANES), lambda i: (i % nl, 0)),
            full(g512), full(g128), full(dft_c),
        ],
        out_specs=[tok(512), tok(128), tok(256), tok(512), tok(512), tok(1024), fo, fo],
        out_shape=[o_tok(512), o_tok(128), o_tok(256), o_tok(512), o_tok(512), o_tok(1024),
                   jax.ShapeDtypeStruct((seq, batch * 512), BF16),
                   jax.ShapeDtypeStruct((seq, batch * 512), BF16)],
        compiler_params=_cparams("arbitrary"),
        name="inproj",
    )(x2, mod3, w_qkvu, q_gain, k_gain, cos_t, sin_t, g512, g128, dft_c)


def _attend_t_multi(qps, chunk_lists, shift=None):
    n = len(qps)
    m = [None] * n
    acc = [None] * n
    units = [(i, c) for c in range(len(chunk_lists[0])) for i in range(n)]

    def scores(u):
        i, c = u
        chunk = chunk_lists[i][c]
        s = _dot_nt(chunk[0], qps[i])
        return s if chunk[2] is None or len(chunk) == 4 else s + chunk[2]

    def banded(s, bias_fn, band):
        cols = [slice(b * LANES, (b + 1) * LANES) for b in range(len(band))]
        subs = [s[lo:hi, cs] + bias_fn(lo, hi, cs) for (lo, hi), cs in zip(band, cols)]
        mc = jnp.concatenate([jnp.max(x, axis=0, keepdims=True) for x in subs], axis=1)

        def probs(m_ref):
            out = []
            for x, (lo, hi), cs in zip(subs, band, cols):
                parts = [jnp.exp2(x - m_ref[:, cs]).astype(BF16)]
                if lo:
                    parts.insert(0, jnp.zeros((lo, LANES), BF16))
                if s.shape[0] - hi:
                    parts.append(jnp.zeros((s.shape[0] - hi, LANES), BF16))
                out.append(parts[0] if len(parts) == 1 else jnp.concatenate(parts, axis=0))
            return jnp.concatenate(out, axis=1)

        return mc, probs

    s_next = scores(units[0])
    for idx, (i, c) in enumerate(units):
        s = s_next
        if idx + 1 < len(units):
            s_next = scores(units[idx + 1])
        chunk = chunk_lists[i][c]
        v1 = chunk[1]
        if shift is not None:
            d = _dot_tn(v1, jnp.exp2(s - shift).astype(BF16))
            acc[i] = d if acc[i] is None else acc[i] + d
            continue
        if len(chunk) == 4:
            mc, probs = banded(s, chunk[2], chunk[3])
        else:
            mc, probs = jnp.max(s, axis=0, keepdims=True), lambda m_ref, s=s: jnp.exp2(s - m_ref).astype(BF16)
        if m[i] is None:
            m[i] = mc
            acc[i] = _dot_tn(v1, probs(mc))
        else:
            m_new = jnp.maximum(m[i], mc)
            acc[i] = jnp.exp2(m[i] - m_new) * acc[i] + _dot_tn(v1, probs(m_new))
            m[i] = m_new
    return [a[0:HEAD_DIM, :] / a[HEAD_DIM:HEAD_DIM + 1, :] for a in acc]


def _place_head(q128, src_half, dst_half, lane_hi):
    x = q128.astype(F32)
    if src_half != dst_half:
        x = pltpu.roll(x, 64, 1)
    keep = lane_hi if dst_half == 1 else jnp.logical_not(lane_hi)
    return jnp.where(keep, x, 0.0).astype(BF16)


def _gqa_tile_t(q, chunks_fn, tq, lane_hi, shift=None):
    qps = []
    for g in range(A_KV_HEADS):
        parts = []
        for j in range(A_GROUP):
            hd = A_GROUP * g + j
            parts.append(_place_head(q[:, LANES * (hd // 2):LANES * (hd // 2 + 1)], hd % 2, g, lane_hi))
        qps.append(jnp.concatenate(parts, axis=0))
    outs = _attend_t_multi(qps, [chunks_fn(g) for g in range(A_KV_HEADS)], shift)
    heads_t = [o_t[:, j * tq:(j + 1) * tq] for o_t in outs for j in range(A_GROUP)]
    return jnp.concatenate(heads_t, axis=0).T


def _mha_pair_t(q128, chunks_fn, lane_hi):
    outs = _attend_t_multi([_place_head(q128, hh, hh, lane_hi) for hh in range(2)],
                           [chunks_fn(hh) for hh in range(2)])
    return jnp.concatenate(outs, axis=0).T


def _attn_a_kernel(bound_ref, q_ref, k_ref, v_ref, kc_ref, vc_ref, o_ref, *, tq, chunk_bounds):
    lane_hi = lax.broadcasted_iota(jnp.int32, (tq, LANES), 1) >= 64

    def chunks(g):
        sl = slice(g * LANES, (g + 1) * LANES)
        return ([(kc_ref[...], vc_ref[:, sl], None)]
                + [(k_ref[s:e, :], v_ref[s:e, sl], None) for s, e in chunk_bounds])

    bound = bound_ref[0]
    o_ref[...] = lax.cond(
        bound <= MAX_SHIFT_BOUND,
        lambda: _gqa_tile_t(q_ref[...], chunks, tq, lane_hi, shift=bound),
        lambda: _gqa_tile_t(q_ref[...], chunks, tq, lane_hi),
    ).astype(BF16)


def _attn_a(score_bound, qa, ka, va1, kac, vac1, *, batch, seq, ctx_len, tq, tk):
    bounds = [(i, min(i + tk, seq)) for i in range(0, seq, tk)]
    nq = seq // tq
    return pl.pallas_call(
        functools.partial(_attn_a_kernel, tq=tq, chunk_bounds=bounds),
        grid=(batch, nq),
        in_specs=[pl.BlockSpec(memory_space=pltpu.SMEM),
                  pl.BlockSpec((tq, 512), lambda b, i: (b * nq + i, 0)),
                  pl.BlockSpec((seq, LANES), lambda b, i: (b, 0)),
                  pl.BlockSpec((seq, 2 * LANES), lambda b, i: (b, 0)),
                  pl.BlockSpec((ctx_len, LANES), lambda b, i: (b, 0)),
                  pl.BlockSpec((ctx_len, 2 * LANES), lambda b, i: (b, 0))],
        out_specs=pl.BlockSpec((tq, 512), lambda b, i: (b * nq + i, 0)),
        out_shape=jax.ShapeDtypeStruct((batch * seq, 512), BF16),
        compiler_params=_cparams("arbitrary", "arbitrary"),
        name="attn_a",
    )(score_bound, qa, ka, va1, kac, vac1)


def _attn_b_kernel(q_ref, k_ref, v_ref, kc_ref, vc_ref, bias_ref, o_ref, *, rows_n, tk):
    nq = Q_ROWS * GRID_W
    nw = WIN_ROWS * GRID_W
    n_groups = rows_n // Q_ROWS
    lane_hi = lax.broadcasted_iota(jnp.int32, (nq, LANES), 1) >= 64
    kc = kc_ref[...]

    def run(groups):
        qps, chunk_lists, q0s = [], [], []
        for g, var, use_band in groups:
            if isinstance(g, int):
                q0 = g * nq
                k0 = int(np.clip(Q_ROWS * g - (WIN_ROWS - Q_ROWS) // 2, 0, rows_n - WIN_ROWS)) * GRID_W
            else:
                q0 = pl.multiple_of(g * nq, nq)
                w0 = jnp.clip(Q_ROWS * g - (WIN_ROWS - Q_ROWS) // 2, 0, rows_n - WIN_ROWS)
                k0 = pl.multiple_of(w0 * GRID_W, GRID_W)
            q128 = q_ref[pl.ds(q0, nq), :]
            q0s.append(q0)
            for hh in range(2):
                sl = slice(hh * LANES, (hh + 1) * LANES)
                qps.append(_place_head(q128, hh, hh, lane_hi))
                chunks = []
                for s in range(0, nw, tk):
                    k = k_ref[pl.ds(k0 + s, tk), :]
                    v = v_ref[pl.ds(k0 + s, tk), sl]
                    if use_band:
                        band = []
                        for b in range(nq // LANES):
                            a0, a1 = b * LANES // GRID_W, (b + 1) * LANES // GRID_W - 1
                            lo = max(a0 * GRID_W, s)
                            hi = min((a1 + NA_ROWS) * GRID_W, s + tk)
                            band.append((lo - s, hi - s))
                        bias_fn = (lambda lo, hi, cs, s=s, hh=hh, var=var:
                                   bias_ref[var, hh, s + lo:s + hi, cs])
                        chunks.append((k, v, bias_fn, band))
                    else:
                        chunks.append((k, v, bias_ref[var, hh, s:s + tk, :]))
                chunk_lists.append(chunks + [(kc, vc_ref[:, sl], None)])
        outs = _attend_t_multi(qps, chunk_lists)
        for u in range(len(groups)):
            o_ref[pl.ds(q0s[u], nq), :] = jnp.concatenate(outs[2 * u:2 * u + 2], axis=0).T.astype(BF16)

    run([(0, 0, False), (n_groups - 1, 2, False)])

    interior = n_groups - 2
    per_trip = max([d for d in range(1, MAX_GROUPS_PER_TRIP + 1) if interior % d == 0], default=0)
    if interior == per_trip:
        if interior:
            run([(1 + u, 1, True) for u in range(per_trip)])
    else:
        def body(t, carry):
            run([(1 + t * per_trip + u, 1, True) for u in range(per_trip)])
            return carry

        lax.fori_loop(0, interior // per_trip, body, 0)


def _attn_b(qb, kb, vb1, kbc, vbc1, bias_t, *, batch, seq, ctx_len, tk):
    rows_n = seq // GRID_W
    nq = Q_ROWS * GRID_W
    nw = WIN_ROWS * GRID_W
    lat = lambda w: pl.BlockSpec((seq, w), lambda hp, b: (b, hp))
    ctx = lambda w: pl.BlockSpec((ctx_len, w), lambda hp, b: (b, hp))
    return pl.pallas_call(
        functools.partial(_attn_b_kernel, rows_n=rows_n, tk=tk),
        grid=(B_HEADS // 2, batch),
        in_specs=[lat(LANES), lat(LANES), lat(2 * LANES), ctx(LANES), ctx(2 * LANES),
                  pl.BlockSpec((3, 2, nw, nq), lambda hp, b: (0, hp, 0, 0))],
        out_specs=lat(LANES),
        out_shape=jax.ShapeDtypeStruct((batch * seq, 512), BF16),
        compiler_params=_cparams("arbitrary", "arbitrary"),
        name="attn_b",
    )(qb, kb, vb1, kbc, vbc1, bias_t)


def _neighbourhood_bias_t(rpb, rows_n):
    kc = NA_COLS
    cols = np.arange(GRID_W)
    c0 = np.clip(cols - kc // 2, 0, GRID_W - kc)
    col_ok = (cols[None, :] >= c0[:, None]) & (cols[None, :] < c0[:, None] + kc)
    heads, n_dr, n_dc = rpb.shape
    dc = cols[:, None] - cols[None, :] + NA_COLS - 1
    onehot = (np.arange(n_dc)[:, None, None] == dc[None]).astype(np.float32)
    pad = Q_ROWS
    e_max = n_dr - 1 + 2 * pad
    t = jnp.einsum("hdj,jxy->hxdy", rpb[:, ::-1] * LOG2E, jnp.asarray(onehot), precision=lax.Precision.HIGHEST)
    t = jnp.where(col_ok.T[None, :, None, :], t, NEG)
    rev = jnp.pad(t, ((0, 0), (0, 0), (pad, pad), (0, 0))).reshape(heads, GRID_W, (e_max + 1) * GRID_W)
    nq = Q_ROWS * GRID_W
    width = -(-((e_max + 1) * GRID_W + LANES) // LANES) * LANES
    rev = jnp.pad(rev, ((0, 0), (0, 0), (0, width - rev.shape[2])))
    n_groups = rows_n // Q_ROWS
    plans = []
    for g in (0, min(1, n_groups - 1), n_groups - 1):
        w0 = int(np.clip(Q_ROWS * g - (WIN_ROWS - Q_ROWS) // 2, 0, rows_n - WIN_ROWS))
        plan = []
        for i in range(WIN_ROWS):
            d0 = w0 + i - Q_ROWS * g + NA_ROWS - 1
            ok = [a for a in range(Q_ROWS)
                  if int(np.clip(Q_ROWS * g + a - NA_ROWS // 2, 0, rows_n - NA_ROWS)) <= w0 + i
                  < int(np.clip(Q_ROWS * g + a - NA_ROWS // 2, 0, rows_n - NA_ROWS)) + NA_ROWS]
            assert ok == list(range(ok[0], ok[-1] + 1)) if ok else True
            plan.append(((e_max - pad - d0) * GRID_W, (ok[0], ok[-1] + 1) if ok else (0, 0)))
        plans.append(plan)
    return pl.pallas_call(
        functools.partial(_bias_kernel, plans=plans),
        grid=(3, heads),
        in_specs=[pl.BlockSpec((None, GRID_W, width), lambda v, h: (h, 0, 0))],
        out_specs=pl.BlockSpec((None, None, WIN_ROWS * GRID_W, nq), lambda v, h: (v, h, 0, 0)),
        out_shape=jax.ShapeDtypeStruct((3, heads, WIN_ROWS * GRID_W, nq), F32),
        compiler_params=_cparams("arbitrary", "arbitrary"),
        name="nbr_bias",
    )(rev)


def _bias_kernel(rev_ref, o_ref, *, plans):
    nq = o_ref.shape[1]
    q_row = lax.broadcasted_iota(jnp.int32, (GRID_W, nq), 1) // GRID_W
    for v, plan in enumerate(plans):
        @pl.when(pl.program_id(0) == v)
        def _(plan=plan):
            for i, (off, (lo, hi)) in enumerate(plan):
                rows = slice(i * GRID_W, (i + 1) * GRID_W)
                if lo >= hi:
                    o_ref[rows, :] = jnp.full((GRID_W, nq), NEG, F32)
                    continue
                al = off // LANES * LANES
                strip = rev_ref[:, al:al + nq + LANES][:, off - al:off - al + nq]
                o_ref[rows, :] = jnp.where((q_row >= lo) & (q_row < hi), strip, NEG)


def _attn_ctx_kernel(qa_ref, ka_ref, va_ref, qb_ref, kb_ref, vb_ref, oa_ref, ob_ref, *, lc):
    lane_hi = lax.broadcasted_iota(jnp.int32, (lc, LANES), 1) >= 64
    ka = ka_ref[...]
    oa_ref[...] = _gqa_tile_t(
        qa_ref[...], lambda g: [(ka, va_ref[:, g * LANES:(g + 1) * LANES], None)], lc, lane_hi).astype(BF16)
    for hp in range(B_HEADS // 2):
        sl = slice(hp * LANES, (hp + 1) * LANES)
        k = kb_ref[:, sl]
        chunks = lambda hh: [(k, vb_ref[:, (2 * hp + hh) * LANES:(2 * hp + hh + 1) * LANES], None)]
        ob_ref[:, sl] = _mha_pair_t(qb_ref[:, sl], chunks, lane_hi).astype(BF16)


def _attn_ctx(qac, kac, vac1, qbc, kbc, vbc1, *, batch, ctx_len):
    t = lambda w: pl.BlockSpec((ctx_len, w), lambda b: (b, 0))
    o = jax.ShapeDtypeStruct((batch * ctx_len, 512), BF16)
    return pl.pallas_call(
        functools.partial(_attn_ctx_kernel, lc=ctx_len),
        grid=(batch,),
        in_specs=[t(512), t(128), t(256), t(512), t(512), t(1024)],
        out_specs=[t(512), t(512)],
        out_shape=[o, o],
        compiler_params=_cparams("arbitrary"),
        name="attn_ctx",
    )(qac, kac, vac1, qbc, kbc, vbc1)


def _dft_kernel(ac_ref, as_ref, bc_ref, bs_ref, uc_ref, us_ref, o_ref, ct_ref, st_ref, *, bm):
    i = pl.program_id(0)

    @pl.when(pl.program_id(1) == 0)
    def _():
        bc = bc_ref[...]
        bs = bs_ref[...]
        for r in range(bm // GRID_W):
            a_c = ac_ref[pl.ds(i * (bm // GRID_W) + r, 1), :]
            a_s = as_ref[pl.ds(i * (bm // GRID_W) + r, 1), :]
            ct_ref[r * GRID_W:(r + 1) * GRID_W, :] = (a_c * bc - a_s * bs).astype(BF16)
            st_ref[r * GRID_W:(r + 1) * GRID_W, :] = (a_s * bc + a_c * bs).astype(BF16)

    o_ref[...] = (_dot(ct_ref[...], uc_ref[...]) - _dot(st_ref[...], us_ref[...])).astype(BF16)


def _dft(tabs, uc, us):
    n, width = uc.shape
    bm = min(n, 1024)
    bn = min(width, 512)
    u_spec = pl.BlockSpec((n, bn), lambda i, j: (0, j))
    return pl.pallas_call(
        functools.partial(_dft_kernel, bm=bm),
        grid=(n // bm, width // bn),
        in_specs=[_resident_spec(t) for t in tabs] + [u_spec, u_spec],
        out_specs=pl.BlockSpec((bm, bn), lambda i, j: (i, j)),
        out_shape=jax.ShapeDtypeStruct((n, width), BF16),
        scratch_shapes=[pltpu.VMEM((bm, n), BF16), pltpu.VMEM((bm, n), BF16)],
        compiler_params=_cparams("arbitrary", "arbitrary"),
        name="dft",
    )(*tabs, uc, us)


def _dft_factor_tables(n):
    r = n // GRID_W
    k = np.arange(n, dtype=np.int64)
    ang_a = (2.0 * np.pi / r) * ((np.arange(r)[:, None] * k[None, :]) % r)
    ang_b = (2.0 * np.pi / n) * ((np.arange(GRID_W)[:, None] * k[None, :]) % n)
    pad = ((0, (-r) % 8), (0, 0))
    scale = n ** -0.5
    return tuple(jnp.asarray(t, dtype=F32) for t in (
        np.pad(np.cos(ang_a), pad), np.pad(np.sin(ang_a), pad), np.cos(ang_b) * scale, np.sin(ang_b) * scale))


def _merge_kernel(x_ref, mod_ref, oa_ref, ob_ref, oc_ref, wg_ref, wb_ref, wo_ref, g_ref, b_ref, o_ref, *, alpha):
    mod = mod_ref[...]
    sh = mod[:, 0:D_MODEL]
    sc = mod[:, D_MODEL:2 * D_MODEL]
    gate = mod[:, 2 * D_MODEL:3 * D_MODEL]
    blocks = _row_blocks(x_ref.shape[0])
    xs = [x_ref[b, :] for b in blocks]
    hs = [(_layernorm(x) * (1.0 + sc) + sh).astype(BF16) for x in xs]
    ys = [None] * len(blocks)
    for i, br_ref in enumerate((oa_ref, ob_ref, oc_ref)):
        for r, b in enumerate(blocks):
            gz = _dot(hs[r], wg_ref[:, i * D_MODEL:(i + 1) * D_MODEL])
            t = _dot(br_ref[b, :], wb_ref[i]) * (1.0 / (1.0 + jnp.exp(-gz)))
            ys[r] = t if ys[r] is None else ys[r] + t
    for r, b in enumerate(blocks):
        yo = _dot(ys[r].astype(BF16), wo_ref[...])
        o_ref[b, :] = _layernorm(alpha * xs[r] + gate * yo) * g_ref[...] + b_ref[...]


def _merge(x2, mod3, oa, ob, oc, w_gate, w_branch, w_out, ln_g, ln_b, *, batch, seq, tm, alpha):
    rows = batch * seq
    nl = seq // tm
    tok = lambda w: pl.BlockSpec((tm, w), lambda i: (i, 0))
    full = _resident_spec
    return pl.pallas_call(
        functools.partial(_merge_kernel, alpha=alpha),
        grid=(rows // tm,),
        in_specs=[
            tok(D_MODEL),
            pl.BlockSpec((None, 1, mod3.shape[2]), lambda i: (i // nl, 0, 0)),
            tok(512), tok(512),
            pl.BlockSpec((tm, 512), lambda i: (i % nl, i // nl)),
            full(w_gate), full(w_branch), full(w_out), full(ln_g), full(ln_b),
        ],
        out_specs=tok(D_MODEL),
        out_shape=jax.ShapeDtypeStruct((rows, D_MODEL), F32),
        compiler_params=_cparams("arbitrary"),
        name="merge",
    )(x2, mod3, oa, ob, oc, w_gate, w_branch, w_out, ln_g, ln_b)


FF_CHUNKS = ((0, 768), (768, 1536), (1536, 2304), (2304, 2816))


def _ffn_kernel(x_ref, mod_ref, wgu_ref, wd_ref, g_ref, b_ref, o_ref, *, alpha):
    mod = mod_ref[...]
    sh = mod[:, 3 * D_MODEL:4 * D_MODEL]
    sc = mod[:, 4 * D_MODEL:5 * D_MODEL]
    gate = mod[:, 5 * D_MODEL:6 * D_MODEL]
    blocks = _row_blocks(x_ref.shape[0])
    xs = [x_ref[b, :] for b in blocks]
    hs = [(_layernorm(x) * (1.0 + sc) + sh).astype(BF16) for x in xs]
    fs = [None] * len(blocks)
    for s, e in FF_CHUNKS:
        for i in range(len(blocks)):
            g = _dot(hs[i], wgu_ref[:, s:e])
            u = _dot(hs[i], wgu_ref[:, D_FF + s:D_FF + e])
            a = (g * (1.0 / (1.0 + jnp.exp(-g))) * u).astype(BF16)
            t = _dot(a, wd_ref[s:e, :])
            fs[i] = t if fs[i] is None else fs[i] + t
    for i, b in enumerate(blocks):
        o_ref[b, :] = _layernorm(alpha * xs[i] + gate * fs[i]) * g_ref[...] + b_ref[...]


def _ffn(x2, mod3, w_gu, w_down, ln_g, ln_b, *, batch, seq, tm, alpha):
    rows = batch * seq
    nl = seq // tm
    tok = pl.BlockSpec((tm, D_MODEL), lambda i: (i, 0))
    full = _resident_spec
    return pl.pallas_call(
        functools.partial(_ffn_kernel, alpha=alpha),
        grid=(rows // tm,),
        in_specs=[tok, pl.BlockSpec((None, 1, mod3.shape[2]), lambda i: (i // nl, 0, 0)),
                  full(w_gu), full(w_down), full(ln_g), full(ln_b)],
        out_specs=tok,
        out_shape=jax.ShapeDtypeStruct((rows, D_MODEL), F32),
        compiler_params=_cparams("arbitrary"),
        name="ffn",
    )(x2, mod3, w_gu, w_down, ln_g, ln_b)


def _rope_tables(seq):
    quarter = HEAD_DIM // 4
    pos = np.arange(seq)
    freqs = ROPE_THETA ** (-np.arange(quarter, dtype=np.float64) / quarter)
    ar = (pos // GRID_W)[:, None] * freqs
    ac = (pos % GRID_W)[:, None] * freqs
    cos64 = np.concatenate([np.cos(ar), np.cos(ar), np.cos(ac), np.cos(ac)], axis=1)
    sin64 = np.concatenate([-np.sin(ar), np.sin(ar), -np.sin(ac), np.sin(ac)], axis=1)
    return jnp.asarray(np.tile(cos64, (1, 2)), dtype=F32), jnp.asarray(np.tile(sin64, (1, 2)), dtype=F32)


def _head_mean_matrix(width):
    idx = np.arange(width) // HEAD_DIM
    return jnp.asarray((idx[:, None] == idx[None, :]).astype(np.float32) / HEAD_DIM).astype(BF16)


def _channel_dft_matrix():
    idx = np.arange(C_GROUP_W)
    ang = (2.0 * np.pi / C_GROUP_W) * ((idx[:, None] * idx[None, :]) % C_GROUP_W)
    scale = C_GROUP_W ** -0.5
    mat = np.concatenate([np.cos(ang), np.sin(ang)], axis=1) * scale
    return jnp.asarray(mat, dtype=F32).astype(BF16)


def kernel(x, c, ctx, c_ctx, w_ada, b_ada, w_in, q_norm, k_norm, rpb, w_branch, w_out,
           ln1_g, ln1_b, w_gu, w_down, ln2_g, ln2_b):
    batch, seq, d = x.shape
    ctx_len = ctx.shape[1]
    depth = w_ada.shape[0]
    n_groups = seq // (Q_ROWS * GRID_W)
    assert d == D_MODEL and seq % (Q_ROWS * GRID_W) == 0 and seq // GRID_W >= WIN_ROWS
    assert n_groups >= 2
    assert ctx_len % LANES == 0
    alpha = (2.0 * depth) ** 0.25
    tm = 1024
    tmc = min(ctx_len, 1024)
    assert seq % tm == 0 and tm % ROW_BLOCK == 0 and tmc % ROW_BLOCK == 0

    pad = (-(batch + 1)) % 8
    cc = jnp.concatenate([c, c_ctx[None, :], jnp.zeros((pad, d), F32)], axis=0)
    mods = _ada(cc, w_ada, b_ada)

    cos_t, sin_t = _rope_tables(seq)
    cos_c = jnp.ones((ctx_len, LANES), F32)
    sin_c = jnp.zeros((ctx_len, LANES), F32)
    g512 = _head_mean_matrix(512)
    g128 = _head_mean_matrix(128)
    dft_c = _channel_dft_matrix()
    tabs_l = _dft_factor_tables(seq)
    tabs_c = _dft_factor_tables(ctx_len)

    xl = x.reshape(batch * seq, d)
    xc = ctx.reshape(batch * ctx_len, d)
    for l in range(depth):
        with_ctx = l < depth - 1
        w_qkvu = w_in[l, :, :N_QKVU].astype(BF16)
        w_gate = w_in[l, :, N_QKVU:].astype(BF16)
        wb = w_branch[l].astype(BF16)
        wo = w_out[l].astype(BF16)
        wgu = w_gu[l].astype(BF16)
        wd = w_down[l].astype(BF16)
        q_gain = jnp.tile(q_norm[l] * Q_SCALE, A_HEADS)[None, :]
        k_gain = jnp.tile(k_norm[l], A_KV_HEADS)[None, :]
        mod_l = mods[l, :batch][:, None, :]
        mod_c = jnp.broadcast_to(mods[l, batch][None, None, :], (batch, 1, 6 * d))
        ln1 = (ln1_g[l][None, :], ln1_b[l][None, :])
        ln2 = (ln2_g[l][None, :], ln2_b[l][None, :])

        qa, ka, va1, qb, kb, vb1, uc, us = _inproj(
            xl, mod_l, w_qkvu, q_gain, k_gain, cos_t, sin_t, g512, g128, dft_c, batch=batch, seq=seq, tm=tm)
        qac, kac, vac1, qbc, kbc, vbc1, ucc, usc = _inproj(
            xc, mod_c, w_qkvu, q_gain, k_gain, cos_c, sin_c, g512, g128, dft_c, batch=batch, seq=ctx_len, tm=tmc)

        score_bound = (1.01 * HEAD_DIM * jnp.max(jnp.abs(q_gain)) * jnp.max(jnp.abs(k_gain))).reshape(1)
        oa = _attn_a(score_bound, qa, ka, va1, kac, vac1, batch=batch, seq=seq, ctx_len=ctx_len, tq=256, tk=256)
        bias_t = _neighbourhood_bias_t(rpb[l], seq // GRID_W)
        ob = _attn_b(qb, kb, vb1, kbc, vbc1, bias_t, batch=batch, seq=seq, ctx_len=ctx_len, tk=512)
        oc = _dft(tabs_l, uc, us)

        x1 = _merge(xl, mod_l, oa, ob, oc, w_gate, wb, wo, *ln1, batch=batch, seq=seq, tm=tm, alpha=alpha)
        xl = _ffn(x1, mod_l, wgu, wd, *ln2, batch=batch, seq=seq, tm=tm, alpha=alpha)

        if with_ctx:
            oac, obc = _attn_ctx(qac, kac, vac1, qbc, kbc, vbc1, batch=batch, ctx_len=ctx_len)
            occ = _dft(tabs_c, ucc, usc)
            xc1 = _merge(xc, mod_c, oac, obc, occ, w_gate, wb, wo, *ln1,
                         batch=batch, seq=ctx_len, tm=tmc, alpha=alpha)
            xc = _ffn(xc1, mod_c, wgu, wd, *ln2, batch=batch, seq=ctx_len, tm=tmc, alpha=alpha)
    return xl.reshape(batch, seq, d)
```

```python
import functools
import math

import numpy as np
import jax
import jax.numpy as jnp
from jax import lax
from jax.experimental import pallas as pl
from jax.experimental.pallas import tpu as pltpu

F32 = jnp.float32
BF16 = jnp.bfloat16

D_MODEL = 1024
GRID_W = 64
HEAD_DIM = 64
A_HEADS = 8
A_KV_HEADS = 2
A_GROUP = A_HEADS // A_KV_HEADS
B_HEADS = 8
NA_ROWS = 8
NA_COLS = 16
C_GROUPS = 4
C_GROUP_W = 128
BRANCH_W = 512
N_BRANCH = 3
D_FF = 2816
ROPE_THETA = 10000.0
LN_EPS = 1e-6
RMS_EPS = 1e-6
N_QKVU = 2816
Q_ROWS = 8
WIN_ROWS = 16
MAX_GROUPS_PER_TRIP = 6
ROW_BLOCK = 256
MAX_SHIFT_BOUND = 40.0
NEG = -1e30
LOG2E = math.log2(math.e)
Q_SCALE = HEAD_DIM ** -0.5 * LOG2E
LANES = 128
VMEM_LIMIT = 56 * 1024 * 1024


def _cparams(*sem):
    return pltpu.CompilerParams(dimension_semantics=sem, vmem_limit_bytes=VMEM_LIMIT)


def _resident_spec(a):
    return pl.BlockSpec(a.shape, lambda *_: (0,) * a.ndim, pipeline_mode=pl.Buffered(1))


def _dot(a, b):
    return jnp.dot(a, b, preferred_element_type=F32)


def _dot_nt(a, b):
    return lax.dot_general(a, b, (((1,), (1,)), ((), ())), preferred_element_type=F32)


def _dot_tn(a, b):
    return lax.dot_general(a, b, (((0,), (0,)), ((), ())), preferred_element_type=F32)


def _row_blocks(rows):
    return [slice(r, r + ROW_BLOCK) for r in range(0, rows, ROW_BLOCK)]


def _layernorm(x):
    mu = jnp.mean(x, axis=-1, keepdims=True)
    xc = x - mu
    var = jnp.mean(xc * xc, axis=-1, keepdims=True)
    return xc * lax.rsqrt(var + LN_EPS)


def _ada_kernel(c_ref, w_ref, b_ref, o_ref):
    c = c_ref[...]
    a = c * (1.0 / (1.0 + jnp.exp(-c)))
    a_hi = a.astype(BF16)
    a_lo = (a - a_hi.astype(F32)).astype(BF16)
    w = w_ref[...]
    w_hi = w.astype(BF16)
    w_lo = (w - w_hi.astype(F32)).astype(BF16)
    o_ref[...] = _dot(a_hi, w_hi) + _dot(a_hi, w_lo) + _dot(a_lo, w_hi) + b_ref[...]


def _ada(cc, w_ada, b_ada):
    depth, d, n = w_ada.shape
    tn = 1536
    return pl.pallas_call(
        _ada_kernel,
        grid=(depth, n // tn),
        in_specs=[
            pl.BlockSpec((cc.shape[0], d), lambda l, j: (0, 0)),
            pl.BlockSpec((None, d, tn), lambda l, j: (l, 0, j)),
            pl.BlockSpec((None, 1, tn), lambda l, j: (l, 0, j)),
        ],
        out_specs=pl.BlockSpec((None, cc.shape[0], tn), lambda l, j: (l, 0, j)),
        out_shape=jax.ShapeDtypeStruct((depth, cc.shape[0], n), F32),
        compiler_params=_cparams("arbitrary", "arbitrary"),
        name="ada",
    )(cc, w_ada, b_ada.reshape(depth, 1, n))


def _swap16(x, lane_lo):
    up = pltpu.roll(x, 16, 1)
    dn = pltpu.roll(x, LANES - 16, 1)
    return jnp.where(lane_lo, dn, up)


def _with_ones(v):
    ones = jnp.ones((v.shape[0], HEAD_DIM), F32)
    parts = []
    for hd in range(v.shape[1] // HEAD_DIM):
        parts += [v[:, hd * HEAD_DIM:(hd + 1) * HEAD_DIM], ones]
    return jnp.concatenate(parts, axis=1).astype(BF16)


def _inproj_kernel(x_ref, mod_ref, w_ref, qg_ref, kg_ref, cos_ref, sin_ref, g512_ref, g128_ref, dft_ref,
                   qa_ref, ka_ref, va_ref, qb_ref, kb_ref, vb_ref, uc_ref, us_ref):
    mod = mod_ref[...]
    sh = mod[:, 0:D_MODEL]
    sc = mod[:, D_MODEL:2 * D_MODEL]
    h = (_layernorm(x_ref[...]) * (1.0 + sc) + sh).astype(BF16)

    cos = cos_ref[...]
    sin = sin_ref[...]
    lane_lo = (lax.broadcasted_iota(jnp.int32, cos.shape, 1) % 32) < 16

    def norm_rope(v, ms, gain):
        vn = v * lax.rsqrt(ms + RMS_EPS) * gain
        outs = []
        for j in range(v.shape[1] // LANES):
            t = vn[:, j * LANES:(j + 1) * LANES]
            outs.append(t * cos + _swap16(t, lane_lo) * sin)
        return outs[0] if len(outs) == 1 else jnp.concatenate(outs, axis=1)

    q = _dot(h, w_ref[:, 0:512])
    kv = _dot(h, w_ref[:, 512:768])
    qb_ref[...] = (_dot(h, w_ref[:, 768:1280]) * Q_SCALE).astype(BF16)
    kb_ref[...] = _dot(h, w_ref[:, 1280:1792]).astype(BF16)
    k = kv[:, 0:128]
    ms_q = _dot((q * q).astype(BF16), g512_ref[...])
    ms_k = _dot((k * k).astype(BF16), g128_ref[...])
    vb_ref[...] = _with_ones(_dot(h, w_ref[:, 1792:2304]))
    u = _dot(h, w_ref[:, 2304:2816]).astype(BF16)
    qa_ref[...] = norm_rope(q, ms_q, qg_ref[...]).astype(BF16)
    ka_ref[...] = norm_rope(k, ms_k, kg_ref[...]).astype(BF16)
    va_ref[...] = _with_ones(kv[:, 128:256])
    dft = dft_ref[...]
    for g in range(C_GROUPS):
        z = _dot(u[:, g * C_GROUP_W:(g + 1) * C_GROUP_W], dft)
        uc_ref[:, g * C_GROUP_W:(g + 1) * C_GROUP_W] = z[:, 0:C_GROUP_W].astype(BF16)
        us_ref[:, g * C_GROUP_W:(g + 1) * C_GROUP_W] = z[:, C_GROUP_W:2 * C_GROUP_W].astype(BF16)


def _inproj(x2, mod3, w_qkvu, q_gain, k_gain, cos_t, sin_t, g512, g128, dft_c, *, batch, seq, tm):
    rows = batch * seq
    nl = seq // tm
    tok = lambda w: pl.BlockSpec((tm, w), lambda i: (i, 0))
    full = _resident_spec
    fo = pl.BlockSpec((tm, 512), lambda i: (i % nl, i // nl))
    o_tok = lambda w: jax.ShapeDtypeStruct((rows, w), BF16)
    return pl.pallas_call(
        _inproj_kernel,
        grid=(rows // tm,),
        in_specs=[
            tok(D_MODEL),
            pl.BlockSpec((None, 1, mod3.shape[2]), lambda i: (i // nl, 0, 0)),
            full(w_qkvu), full(q_gain), full(k_gain),
            pl.BlockSpec((tm, LANES), lambda i: (i % nl, 0)),
            pl.BlockSpec((tm, LANES), lambda i: (i % nl, 0)),
            full(g512), full(g128), full(dft_c),
        ],
        out_specs=[tok(512), tok(128), tok(256), tok(512), tok(512), tok(1024), fo, fo],
        out_shape=[o_tok(512), o_tok(128), o_tok(256), o_tok(512), o_tok(512), o_tok(1024),
                   jax.ShapeDtypeStruct((seq, batch * 512), BF16),
                   jax.ShapeDtypeStruct((seq, batch * 512), BF16)],
        compiler_params=_cparams("arbitrary"),
        name="inproj",
    )(x2, mod3, w_qkvu, q_gain, k_gain, cos_t, sin_t, g512, g128, dft_c)


def _attend_t_multi(qps, chunk_lists, shift=None):
    n = len(qps)
    m = [None] * n
    acc = [None] * n
    units = [(i, c) for c in range(len(chunk_lists[0])) for i in range(n)]

    def scores(u):
        i, c = u
        chunk = chunk_lists[i][c]
        s = _dot_nt(chunk[0], qps[i])
        return s if chunk[2] is None or len(chunk) == 4 else s + chunk[2]

    def banded(s, bias_fn, band):
        cols = [slice(b * LANES, (b + 1) * LANES) for b in range(len(band))]
        subs = [s[lo:hi, cs] + bias_fn(lo, hi, cs) for (lo, hi), cs in zip(band, cols)]
        mc = jnp.concatenate([jnp.max(x, axis=0, keepdims=True) for x in subs], axis=1)

        def probs(m_ref):
            out = []
            for x, (lo, hi), cs in zip(subs, band, cols):
                parts = [jnp.exp2(x - m_ref[:, cs]).astype(BF16)]
                if lo:
                    parts.insert(0, jnp.zeros((lo, LANES), BF16))
                if s.shape[0] - hi:
                    parts.append(jnp.zeros((s.shape[0] - hi, LANES), BF16))
                out.append(parts[0] if len(parts) == 1 else jnp.concatenate(parts, axis=0))
            return jnp.concatenate(out, axis=1)

        return mc, probs

    s_next = scores(units[0])
    for idx, (i, c) in enumerate(units):
        s = s_next
        if idx + 1 < len(units):
            s_next = scores(units[idx + 1])
        chunk = chunk_lists[i][c]
        v1 = chunk[1]
        if shift is not None:
            d = _dot_tn(v1, jnp.exp2(s - shift).astype(BF16))
            acc[i] = d if acc[i] is None else acc[i] + d
            continue
        if len(chunk) == 4:
            mc, probs = banded(s, chunk[2], chunk[3])
        else:
            mc, probs = jnp.max(s, axis=0, keepdims=True), lambda m_ref, s=s: jnp.exp2(s - m_ref).astype(BF16)
        if m[i] is None:
            m[i] = mc
            acc[i] = _dot_tn(v1, probs(mc))
        else:
            m_new = jnp.maximum(m[i], mc)
            acc[i] = jnp.exp2(m[i] - m_new) * acc[i] + _dot_tn(v1, probs(m_new))
            m[i] = m_new
    return [a[0:HEAD_DIM, :] / a[HEAD_DIM:HEAD_DIM + 1, :] for a in acc]


def _place_head(q128, src_half, dst_half, lane_hi):
    x = q128.astype(F32)
    if src_half != dst_half:
        x = pltpu.roll(x, 64, 1)
    keep = lane_hi if dst_half == 1 else jnp.logical_not(lane_hi)
    return jnp.where(keep, x, 0.0).astype(BF16)


def _gqa_tile_t(q, chunks_fn, tq, lane_hi, shift=None):
    qps = []
    for g in range(A_KV_HEADS):
        parts = []
        for j in range(A_GROUP):
            hd = A_GROUP * g + j
            parts.append(_place_head(q[:, LANES * (hd // 2):LANES * (hd // 2 + 1)], hd % 2, g, lane_hi))
        qps.append(jnp.concatenate(parts, axis=0))
    outs = _attend_t_multi(qps, [chunks_fn(g) for g in range(A_KV_HEADS)], shift)
    heads_t = [o_t[:, j * tq:(j + 1) * tq] for o_t in outs for j in range(A_GROUP)]
    return jnp.concatenate(heads_t, axis=0).T


def _mha_pair_t(q128, chunks_fn, lane_hi):
    outs = _attend_t_multi([_place_head(q128, hh, hh, lane_hi) for hh in range(2)],
                           [chunks_fn(hh) for hh in range(2)])
    return jnp.concatenate(outs, axis=0).T


def _attn_a_kernel(bound_ref, q_ref, k_ref, v_ref, kc_ref, vc_ref, o_ref, *, tq, chunk_bounds):
    lane_hi = lax.broadcasted_iota(jnp.int32, (tq, LANES), 1) >= 64

    def chunks(g):
        sl = slice(g * LANES, (g + 1) * LANES)
        return ([(kc_ref[...], vc_ref[:, sl], None)]
                + [(k_ref[s:e, :], v_ref[s:e, sl], None) for s, e in chunk_bounds])

    bound = bound_ref[0]
    o_ref[...] = lax.cond(
        bound <= MAX_SHIFT_BOUND,
        lambda: _gqa_tile_t(q_ref[...], chunks, tq, lane_hi, shift=bound),
        lambda: _gqa_tile_t(q_ref[...], chunks, tq, lane_hi),
    ).astype(BF16)


def _attn_a(score_bound, qa, ka, va1, kac, vac1, *, batch, seq, ctx_len, tq, tk):
    bounds = [(i, min(i + tk, seq)) for i in range(0, seq, tk)]
    nq = seq // tq
    return pl.pallas_call(
        functools.partial(_attn_a_kernel, tq=tq, chunk_bounds=bounds),
        grid=(batch, nq),
        in_specs=[pl.BlockSpec(memory_space=pltpu.SMEM),
                  pl.BlockSpec((tq, 512), lambda b, i: (b * nq + i, 0)),
                  pl.BlockSpec((seq, LANES), lambda b, i: (b, 0)),
                  pl.BlockSpec((seq, 2 * LANES), lambda b, i: (b, 0)),
                  pl.BlockSpec((ctx_len, LANES), lambda b, i: (b, 0)),
                  pl.BlockSpec((ctx_len, 2 * LANES), lambda b, i: (b, 0))],
        out_specs=pl.BlockSpec((tq, 512), lambda b, i: (b * nq + i, 0)),
        out_shape=jax.ShapeDtypeStruct((batch * seq, 512), BF16),
        compiler_params=_cparams("arbitrary", "arbitrary"),
        name="attn_a",
    )(score_bound, qa, ka, va1, kac, vac1)


def _attn_b_kernel(q_ref, k_ref, v_ref, kc_ref, vc_ref, bias_ref, o_ref, *, rows_n, tk):
    nq = Q_ROWS * GRID_W
    nw = WIN_ROWS * GRID_W
    n_groups = rows_n // Q_ROWS
    lane_hi = lax.broadcasted_iota(jnp.int32, (nq, LANES), 1) >= 64
    kc = kc_ref[...]

    def key_band(g_pattern, b):
        w0 = int(np.clip(Q_ROWS * g_pattern - (WIN_ROWS - Q_ROWS) // 2, 0, rows_n - WIN_ROWS))
        r0s = [int(np.clip(Q_ROWS * g_pattern + a - NA_ROWS // 2, 0, rows_n - NA_ROWS)) - w0
               for a in range(b * LANES // GRID_W, (b + 1) * LANES // GRID_W)]
        return min(r0s) * GRID_W, (max(r0s) + NA_ROWS) * GRID_W

    def run(groups):
        qps, chunk_lists, q0s = [], [], []
        for g, var, g_pattern in groups:
            if isinstance(g, int):
                q0 = g * nq
                k0 = int(np.clip(Q_ROWS * g - (WIN_ROWS - Q_ROWS) // 2, 0, rows_n - WIN_ROWS)) * GRID_W
            else:
                q0 = pl.multiple_of(g * nq, nq)
                w0 = jnp.clip(Q_ROWS * g - (WIN_ROWS - Q_ROWS) // 2, 0, rows_n - WIN_ROWS)
                k0 = pl.multiple_of(w0 * GRID_W, GRID_W)
            q128 = q_ref[pl.ds(q0, nq), :]
            q0s.append(q0)
            for hh in range(2):
                sl = slice(hh * LANES, (hh + 1) * LANES)
                qps.append(_place_head(q128, hh, hh, lane_hi))
                chunks = []
                for s in range(0, nw, tk):
                    k = k_ref[pl.ds(k0 + s, tk), :]
                    v = v_ref[pl.ds(k0 + s, tk), sl]
                    band = []
                    for b in range(nq // LANES):
                        lo, hi = key_band(g_pattern, b)
                        band.append((max(lo, s) - s, min(hi, s + tk) - s))
                    if all(lo < hi for lo, hi in band):
                        bias_fn = (lambda lo, hi, cs, s=s, hh=hh, var=var:
                                   bias_ref[var, hh, s + lo:s + hi, cs])
                        chunks.append((k, v, bias_fn, band))
                    else:
                        chunks.append((k, v, bias_ref[var, hh, s:s + tk, :]))
                chunk_lists.append(chunks + [(kc, vc_ref[:, sl], None)])
        outs = _attend_t_multi(qps, chunk_lists)
        for u in range(len(groups)):
            o_ref[pl.ds(q0s[u], nq), :] = jnp.concatenate(outs[2 * u:2 * u + 2], axis=0).T.astype(BF16)

    edges = [(0, 0, 0), (n_groups - 1, 2, n_groups - 1)]
    interior = n_groups - 2
    per_trip = max([d for d in range(1, MAX_GROUPS_PER_TRIP + 1) if interior % d == 0], default=0)
    if interior == per_trip:
        run(edges + [(1 + u, 1, 1) for u in range(per_trip)])
    else:
        run(edges)

        def body(t, carry):
            run([(1 + t * per_trip + u, 1, 1) for u in range(per_trip)])
            return carry

        lax.fori_loop(0, interior // per_trip, body, 0)


def _attn_b(qb, kb, vb1, kbc, vbc1, bias_t, *, batch, seq, ctx_len, tk):
    rows_n = seq // GRID_W
    nq = Q_ROWS * GRID_W
    nw = WIN_ROWS * GRID_W
    lat = lambda w: pl.BlockSpec((seq, w), lambda hp, b: (b, hp))
    ctx = lambda w: pl.BlockSpec((ctx_len, w), lambda hp, b: (b, hp))
    return pl.pallas_call(
        functools.partial(_attn_b_kernel, rows_n=rows_n, tk=tk),
        grid=(B_HEADS // 2, batch),
        in_specs=[lat(LANES), lat(LANES), lat(2 * LANES), ctx(LANES), ctx(2 * LANES),
                  pl.BlockSpec((3, 2, nw, nq), lambda hp, b: (0, hp, 0, 0))],
        out_specs=lat(LANES),
        out_shape=jax.ShapeDtypeStruct((batch * seq, 512), BF16),
        compiler_params=_cparams("arbitrary", "arbitrary"),
        name="attn_b",
    )(qb, kb, vb1, kbc, vbc1, bias_t)


def _neighbourhood_bias_t(rpb, rows_n):
    kc = NA_COLS
    cols = np.arange(GRID_W)
    c0 = np.clip(cols - kc // 2, 0, GRID_W - kc)
    col_ok = (cols[None, :] >= c0[:, None]) & (cols[None, :] < c0[:, None] + kc)
    heads, n_dr, n_dc = rpb.shape
    dc = cols[:, None] - cols[None, :] + NA_COLS - 1
    onehot = (np.arange(n_dc)[:, None, None] == dc[None]).astype(np.float32)
    pad = Q_ROWS
    e_max = n_dr - 1 + 2 * pad
    t = jnp.einsum("hdj,jxy->hxdy", rpb[:, ::-1] * LOG2E, jnp.asarray(onehot), precision=lax.Precision.HIGHEST)
    t = jnp.where(col_ok.T[None, :, None, :], t, NEG)
    rev = jnp.pad(t, ((0, 0), (0, 0), (pad, pad), (0, 0))).reshape(heads, GRID_W, (e_max + 1) * GRID_W)
    nq = Q_ROWS * GRID_W
    width = -(-((e_max + 1) * GRID_W + LANES) // LANES) * LANES
    rev = jnp.pad(rev, ((0, 0), (0, 0), (0, width - rev.shape[2])))
    n_groups = rows_n // Q_ROWS
    plans = []
    for g in (0, min(1, n_groups - 1), n_groups - 1):
        w0 = int(np.clip(Q_ROWS * g - (WIN_ROWS - Q_ROWS) // 2, 0, rows_n - WIN_ROWS))
        plan = []
        for i in range(WIN_ROWS):
            d0 = w0 + i - Q_ROWS * g + NA_ROWS - 1
            ok = [a for a in range(Q_ROWS)
                  if int(np.clip(Q_ROWS * g + a - NA_ROWS // 2, 0, rows_n - NA_ROWS)) <= w0 + i
                  < int(np.clip(Q_ROWS * g + a - NA_ROWS // 2, 0, rows_n - NA_ROWS)) + NA_ROWS]
            assert ok == list(range(ok[0], ok[-1] + 1)) if ok else True
            plan.append(((e_max - pad - d0) * GRID_W, (ok[0], ok[-1] + 1) if ok else (0, 0)))
        plans.append(plan)
    return pl.pallas_call(
        functools.partial(_bias_kernel, plans=plans),
        grid=(3, heads),
        in_specs=[pl.BlockSpec((None, GRID_W, width), lambda v, h: (h, 0, 0))],
        out_specs=pl.BlockSpec((None, None, WIN_ROWS * GRID_W, nq), lambda v, h: (v, h, 0, 0)),
        out_shape=jax.ShapeDtypeStruct((3, heads, WIN_ROWS * GRID_W, nq), F32),
        compiler_params=_cparams("arbitrary", "arbitrary"),
        name="nbr_bias",
    )(rev)


def _bias_kernel(rev_ref, o_ref, *, plans):
    nq = o_ref.shape[1]
    q_row = lax.broadcasted_iota(jnp.int32, (GRID_W, nq), 1) // GRID_W
    for v, plan in enumerate(plans):
        @pl.when(pl.program_id(0) == v)
        def _(plan=plan):
            for i, (off, (lo, hi)) in enumerate(plan):
                rows = slice(i * GRID_W, (i + 1) * GRID_W)
                if lo >= hi:
                    o_ref[rows, :] = jnp.full((GRID_W, nq), NEG, F32)
                    continue
                al = off // LANES * LANES
                strip = rev_ref[:, al:al + nq + LANES][:, off - al:off - al + nq]
                o_ref[rows, :] = jnp.where((q_row >= lo) & (q_row < hi), strip, NEG)


def _attn_ctx_kernel(qa_ref, ka_ref, va_ref, qb_ref, kb_ref, vb_ref, oa_ref, ob_ref, *, lc):
    lane_hi = lax.broadcasted_iota(jnp.int32, (lc, LANES), 1) >= 64
    ka = ka_ref[...]
    oa_ref[...] = _gqa_tile_t(
        qa_ref[...], lambda g: [(ka, va_ref[:, g * LANES:(g + 1) * LANES], None)], lc, lane_hi).astype(BF16)
    for hp in range(B_HEADS // 2):
        sl = slice(hp * LANES, (hp + 1) * LANES)
        k = kb_ref[:, sl]
        chunks = lambda hh: [(k, vb_ref[:, (2 * hp + hh) * LANES:(2 * hp + hh + 1) * LANES], None)]
        ob_ref[:, sl] = _mha_pair_t(qb_ref[:, sl], chunks, lane_hi).astype(BF16)


def _attn_ctx(qac, kac, vac1, qbc, kbc, vbc1, *, batch, ctx_len):
    t = lambda w: pl.BlockSpec((ctx_len, w), lambda b: (b, 0))
    o = jax.ShapeDtypeStruct((batch * ctx_len, 512), BF16)
    return pl.pallas_call(
        functools.partial(_attn_ctx_kernel, lc=ctx_len),
        grid=(batch,),
        in_specs=[t(512), t(128), t(256), t(512), t(512), t(1024)],
        out_specs=[t(512), t(512)],
        out_shape=[o, o],
        compiler_params=_cparams("arbitrary"),
        name="attn_ctx",
    )(qac, kac, vac1, qbc, kbc, vbc1)


def _dft_kernel(ac_ref, as_ref, bc_ref, bs_ref, uc_ref, us_ref, o_ref, ct_ref, st_ref, *, bm):
    i = pl.program_id(0)

    @pl.when(pl.program_id(1) == 0)
    def _():
        bc = bc_ref[...]
        bs = bs_ref[...]
        for r in range(bm // GRID_W):
            a_c = ac_ref[pl.ds(i * (bm // GRID_W) + r, 1), :]
            a_s = as_ref[pl.ds(i * (bm // GRID_W) + r, 1), :]
            ct_ref[r * GRID_W:(r + 1) * GRID_W, :] = (a_c * bc - a_s * bs).astype(BF16)
            st_ref[r * GRID_W:(r + 1) * GRID_W, :] = (a_s * bc + a_c * bs).astype(BF16)

    o_ref[...] = (_dot(ct_ref[...], uc_ref[...]) - _dot(st_ref[...], us_ref[...])).astype(BF16)


def _dft(tabs, uc, us):
    n, width = uc.shape
    bm = min(n, 1024)
    bn = min(width, 512)
    u_spec = pl.BlockSpec((n, bn), lambda i, j: (0, j))
    return pl.pallas_call(
        functools.partial(_dft_kernel, bm=bm),
        grid=(n // bm, width // bn),
        in_specs=[_resident_spec(t) for t in tabs] + [u_spec, u_spec],
        out_specs=pl.BlockSpec((bm, bn), lambda i, j: (i, j)),
        out_shape=jax.ShapeDtypeStruct((n, width), BF16),
        scratch_shapes=[pltpu.VMEM((bm, n), BF16), pltpu.VMEM((bm, n), BF16)],
        compiler_params=_cparams("arbitrary", "arbitrary"),
        name="dft",
    )(*tabs, uc, us)


def _dft_factor_tables(n):
    r = n // GRID_W
    k = np.arange(n, dtype=np.int64)
    ang_a = (2.0 * np.pi / r) * ((np.arange(r)[:, None] * k[None, :]) % r)
    ang_b = (2.0 * np.pi / n) * ((np.arange(GRID_W)[:, None] * k[None, :]) % n)
    pad = ((0, (-r) % 8), (0, 0))
    scale = n ** -0.5
    return tuple(jnp.asarray(t, dtype=F32) for t in (
        np.pad(np.cos(ang_a), pad), np.pad(np.sin(ang_a), pad), np.cos(ang_b) * scale, np.sin(ang_b) * scale))


def _merge_kernel(x_ref, mod_ref, oa_ref, ob_ref, oc_ref, wg_ref, wb_ref, wo_ref, g_ref, b_ref, o_ref, *, alpha):
    mod = mod_ref[...]
    sh = mod[:, 0:D_MODEL]
    sc = mod[:, D_MODEL:2 * D_MODEL]
    gate = mod[:, 2 * D_MODEL:3 * D_MODEL]
    blocks = _row_blocks(x_ref.shape[0])
    xs = [x_ref[b, :] for b in blocks]
    hs = [(_layernorm(x) * (1.0 + sc) + sh).astype(BF16) for x in xs]
    ys = [None] * len(blocks)
    for i, br_ref in enumerate((oa_ref, ob_ref, oc_ref)):
        for r, b in enumerate(blocks):
            gz = _dot(hs[r], wg_ref[:, i * D_MODEL:(i + 1) * D_MODEL])
            t = _dot(br_ref[b, :], wb_ref[i]) * (1.0 / (1.0 + jnp.exp(-gz)))
            ys[r] = t if ys[r] is None else ys[r] + t
    for r, b in enumerate(blocks):
        yo = _dot(ys[r].astype(BF16), wo_ref[...])
        o_ref[b, :] = _layernorm(alpha * xs[r] + gate * yo) * g_ref[...] + b_ref[...]


def _merge(x2, mod3, oa, ob, oc, w_gate, w_branch, w_out, ln_g, ln_b, *, batch, seq, tm, alpha):
    rows = batch * seq
    nl = seq // tm
    tok = lambda w: pl.BlockSpec((tm, w), lambda i: (i, 0))
    full = _resident_spec
    return pl.pallas_call(
        functools.partial(_merge_kernel, alpha=alpha),
        grid=(rows // tm,),
        in_specs=[
            tok(D_MODEL),
            pl.BlockSpec((None, 1, mod3.shape[2]), lambda i: (i // nl, 0, 0)),
            tok(512), tok(512),
            pl.BlockSpec((tm, 512), lambda i: (i % nl, i // nl)),
            full(w_gate), full(w_branch), full(w_out), full(ln_g), full(ln_b),
        ],
        out_specs=tok(D_MODEL),
        out_shape=jax.ShapeDtypeStruct((rows, D_MODEL), F32),
        compiler_params=_cparams("arbitrary"),
        name="merge",
    )(x2, mod3, oa, ob, oc, w_gate, w_branch, w_out, ln_g, ln_b)


FF_CHUNKS = ((0, 768), (768, 1536), (1536, 2304), (2304, 2816))


def _ffn_kernel(x_ref, mod_ref, wgu_ref, wd_ref, g_ref, b_ref, o_ref, *, alpha):
    mod = mod_ref[...]
    sh = mod[:, 3 * D_MODEL:4 * D_MODEL]
    sc = mod[:, 4 * D_MODEL:5 * D_MODEL]
    gate = mod[:, 5 * D_MODEL:6 * D_MODEL]
    blocks = _row_blocks(x_ref.shape[0])
    xs = [x_ref[b, :] for b in blocks]
    hs = [(_layernorm(x) * (1.0 + sc) + sh).astype(BF16) for x in xs]
    fs = [None] * len(blocks)
    for s, e in FF_CHUNKS:
        for i in range(len(blocks)):
            g = _dot(hs[i], wgu_ref[:, s:e])
            u = _dot(hs[i], wgu_ref[:, D_FF + s:D_FF + e])
            a = (g * (1.0 / (1.0 + jnp.exp(-g))) * u).astype(BF16)
            t = _dot(a, wd_ref[s:e, :])
            fs[i] = t if fs[i] is None else fs[i] + t
    for i, b in enumerate(blocks):
        o_ref[b, :] = _layernorm(alpha * xs[i] + gate * fs[i]) * g_ref[...] + b_ref[...]


def _ffn(x2, mod3, w_gu, w_down, ln_g, ln_b, *, batch, seq, tm, alpha):
    rows = batch * seq
    nl = seq // tm
    tok = pl.BlockSpec((tm, D_MODEL), lambda i: (i, 0))
    full = _resident_spec
    return pl.pallas_call(
        functools.partial(_ffn_kernel, alpha=alpha),
        grid=(rows // tm,),
        in_specs=[tok, pl.BlockSpec((None, 1, mod3.shape[2]), lambda i: (i // nl, 0, 0)),
                  full(w_gu), full(w_down), full(ln_g), full(ln_b)],
        out_specs=tok,
        out_shape=jax.ShapeDtypeStruct((rows, D_MODEL), F32),
        compiler_params=_cparams("arbitrary"),
        name="ffn",
    )(x2, mod3, w_gu, w_down, ln_g, ln_b)


def _rope_tables(seq):
    quarter = HEAD_DIM // 4
    pos = np.arange(seq)
    freqs = ROPE_THETA ** (-np.arange(quarter, dtype=np.float64) / quarter)
    ar = (pos // GRID_W)[:, None] * freqs
    ac = (pos % GRID_W)[:, None] * freqs
    cos64 = np.concatenate([np.cos(ar), np.cos(ar), np.cos(ac), np.cos(ac)], axis=1)
    sin64 = np.concatenate([-np.sin(ar), np.sin(ar), -np.sin(ac), np.sin(ac)], axis=1)
    return jnp.asarray(np.tile(cos64, (1, 2)), dtype=F32), jnp.asarray(np.tile(sin64, (1, 2)), dtype=F32)


def _head_mean_matrix(width):
    idx = np.arange(width) // HEAD_DIM
    return jnp.asarray((idx[:, None] == idx[None, :]).astype(np.float32) / HEAD_DIM).astype(BF16)


def _channel_dft_matrix():
    idx = np.arange(C_GROUP_W)
    ang = (2.0 * np.pi / C_GROUP_W) * ((idx[:, None] * idx[None, :]) % C_GROUP_W)
    scale = C_GROUP_W ** -0.5
    mat = np.concatenate([np.cos(ang), np.sin(ang)], axis=1) * scale
    return jnp.asarray(mat, dtype=F32).astype(BF16)


def kernel(x, c, ctx, c_ctx, w_ada, b_ada, w_in, q_norm, k_norm, rpb, w_branch, w_out,
           ln1_g, ln1_b, w_gu, w_down, ln2_g, ln2_b):
    batch, seq, d = x.shape
    ctx_len = ctx.shape[1]
    depth = w_ada.shape[0]
    n_groups = seq // (Q_ROWS * GRID_W)
    assert d == D_MODEL and seq % (Q_ROWS * GRID_W) == 0 and seq // GRID_W >= WIN_ROWS
    assert n_groups >= 2
    assert ctx_len % LANES == 0
    alpha = (2.0 * depth) ** 0.25
    tm = 1024
    tmc = min(ctx_len, 1024)
    assert seq % tm == 0 and tm % ROW_BLOCK == 0 and tmc % ROW_BLOCK == 0

    pad = (-(batch + 1)) % 8
    cc = jnp.concatenate([c, c_ctx[None, :], jnp.zeros((pad, d), F32)], axis=0)
    mods = _ada(cc, w_ada, b_ada)

    cos_t, sin_t = _rope_tables(seq)
    cos_c = jnp.ones((ctx_len, LANES), F32)
    sin_c = jnp.zeros((ctx_len, LANES), F32)
    g512 = _head_mean_matrix(512)
    g128 = _head_mean_matrix(128)
    dft_c = _channel_dft_matrix()
    tabs_l = _dft_factor_tables(seq)
    tabs_c = _dft_factor_tables(ctx_len)

    xl = x.reshape(batch * seq, d)
    xc = ctx.reshape(batch * ctx_len, d)
    for l in range(depth):
        with_ctx = l < depth - 1
        w_qkvu = w_in[l, :, :N_QKVU].astype(BF16)
        w_gate = w_in[l, :, N_QKVU:].astype(BF16)
        wb = w_branch[l].astype(BF16)
        wo = w_out[l].astype(BF16)
        wgu = w_gu[l].astype(BF16)
        wd = w_down[l].astype(BF16)
        q_gain = jnp.tile(q_norm[l] * Q_SCALE, A_HEADS)[None, :]
        k_gain = jnp.tile(k_norm[l], A_KV_HEADS)[None, :]
        mod_l = mods[l, :batch][:, None, :]
        mod_c = jnp.broadcast_to(mods[l, batch][None, None, :], (batch, 1, 6 * d))
        ln1 = (ln1_g[l][None, :], ln1_b[l][None, :])
        ln2 = (ln2_g[l][None, :], ln2_b[l][None, :])

        qa, ka, va1, qb, kb, vb1, uc, us = _inproj(
            xl, mod_l, w_qkvu, q_gain, k_gain, cos_t, sin_t, g512, g128, dft_c, batch=batch, seq=seq, tm=tm)
        qac, kac, vac1, qbc, kbc, vbc1, ucc, usc = _inproj(
            xc, mod_c, w_qkvu, q_gain, k_gain, cos_c, sin_c, g512, g128, dft_c, batch=batch, seq=ctx_len, tm=tmc)

        score_bound = (1.01 * HEAD_DIM * jnp.max(jnp.abs(q_gain)) * jnp.max(jnp.abs(k_gain))).reshape(1)
        oa = _attn_a(score_bound, qa, ka, va1, kac, vac1, batch=batch, seq=seq, ctx_len=ctx_len, tq=256, tk=256)
        bias_t = _neighbourhood_bias_t(rpb[l], seq // GRID_W)
        ob = _attn_b(qb, kb, vb1, kbc, vbc1, bias_t, batch=batch, seq=seq, ctx_len=ctx_len, tk=512)
        oc = _dft(tabs_l, uc, us)

        x1 = _merge(xl, mod_l, oa, ob, oc, w_gate, wb, wo, *ln1, batch=batch, seq=seq, tm=tm, alpha=alpha)
        xl = _ffn(x1, mod_l, wgu, wd, *ln2, batch=batch, seq=seq, tm=tm, alpha=alpha)

        if with_ctx:
            oac, obc = _attn_ctx(qac, kac, vac1, qbc, kbc, vbc1, batch=batch, ctx_len=ctx_len)
            occ = _dft(tabs_c, ucc, usc)
            xc1 = _merge(xc, mod_c, oac, obc, occ, w_gate, wb, wo, *ln1,
                         batch=batch, seq=ctx_len, tm=tmc, alpha=alpha)
            xc = _ffn(xc1, mod_c, wgu, wd, *ln2, batch=batch, seq=ctx_len, tm=tmc, alpha=alpha)
    return xl.reshape(batch, seq, d)
```

```python
import functools
import math

import numpy as np
import jax
import jax.numpy as jnp
from jax import lax
from jax.experimental import pallas as pl
from jax.experimental.pallas import tpu as pltpu

F32 = jnp.float32
BF16 = jnp.bfloat16

D_MODEL = 1024
GRID_W = 64
HEAD_DIM = 64
A_HEADS = 8
A_KV_HEADS = 2
A_GROUP = A_HEADS // A_KV_HEADS
B_HEADS = 8
NA_ROWS = 8
NA_COLS = 16
C_GROUPS = 4
C_GROUP_W = 128
D_FF = 2816
ROPE_THETA = 10000.0
LN_EPS = 1e-6
RMS_EPS = 1e-6
N_QKVU = 2816
Q_ROWS = 8
WIN_ROWS = 16
MAX_GROUPS_PER_TRIP = 6
ROW_BLOCK = 256
MAX_SHIFT_BOUND = 40.0
NEG = -1e30
LOG2E = math.log2(math.e)
Q_SCALE = HEAD_DIM ** -0.5 * LOG2E
LANES = 128
VMEM_LIMIT = 56 * 1024 * 1024


def _cparams(*sem):
    return pltpu.CompilerParams(dimension_semantics=sem, vmem_limit_bytes=VMEM_LIMIT)


def _resident_spec(a):
    return pl.BlockSpec(a.shape, lambda *_: (0,) * a.ndim, pipeline_mode=pl.Buffered(1))


def _dot(a, b):
    return jnp.dot(a, b, preferred_element_type=F32)


def _dot_nt(a, b):
    return lax.dot_general(a, b, (((1,), (1,)), ((), ())), preferred_element_type=F32)


def _dot_tn(a, b):
    return lax.dot_general(a, b, (((0,), (0,)), ((), ())), preferred_element_type=F32)


def _row_blocks(rows):
    return [slice(r, r + ROW_BLOCK) for r in range(0, rows, ROW_BLOCK)]


def _layernorm(x):
    mu = jnp.mean(x, axis=-1, keepdims=True)
    xc = x - mu
    var = jnp.mean(xc * xc, axis=-1, keepdims=True)
    return xc * lax.rsqrt(var + LN_EPS)


def _ada_kernel(c_ref, w_ref, b_ref, o_ref):
    c = c_ref[...]
    a = c * (1.0 / (1.0 + jnp.exp(-c)))
    a_hi = a.astype(BF16)
    a_lo = (a - a_hi.astype(F32)).astype(BF16)
    w = w_ref[...]
    w_hi = w.astype(BF16)
    w_lo = (w - w_hi.astype(F32)).astype(BF16)
    o_ref[...] = _dot(a_hi, w_hi) + _dot(a_hi, w_lo) + _dot(a_lo, w_hi) + b_ref[...]


def _ada(cc, w_ada, b_ada):
    depth, d, n = w_ada.shape
    tn = 1536
    return pl.pallas_call(
        _ada_kernel,
        grid=(depth, n // tn),
        in_specs=[
            pl.BlockSpec((cc.shape[0], d), lambda l, j: (0, 0)),
            pl.BlockSpec((None, d, tn), lambda l, j: (l, 0, j)),
            pl.BlockSpec((None, 1, tn), lambda l, j: (l, 0, j)),
        ],
        out_specs=pl.BlockSpec((None, cc.shape[0], tn), lambda l, j: (l, 0, j)),
        out_shape=jax.ShapeDtypeStruct((depth, cc.shape[0], n), F32),
        compiler_params=_cparams("arbitrary", "arbitrary"),
        name="ada",
    )(cc, w_ada, b_ada.reshape(depth, 1, n))


def _swap16(x, lane_lo):
    up = pltpu.roll(x, 16, 1)
    dn = pltpu.roll(x, LANES - 16, 1)
    return jnp.where(lane_lo, dn, up)


def _with_ones(v):
    ones = jnp.ones((v.shape[0], HEAD_DIM), F32)
    parts = []
    for hd in range(v.shape[1] // HEAD_DIM):
        parts += [v[:, hd * HEAD_DIM:(hd + 1) * HEAD_DIM], ones]
    return jnp.concatenate(parts, axis=1).astype(BF16)


def _inproj_kernel(x_ref, mod_ref, w_ref, qg_ref, kg_ref, cos_ref, sin_ref, g512_ref, g128_ref, dft_ref,
                   qa_ref, ka_ref, va_ref, qb_ref, kb_ref, vb_ref, uc_ref, us_ref):
    mod = mod_ref[...]
    sh = mod[:, 0:D_MODEL]
    sc = mod[:, D_MODEL:2 * D_MODEL]
    h = (_layernorm(x_ref[...]) * (1.0 + sc) + sh).astype(BF16)

    cos = cos_ref[...]
    sin = sin_ref[...]
    lane_lo = (lax.broadcasted_iota(jnp.int32, cos.shape, 1) % 32) < 16

    def norm_rope(v, ms, gain):
        vn = v * lax.rsqrt(ms + RMS_EPS) * gain
        outs = []
        for j in range(v.shape[1] // LANES):
            t = vn[:, j * LANES:(j + 1) * LANES]
            outs.append(t * cos + _swap16(t, lane_lo) * sin)
        return outs[0] if len(outs) == 1 else jnp.concatenate(outs, axis=1)

    q = _dot(h, w_ref[:, 0:512])
    kv = _dot(h, w_ref[:, 512:768])
    qb_ref[...] = (_dot(h, w_ref[:, 768:1280]) * Q_SCALE).astype(BF16)
    kb_ref[...] = _dot(h, w_ref[:, 1280:1792]).astype(BF16)
    k = kv[:, 0:128]
    ms_q = _dot((q * q).astype(BF16), g512_ref[...])
    ms_k = _dot((k * k).astype(BF16), g128_ref[...])
    vb_ref[...] = _with_ones(_dot(h, w_ref[:, 1792:2304]))
    u = _dot(h, w_ref[:, 2304:2816]).astype(BF16)
    qa_ref[...] = norm_rope(q, ms_q, qg_ref[...]).astype(BF16)
    ka_ref[...] = norm_rope(k, ms_k, kg_ref[...]).astype(BF16)
    va_ref[...] = _with_ones(kv[:, 128:256])
    dft = dft_ref[...]
    for g in range(C_GROUPS):
        z = _dot(u[:, g * C_GROUP_W:(g + 1) * C_GROUP_W], dft)
        uc_ref[:, g * C_GROUP_W:(g + 1) * C_GROUP_W] = z[:, 0:C_GROUP_W].astype(BF16)
        us_ref[:, g * C_GROUP_W:(g + 1) * C_GROUP_W] = z[:, C_GROUP_W:2 * C_GROUP_W].astype(BF16)


def _inproj(x2, mod3, w_qkvu, q_gain, k_gain, cos_t, sin_t, g512, g128, dft_c, *, batch, seq, tm):
    rows = batch * seq
    nl = seq // tm
    tok = lambda w: pl.BlockSpec((tm, w), lambda i: (i, 0))
    full = _resident_spec
    fo = pl.BlockSpec((tm, 512), lambda i: (i % nl, i // nl))
    o_tok = lambda w: jax.ShapeDtypeStruct((rows, w), BF16)
    return pl.pallas_call(
        _inproj_kernel,
        grid=(rows // tm,),
        in_specs=[
            tok(D_MODEL),
            pl.BlockSpec((None, 1, mod3.shape[2]), lambda i: (i // nl, 0, 0)),
            full(w_qkvu), full(q_gain), full(k_gain),
            pl.BlockSpec((tm, LANES), lambda i: (i % nl, 0)),
            pl.BlockSpec((tm, LANES), lambda i: (i % nl, 0)),
            full(g512), full(g128), full(dft_c),
        ],
        out_specs=[tok(512), tok(128), tok(256), tok(512), tok(512), tok(1024), fo, fo],
        out_shape=[o_tok(512), o_tok(128), o_tok(256), o_tok(512), o_tok(512), o_tok(1024),
                   jax.ShapeDtypeStruct((seq, batch * 512), BF16),
                   jax.ShapeDtypeStruct((seq, batch * 512), BF16)],
        compiler_params=_cparams("arbitrary"),
        name="inproj",
    )(x2, mod3, w_qkvu, q_gain, k_gain, cos_t, sin_t, g512, g128, dft_c)


def _attend_t_multi(qps, chunk_lists, shift=None):
    n = len(qps)
    m = [None] * n
    acc = [None] * n
    units = [(i, c) for c in range(len(chunk_lists[0])) for i in range(n)]

    def scores(u):
        i, c = u
        chunk = chunk_lists[i][c]
        s = _dot_nt(chunk[0], qps[i])
        return s if chunk[2] is None or len(chunk) == 4 else s + chunk[2]

    def banded(s, bias_fn, band):
        cols = [slice(b * LANES, (b + 1) * LANES) for b in range(len(band))]
        subs = [s[lo:hi, cs] + bias_fn(lo, hi, cs) for (lo, hi), cs in zip(band, cols)]
        mc = jnp.concatenate([jnp.max(x, axis=0, keepdims=True) for x in subs], axis=1)

        def probs(m_ref):
            out = []
            for x, (lo, hi), cs in zip(subs, band, cols):
                parts = [jnp.exp2(x - m_ref[:, cs]).astype(BF16)]
                if lo:
                    parts.insert(0, jnp.zeros((lo, LANES), BF16))
                if s.shape[0] - hi:
                    parts.append(jnp.zeros((s.shape[0] - hi, LANES), BF16))
                out.append(parts[0] if len(parts) == 1 else jnp.concatenate(parts, axis=0))
            return jnp.concatenate(out, axis=1)

        return mc, probs

    s_next = scores(units[0])
    for idx, (i, c) in enumerate(units):
        s = s_next
        if idx + 1 < len(units):
            s_next = scores(units[idx + 1])
        chunk = chunk_lists[i][c]
        v1 = chunk[1]
        if shift is not None:
            d = _dot_tn(v1, jnp.exp2(s - shift).astype(BF16))
            acc[i] = d if acc[i] is None else acc[i] + d
            continue
        if len(chunk) == 4:
            mc, probs = banded(s, chunk[2], chunk[3])
        else:
            mc, probs = jnp.max(s, axis=0, keepdims=True), lambda m_ref, s=s: jnp.exp2(s - m_ref).astype(BF16)
        if m[i] is None:
            m[i] = mc
            acc[i] = _dot_tn(v1, probs(mc))
        else:
            m_new = jnp.maximum(m[i], mc)
            acc[i] = jnp.exp2(m[i] - m_new) * acc[i] + _dot_tn(v1, probs(m_new))
            m[i] = m_new
    return [a[0:HEAD_DIM, :] / a[HEAD_DIM:HEAD_DIM + 1, :] for a in acc]


def _place_head(q128, src_half, dst_half, lane_hi):
    x = q128.astype(F32)
    if src_half != dst_half:
        x = pltpu.roll(x, 64, 1)
    keep = lane_hi if dst_half == 1 else jnp.logical_not(lane_hi)
    return jnp.where(keep, x, 0.0).astype(BF16)


def _gqa_tile_t(q, chunks_fn, tq, lane_hi, shift=None):
    qps = []
    for g in range(A_KV_HEADS):
        parts = []
        for j in range(A_GROUP):
            hd = A_GROUP * g + j
            parts.append(_place_head(q[:, LANES * (hd // 2):LANES * (hd // 2 + 1)], hd % 2, g, lane_hi))
        qps.append(jnp.concatenate(parts, axis=0))
    outs = _attend_t_multi(qps, [chunks_fn(g) for g in range(A_KV_HEADS)], shift)
    heads_t = [o_t[:, j * tq:(j + 1) * tq] for o_t in outs for j in range(A_GROUP)]
    return jnp.concatenate(heads_t, axis=0).T


def _mha_pair_t(q128, chunks_fn, lane_hi):
    outs = _attend_t_multi([_place_head(q128, hh, hh, lane_hi) for hh in range(2)],
                           [chunks_fn(hh) for hh in range(2)])
    return jnp.concatenate(outs, axis=0).T


def _attn_a_kernel(bound_ref, q_ref, k_ref, v_ref, kc_ref, vc_ref, o_ref, *, tq, chunk_bounds):
    lane_hi = lax.broadcasted_iota(jnp.int32, (tq, LANES), 1) >= 64

    def chunks(g):
        sl = slice(g * LANES, (g + 1) * LANES)
        return ([(kc_ref[...], vc_ref[:, sl], None)]
                + [(k_ref[s:e, :], v_ref[s:e, sl], None) for s, e in chunk_bounds])

    bound = bound_ref[0]
    o_ref[...] = lax.cond(
        bound <= MAX_SHIFT_BOUND,
        lambda: _gqa_tile_t(q_ref[...], chunks, tq, lane_hi, shift=bound),
        lambda: _gqa_tile_t(q_ref[...], chunks, tq, lane_hi),
    ).astype(BF16)


def _attn_a(score_bound, qa, ka, va1, kac, vac1, *, batch, seq, ctx_len, tq, tk):
    bounds = [(i, min(i + tk, seq)) for i in range(0, seq, tk)]
    nq = seq // tq
    return pl.pallas_call(
        functools.partial(_attn_a_kernel, tq=tq, chunk_bounds=bounds),
        grid=(batch, nq),
        in_specs=[pl.BlockSpec(memory_space=pltpu.SMEM),
                  pl.BlockSpec((tq, 512), lambda b, i: (b * nq + i, 0)),
                  pl.BlockSpec((seq, LANES), lambda b, i: (b, 0)),
                  pl.BlockSpec((seq, 2 * LANES), lambda b, i: (b, 0)),
                  pl.BlockSpec((ctx_len, LANES), lambda b, i: (b, 0)),
                  pl.BlockSpec((ctx_len, 2 * LANES), lambda b, i: (b, 0))],
        out_specs=pl.BlockSpec((tq, 512), lambda b, i: (b * nq + i, 0)),
        out_shape=jax.ShapeDtypeStruct((batch * seq, 512), BF16),
        compiler_params=_cparams("arbitrary", "arbitrary"),
        name="attn_a",
    )(score_bound, qa, ka, va1, kac, vac1)


def _attn_b_kernel(q_ref, k_ref, v_ref, kc_ref, vc_ref, bias_ref, o_ref, *, rows_n, tk):
    nq = Q_ROWS * GRID_W
    nw = WIN_ROWS * GRID_W
    n_groups = rows_n // Q_ROWS
    lane_hi = lax.broadcasted_iota(jnp.int32, (nq, LANES), 1) >= 64
    kc = kc_ref[...]

    def key_band(g_pattern, b):
        w0 = int(np.clip(Q_ROWS * g_pattern - (WIN_ROWS - Q_ROWS) // 2, 0, rows_n - WIN_ROWS))
        r0s = [int(np.clip(Q_ROWS * g_pattern + a - NA_ROWS // 2, 0, rows_n - NA_ROWS)) - w0
               for a in range(b * LANES // GRID_W, (b + 1) * LANES // GRID_W)]
        return min(r0s) * GRID_W, (max(r0s) + NA_ROWS) * GRID_W

    def run(groups):
        qps, chunk_lists, q0s = [], [], []
        for g, var, g_pattern in groups:
            if isinstance(g, int):
                q0 = g * nq
                k0 = int(np.clip(Q_ROWS * g - (WIN_ROWS - Q_ROWS) // 2, 0, rows_n - WIN_ROWS)) * GRID_W
            else:
                q0 = pl.multiple_of(g * nq, nq)
                w0 = jnp.clip(Q_ROWS * g - (WIN_ROWS - Q_ROWS) // 2, 0, rows_n - WIN_ROWS)
                k0 = pl.multiple_of(w0 * GRID_W, GRID_W)
            q128 = q_ref[pl.ds(q0, nq), :]
            q0s.append(q0)
            for hh in range(2):
                sl = slice(hh * LANES, (hh + 1) * LANES)
                qps.append(_place_head(q128, hh, hh, lane_hi))
                chunks = []
                for s in range(0, nw, tk):
                    k = k_ref[pl.ds(k0 + s, tk), :]
                    v = v_ref[pl.ds(k0 + s, tk), sl]
                    band = []
                    for b in range(nq // LANES):
                        lo, hi = key_band(g_pattern, b)
                        band.append((max(lo, s) - s, min(hi, s + tk) - s))
                    if all(lo < hi for lo, hi in band):
                        bias_fn = (lambda lo, hi, cs, s=s, hh=hh, var=var:
                                   bias_ref[var, hh, s + lo:s + hi, cs])
                        chunks.append((k, v, bias_fn, band))
                    else:
                        chunks.append((k, v, bias_ref[var, hh, s:s + tk, :]))
                chunk_lists.append(chunks + [(kc, vc_ref[:, sl], None)])
        outs = _attend_t_multi(qps, chunk_lists)
        for u in range(len(groups)):
            o_ref[pl.ds(q0s[u], nq), :] = jnp.concatenate(outs[2 * u:2 * u + 2], axis=0).T.astype(BF16)

    edges = [(0, 0, 0), (n_groups - 1, 2, n_groups - 1)]
    interior = n_groups - 2
    per_trip = max([d for d in range(1, MAX_GROUPS_PER_TRIP + 1) if interior % d == 0], default=0)
    if interior == per_trip:
        run(edges + [(1 + u, 1, 1) for u in range(per_trip)])
    else:
        run(edges)

        def body(t, carry):
            run([(1 + t * per_trip + u, 1, 1) for u in range(per_trip)])
            return carry

        lax.fori_loop(0, interior // per_trip, body, 0)


def _attn_b(qb, kb, vb1, kbc, vbc1, bias_t, *, batch, seq, ctx_len, tk):
    rows_n = seq // GRID_W
    nq = Q_ROWS * GRID_W
    nw = WIN_ROWS * GRID_W
    lat = lambda w: pl.BlockSpec((seq, w), lambda hp, b: (b, hp))
    ctx = lambda w: pl.BlockSpec((ctx_len, w), lambda hp, b: (b, hp))
    return pl.pallas_call(
        functools.partial(_attn_b_kernel, rows_n=rows_n, tk=tk),
        grid=(B_HEADS // 2, batch),
        in_specs=[lat(LANES), lat(LANES), lat(2 * LANES), ctx(LANES), ctx(2 * LANES),
                  pl.BlockSpec((3, 2, nw, nq), lambda hp, b: (0, hp, 0, 0))],
        out_specs=lat(LANES),
        out_shape=jax.ShapeDtypeStruct((batch * seq, 512), BF16),
        compiler_params=_cparams("arbitrary", "arbitrary"),
        name="attn_b",
    )(qb, kb, vb1, kbc, vbc1, bias_t)


def _neighbourhood_bias_t(rpb, rows_n):
    kc = NA_COLS
    cols = np.arange(GRID_W)
    c0 = np.clip(cols - kc // 2, 0, GRID_W - kc)
    col_ok = (cols[None, :] >= c0[:, None]) & (cols[None, :] < c0[:, None] + kc)
    heads, n_dr, n_dc = rpb.shape
    dc = cols[:, None] - cols[None, :] + NA_COLS - 1
    onehot = (np.arange(n_dc)[:, None, None] == dc[None]).astype(np.float32)
    pad = Q_ROWS
    e_max = n_dr - 1 + 2 * pad
    t = jnp.einsum("hdj,jxy->hxdy", rpb[:, ::-1] * LOG2E, jnp.asarray(onehot), precision=lax.Precision.HIGHEST)
    t = jnp.where(col_ok.T[None, :, None, :], t, NEG)
    rev = jnp.pad(t, ((0, 0), (0, 0), (pad, pad), (0, 0))).reshape(heads, GRID_W, (e_max + 1) * GRID_W)
    nq = Q_ROWS * GRID_W
    width = -(-((e_max + 1) * GRID_W + LANES) // LANES) * LANES
    rev = jnp.pad(rev, ((0, 0), (0, 0), (0, width - rev.shape[2])))
    n_groups = rows_n // Q_ROWS
    plans = []
    for g in (0, min(1, n_groups - 1), n_groups - 1):
        w0 = int(np.clip(Q_ROWS * g - (WIN_ROWS - Q_ROWS) // 2, 0, rows_n - WIN_ROWS))
        plan = []
        for i in range(WIN_ROWS):
            d0 = w0 + i - Q_ROWS * g + NA_ROWS - 1
            ok = [a for a in range(Q_ROWS)
                  if int(np.clip(Q_ROWS * g + a - NA_ROWS // 2, 0, rows_n - NA_ROWS)) <= w0 + i
                  < int(np.clip(Q_ROWS * g + a - NA_ROWS // 2, 0, rows_n - NA_ROWS)) + NA_ROWS]
            assert ok == list(range(ok[0], ok[-1] + 1)) if ok else True
            plan.append(((e_max - pad - d0) * GRID_W, (ok[0], ok[-1] + 1) if ok else (0, 0)))
        plans.append(plan)
    return pl.pallas_call(
        functools.partial(_bias_kernel, plans=plans),
        grid=(3, heads),
        in_specs=[pl.BlockSpec((None, GRID_W, width), lambda v, h: (h, 0, 0))],
        out_specs=pl.BlockSpec((None, None, WIN_ROWS * GRID_W, nq), lambda v, h: (v, h, 0, 0)),
        out_shape=jax.ShapeDtypeStruct((3, heads, WIN_ROWS * GRID_W, nq), F32),
        compiler_params=_cparams("arbitrary", "arbitrary"),
        name="nbr_bias",
    )(rev)


def _bias_kernel(rev_ref, o_ref, *, plans):
    nq = o_ref.shape[1]
    q_row = lax.broadcasted_iota(jnp.int32, (GRID_W, nq), 1) // GRID_W
    for v, plan in enumerate(plans):
        @pl.when(pl.program_id(0) == v)
        def _(plan=plan):
            for i, (off, (lo, hi)) in enumerate(plan):
                rows = slice(i * GRID_W, (i + 1) * GRID_W)
                if lo >= hi:
                    o_ref[rows, :] = jnp.full((GRID_W, nq), NEG, F32)
                    continue
                al = off // LANES * LANES
                strip = rev_ref[:, al:al + nq + LANES][:, off - al:off - al + nq]
                o_ref[rows, :] = jnp.where((q_row >= lo) & (q_row < hi), strip, NEG)


def _attn_ctx_kernel(qa_ref, ka_ref, va_ref, qb_ref, kb_ref, vb_ref, oa_ref, ob_ref, *, lc):
    lane_hi = lax.broadcasted_iota(jnp.int32, (lc, LANES), 1) >= 64
    ka = ka_ref[...]
    oa_ref[...] = _gqa_tile_t(
        qa_ref[...], lambda g: [(ka, va_ref[:, g * LANES:(g + 1) * LANES], None)], lc, lane_hi).astype(BF16)
    for hp in range(B_HEADS // 2):
        sl = slice(hp * LANES, (hp + 1) * LANES)
        k = kb_ref[:, sl]
        chunks = lambda hh: [(k, vb_ref[:, (2 * hp + hh) * LANES:(2 * hp + hh + 1) * LANES], None)]
        ob_ref[:, sl] = _mha_pair_t(qb_ref[:, sl], chunks, lane_hi).astype(BF16)


def _attn_ctx(qac, kac, vac1, qbc, kbc, vbc1, *, batch, ctx_len):
    t = lambda w: pl.BlockSpec((ctx_len, w), lambda b: (b, 0))
    o = jax.ShapeDtypeStruct((batch * ctx_len, 512), BF16)
    return pl.pallas_call(
        functools.partial(_attn_ctx_kernel, lc=ctx_len),
        grid=(batch,),
        in_specs=[t(512), t(128), t(256), t(512), t(512), t(1024)],
        out_specs=[t(512), t(512)],
        out_shape=[o, o],
        compiler_params=_cparams("arbitrary"),
        name="attn_ctx",
    )(qac, kac, vac1, qbc, kbc, vbc1)


def _dft_kernel(ac_ref, as_ref, bc_ref, bs_ref, uc_ref, us_ref, o_ref, ct_ref, st_ref, *, bm):
    i = pl.program_id(0)

    @pl.when(pl.program_id(1) == 0)
    def _():
        bc = bc_ref[...]
        bs = bs_ref[...]
        for r in range(bm // GRID_W):
            a_c = ac_ref[pl.ds(i * (bm // GRID_W) + r, 1), :]
            a_s = as_ref[pl.ds(i * (bm // GRID_W) + r, 1), :]
            ct_ref[r * GRID_W:(r + 1) * GRID_W, :] = (a_c * bc - a_s * bs).astype(BF16)
            st_ref[r * GRID_W:(r + 1) * GRID_W, :] = (a_s * bc + a_c * bs).astype(BF16)

    o_ref[...] = (_dot(ct_ref[...], uc_ref[...]) - _dot(st_ref[...], us_ref[...])).astype(BF16)


def _dft(tabs, uc, us):
    n, width = uc.shape
    bm = min(n, 1024)
    bn = min(width, 512)
    u_spec = pl.BlockSpec((n, bn), lambda i, j: (0, j))
    return pl.pallas_call(
        functools.partial(_dft_kernel, bm=bm),
        grid=(n // bm, width // bn),
        in_specs=[_resident_spec(t) for t in tabs] + [u_spec, u_spec],
        out_specs=pl.BlockSpec((bm, bn), lambda i, j: (i, j)),
        out_shape=jax.ShapeDtypeStruct((n, width), BF16),
        scratch_shapes=[pltpu.VMEM((bm, n), BF16), pltpu.VMEM((bm, n), BF16)],
        compiler_params=_cparams("arbitrary", "arbitrary"),
        name="dft",
    )(*tabs, uc, us)


def _dft_factor_tables(n):
    r = n // GRID_W
    k = np.arange(n, dtype=np.int64)
    ang_a = (2.0 * np.pi / r) * ((np.arange(r)[:, None] * k[None, :]) % r)
    ang_b = (2.0 * np.pi / n) * ((np.arange(GRID_W)[:, None] * k[None, :]) % n)
    pad = ((0, (-r) % 8), (0, 0))
    scale = n ** -0.5
    return tuple(jnp.asarray(t, dtype=F32) for t in (
        np.pad(np.cos(ang_a), pad), np.pad(np.sin(ang_a), pad), np.cos(ang_b) * scale, np.sin(ang_b) * scale))


def _merge_kernel(x_ref, mod_ref, oa_ref, ob_ref, oc_ref, wg_ref, wb_ref, wo_ref, g_ref, b_ref, o_ref, *, alpha):
    mod = mod_ref[...]
    sh = mod[:, 0:D_MODEL]
    sc = mod[:, D_MODEL:2 * D_MODEL]
    gate = mod[:, 2 * D_MODEL:3 * D_MODEL]
    blocks = _row_blocks(x_ref.shape[0])
    xs = [x_ref[b, :] for b in blocks]
    hs = [(_layernorm(x) * (1.0 + sc) + sh).astype(BF16) for x in xs]
    ys = [None] * len(blocks)
    for i, br_ref in enumerate((oa_ref, ob_ref, oc_ref)):
        for r, b in enumerate(blocks):
            gz = _dot(hs[r], wg_ref[:, N_QKVU + i * D_MODEL:N_QKVU + (i + 1) * D_MODEL])
            t = _dot(br_ref[b, :], wb_ref[i]) * (1.0 / (1.0 + jnp.exp(-gz)))
            ys[r] = t if ys[r] is None else ys[r] + t
    for r, b in enumerate(blocks):
        yo = _dot(ys[r].astype(BF16), wo_ref[...])
        o_ref[b, :] = _layernorm(alpha * xs[r] + gate * yo) * g_ref[...] + b_ref[...]


def _merge(x2, mod3, oa, ob, oc, w_gate, w_branch, w_out, ln_g, ln_b, *, batch, seq, tm, alpha):
    rows = batch * seq
    nl = seq // tm
    tok = lambda w: pl.BlockSpec((tm, w), lambda i: (i, 0))
    full = _resident_spec
    return pl.pallas_call(
        functools.partial(_merge_kernel, alpha=alpha),
        grid=(rows // tm,),
        in_specs=[
            tok(D_MODEL),
            pl.BlockSpec((None, 1, mod3.shape[2]), lambda i: (i // nl, 0, 0)),
            tok(512), tok(512),
            pl.BlockSpec((tm, 512), lambda i: (i % nl, i // nl)),
            full(w_gate), full(w_branch), full(w_out), full(ln_g), full(ln_b),
        ],
        out_specs=tok(D_MODEL),
        out_shape=jax.ShapeDtypeStruct((rows, D_MODEL), F32),
        compiler_params=_cparams("arbitrary"),
        name="merge",
    )(x2, mod3, oa, ob, oc, w_gate, w_branch, w_out, ln_g, ln_b)


FF_CHUNKS = ((0, 768), (768, 1536), (1536, 2304), (2304, 2816))


def _ffn_kernel(x_ref, mod_ref, wgu_ref, wd_ref, g_ref, b_ref, o_ref, *, alpha):
    mod = mod_ref[...]
    sh = mod[:, 3 * D_MODEL:4 * D_MODEL]
    sc = mod[:, 4 * D_MODEL:5 * D_MODEL]
    gate = mod[:, 5 * D_MODEL:6 * D_MODEL]
    blocks = _row_blocks(x_ref.shape[0])
    xs = [x_ref[b, :] for b in blocks]
    hs = [(_layernorm(x) * (1.0 + sc) + sh).astype(BF16) for x in xs]
    fs = [None] * len(blocks)
    for s, e in FF_CHUNKS:
        for i in range(len(blocks)):
            g = _dot(hs[i], wgu_ref[:, s:e])
            u = _dot(hs[i], wgu_ref[:, D_FF + s:D_FF + e])
            a = (g * (1.0 / (1.0 + jnp.exp(-g))) * u).astype(BF16)
            t = _dot(a, wd_ref[s:e, :])
            fs[i] = t if fs[i] is None else fs[i] + t
    for i, b in enumerate(blocks):
        o_ref[b, :] = _layernorm(alpha * xs[i] + gate * fs[i]) * g_ref[...] + b_ref[...]


def _ffn(x2, mod3, w_gu, w_down, ln_g, ln_b, *, batch, seq, tm, alpha):
    rows = batch * seq
    nl = seq // tm
    tok = pl.BlockSpec((tm, D_MODEL), lambda i: (i, 0))
    full = _resident_spec
    return pl.pallas_call(
        functools.partial(_ffn_kernel, alpha=alpha),
        grid=(rows // tm,),
        in_specs=[tok, pl.BlockSpec((None, 1, mod3.shape[2]), lambda i: (i // nl, 0, 0)),
                  full(w_gu), full(w_down), full(ln_g), full(ln_b)],
        out_specs=tok,
        out_shape=jax.ShapeDtypeStruct((rows, D_MODEL), F32),
        compiler_params=_cparams("arbitrary"),
        name="ffn",
    )(x2, mod3, w_gu, w_down, ln_g, ln_b)


def _rope_tables(seq):
    quarter = HEAD_DIM // 4
    pos = np.arange(seq)
    freqs = ROPE_THETA ** (-np.arange(quarter, dtype=np.float64) / quarter)
    ar = (pos // GRID_W)[:, None] * freqs
    ac = (pos % GRID_W)[:, None] * freqs
    cos64 = np.concatenate([np.cos(ar), np.cos(ar), np.cos(ac), np.cos(ac)], axis=1)
    sin64 = np.concatenate([-np.sin(ar), np.sin(ar), -np.sin(ac), np.sin(ac)], axis=1)
    return jnp.asarray(np.tile(cos64, (1, 2)), dtype=F32), jnp.asarray(np.tile(sin64, (1, 2)), dtype=F32)


def _head_mean_matrix(width):
    idx = np.arange(width) // HEAD_DIM
    return jnp.asarray((idx[:, None] == idx[None, :]).astype(np.float32) / HEAD_DIM).astype(BF16)


def _channel_dft_matrix():
    idx = np.arange(C_GROUP_W)
    ang = (2.0 * np.pi / C_GROUP_W) * ((idx[:, None] * idx[None, :]) % C_GROUP_W)
    scale = C_GROUP_W ** -0.5
    mat = np.concatenate([np.cos(ang), np.sin(ang)], axis=1) * scale
    return jnp.asarray(mat, dtype=F32).astype(BF16)


def kernel(x, c, ctx, c_ctx, w_ada, b_ada, w_in, q_norm, k_norm, rpb, w_branch, w_out,
           ln1_g, ln1_b, w_gu, w_down, ln2_g, ln2_b):
    batch, seq, d = x.shape
    ctx_len = ctx.shape[1]
    depth = w_ada.shape[0]
    n_groups = seq // (Q_ROWS * GRID_W)
    assert d == D_MODEL and seq % (Q_ROWS * GRID_W) == 0 and seq // GRID_W >= WIN_ROWS
    assert n_groups >= 2
    assert ctx_len % LANES == 0
    alpha = (2.0 * depth) ** 0.25
    tm = 1024
    tmc = min(ctx_len, 1024)
    assert seq % tm == 0 and tm % ROW_BLOCK == 0 and tmc % ROW_BLOCK == 0

    pad = (-(batch + 1)) % 8
    cc = jnp.concatenate([c, c_ctx[None, :], jnp.zeros((pad, d), F32)], axis=0)
    mods = _ada(cc, w_ada, b_ada)

    cos_t, sin_t = _rope_tables(seq)
    cos_c = jnp.ones((ctx_len, LANES), F32)
    sin_c = jnp.zeros((ctx_len, LANES), F32)
    g512 = _head_mean_matrix(512)
    g128 = _head_mean_matrix(128)
    dft_c = _channel_dft_matrix()
    tabs_l = _dft_factor_tables(seq)
    tabs_c = _dft_factor_tables(ctx_len)

    xl = x.reshape(batch * seq, d)
    xc = ctx.reshape(batch * ctx_len, d)
    for l in range(depth):
        with_ctx = l < depth - 1
        w_qkvu = w_gate = w_in[l].astype(BF16)
        wb = w_branch[l].astype(BF16)
        wo = w_out[l].astype(BF16)
        wgu = w_gu[l].astype(BF16)
        wd = w_down[l].astype(BF16)
        q_gain = jnp.tile(q_norm[l] * Q_SCALE, A_HEADS)[None, :]
        k_gain = jnp.tile(k_norm[l], A_KV_HEADS)[None, :]
        mod_l = mods[l, :batch][:, None, :]
        mod_c = jnp.broadcast_to(mods[l, batch][None, None, :], (batch, 1, 6 * d))
        ln1 = (ln1_g[l][None, :], ln1_b[l][None, :])
        ln2 = (ln2_g[l][None, :], ln2_b[l][None, :])

        qa, ka, va1, qb, kb, vb1, uc, us = _inproj(
            xl, mod_l, w_qkvu, q_gain, k_gain, cos_t, sin_t, g512, g128, dft_c, batch=batch, seq=seq, tm=tm)
        qac, kac, vac1, qbc, kbc, vbc1, ucc, usc = _inproj(
            xc, mod_c, w_qkvu, q_gain, k_gain, cos_c, sin_c, g512, g128, dft_c, batch=batch, seq=ctx_len, tm=tmc)

        score_bound = (1.01 * HEAD_DIM * jnp.max(jnp.abs(q_gain)) * jnp.max(jnp.abs(k_gain))).reshape(1)
        oa = _attn_a(score_bound, qa, ka, va1, kac, vac1, batch=batch, seq=seq, ctx_len=ctx_len, tq=256, tk=256)
        bias_t = _neighbourhood_bias_t(rpb[l], seq // GRID_W)
        ob = _attn_b(qb, kb, vb1, kbc, vbc1, bias_t, batch=batch, seq=seq, ctx_len=ctx_len, tk=512)
        oc = _dft(tabs_l, uc, us)

        x1 = _merge(xl, mod_l, oa, ob, oc, w_gate, wb, wo, *ln1, batch=batch, seq=seq, tm=tm, alpha=alpha)
        xl = _ffn(x1, mod_l, wgu, wd, *ln2, batch=batch, seq=seq, tm=tm, alpha=alpha)

        if with_ctx:
            oac, obc = _attn_ctx(qac, kac, vac1, qbc, kbc, vbc1, batch=batch, ctx_len=ctx_len)
            occ = _dft(tabs_c, ucc, usc)
            xc1 = _merge(xc, mod_c, oac, obc, occ, w_gate, wb, wo, *ln1,
                         batch=batch, seq=ctx_len, tm=tmc, alpha=alpha)
            xc = _ffn(xc1, mod_c, wgu, wd, *ln2, batch=batch, seq=ctx_len, tm=tmc, alpha=alpha)
    return xl.reshape(batch, seq, d)
```

```python
import functools
import math

import numpy as np
import jax
import jax.numpy as jnp
from jax import lax
from jax.experimental import pallas as pl
from jax.experimental.pallas import tpu as pltpu

F32 = jnp.float32
BF16 = jnp.bfloat16

D_MODEL = 1024
GRID_W = 64
HEAD_DIM = 64
A_HEADS = 8
A_KV_HEADS = 2
A_GROUP = A_HEADS // A_KV_HEADS
B_HEADS = 8
NA_ROWS = 8
NA_COLS = 16
C_GROUPS = 4
C_GROUP_W = 128
D_FF = 2816
ROPE_THETA = 10000.0
LN_EPS = 1e-6
RMS_EPS = 1e-6
N_QKVU = 2816
Q_ROWS = 8
WIN_ROWS = 16
MAX_GROUPS_PER_TRIP = 6
ROW_BLOCK = 256
MAX_SHIFT_BOUND = 40.0
NEG = -1e30
LOG2E = math.log2(math.e)
Q_SCALE = HEAD_DIM ** -0.5 * LOG2E
LANES = 128
VMEM_LIMIT = 56 * 1024 * 1024


def _cparams(*sem):
    return pltpu.CompilerParams(dimension_semantics=sem, vmem_limit_bytes=VMEM_LIMIT)


def _resident_spec(a):
    return pl.BlockSpec(a.shape, lambda *_: (0,) * a.ndim, pipeline_mode=pl.Buffered(1))


def _dot(a, b):
    return jnp.dot(a, b, preferred_element_type=F32)


def _dot_nt(a, b):
    return lax.dot_general(a, b, (((1,), (1,)), ((), ())), preferred_element_type=F32)


def _dot_tn(a, b):
    return lax.dot_general(a, b, (((0,), (0,)), ((), ())), preferred_element_type=F32)


def _row_blocks(rows):
    return [slice(r, r + ROW_BLOCK) for r in range(0, rows, ROW_BLOCK)]


def _layernorm(x):
    mu = jnp.mean(x, axis=-1, keepdims=True)
    xc = x - mu
    var = jnp.mean(xc * xc, axis=-1, keepdims=True)
    return xc * lax.rsqrt(var + LN_EPS)


def _ada_kernel(c_ref, w_ref, b_ref, o_ref):
    c = c_ref[...]
    a = c * (1.0 / (1.0 + jnp.exp(-c)))
    a_hi = a.astype(BF16)
    a_lo = (a - a_hi.astype(F32)).astype(BF16)
    w = w_ref[...]
    w_hi = w.astype(BF16)
    w_lo = (w - w_hi.astype(F32)).astype(BF16)
    o_ref[...] = _dot(a_hi, w_hi) + _dot(a_hi, w_lo) + _dot(a_lo, w_hi) + b_ref[...]


def _ada(cc, w_ada, b_ada):
    depth, d, n = w_ada.shape
    tn = 1536
    return pl.pallas_call(
        _ada_kernel,
        grid=(depth, n // tn),
        in_specs=[
            pl.BlockSpec((cc.shape[0], d), lambda l, j: (0, 0)),
            pl.BlockSpec((None, d, tn), lambda l, j: (l, 0, j)),
            pl.BlockSpec((None, 1, tn), lambda l, j: (l, 0, j)),
        ],
        out_specs=pl.BlockSpec((None, cc.shape[0], tn), lambda l, j: (l, 0, j)),
        out_shape=jax.ShapeDtypeStruct((depth, cc.shape[0], n), F32),
        compiler_params=_cparams("arbitrary", "arbitrary"),
        name="ada",
    )(cc, w_ada, b_ada.reshape(depth, 1, n))


def _swap16(x, lane_lo):
    up = pltpu.roll(x, 16, 1)
    dn = pltpu.roll(x, LANES - 16, 1)
    return jnp.where(lane_lo, dn, up)


def _with_ones(v):
    ones = jnp.ones((v.shape[0], HEAD_DIM), F32)
    parts = []
    for hd in range(v.shape[1] // HEAD_DIM):
        parts += [v[:, hd * HEAD_DIM:(hd + 1) * HEAD_DIM], ones]
    return jnp.concatenate(parts, axis=1).astype(BF16)


def _inproj_kernel(x_ref, mod_ref, w_ref, qg_ref, kg_ref, cos_ref, sin_ref, g512_ref, g128_ref, dft_ref,
                   qa_ref, ka_ref, va_ref, qb_ref, kb_ref, vb_ref, uc_ref, us_ref):
    mod = mod_ref[...]
    sh = mod[:, 0:D_MODEL]
    sc = mod[:, D_MODEL:2 * D_MODEL]
    h = (_layernorm(x_ref[...]) * (1.0 + sc) + sh).astype(BF16)

    cos = cos_ref[...]
    sin = sin_ref[...]
    lane_lo = (lax.broadcasted_iota(jnp.int32, cos.shape, 1) % 32) < 16

    def norm_rope(v, ms, gain):
        vn = v * lax.rsqrt(ms + RMS_EPS) * gain
        outs = []
        for j in range(v.shape[1] // LANES):
            t = vn[:, j * LANES:(j + 1) * LANES]
            outs.append(t * cos + _swap16(t, lane_lo) * sin)
        return outs[0] if len(outs) == 1 else jnp.concatenate(outs, axis=1)

    q = _dot(h, w_ref[:, 0:512])
    kv = _dot(h, w_ref[:, 512:768])
    qb_ref[...] = (_dot(h, w_ref[:, 768:1280]) * Q_SCALE).astype(BF16)
    kb_ref[...] = _dot(h, w_ref[:, 1280:1792]).astype(BF16)
    k = kv[:, 0:128]
    ms_q = _dot((q * q).astype(BF16), g512_ref[...])
    ms_k = _dot((k * k).astype(BF16), g128_ref[...])
    vb_ref[...] = _with_ones(_dot(h, w_ref[:, 1792:2304]))
    u = _dot(h, w_ref[:, 2304:2816]).astype(BF16)
    qa_ref[...] = norm_rope(q, ms_q, qg_ref[...]).astype(BF16)
    ka_ref[...] = norm_rope(k, ms_k, kg_ref[...]).astype(BF16)
    va_ref[...] = _with_ones(kv[:, 128:256])
    dft = dft_ref[...]
    for g in range(C_GROUPS):
        z = _dot(u[:, g * C_GROUP_W:(g + 1) * C_GROUP_W], dft)
        uc_ref[:, g * C_GROUP_W:(g + 1) * C_GROUP_W] = z[:, 0:C_GROUP_W].astype(BF16)
        us_ref[:, g * C_GROUP_W:(g + 1) * C_GROUP_W] = z[:, C_GROUP_W:2 * C_GROUP_W].astype(BF16)


def _inproj(x2, mod3, w_qkvu, q_gain, k_gain, cos_t, sin_t, g512, g128, dft_c, *, batch, seq, tm):
    rows = batch * seq
    nl = seq // tm
    tok = lambda w: pl.BlockSpec((tm, w), lambda i: (i, 0))
    full = _resident_spec
    fo = pl.BlockSpec((tm, 512), lambda i: (i % nl, i // nl))
    o_tok = lambda w: jax.ShapeDtypeStruct((rows, w), BF16)
    return pl.pallas_call(
        _inproj_kernel,
        grid=(rows // tm,),
        in_specs=[
            tok(D_MODEL),
            pl.BlockSpec((None, 1, mod3.shape[2]), lambda i: (i // nl, 0, 0)),
            full(w_qkvu), full(q_gain), full(k_gain),
            pl.BlockSpec((tm, LANES), lambda i: (i % nl, 0)),
            pl.BlockSpec((tm, LANES), lambda i: (i % nl, 0)),
            full(g512), full(g128), full(dft_c),
        ],
        out_specs=[tok(512), tok(128), tok(256), tok(512), tok(512), tok(1024), fo, fo],
        out_shape=[o_tok(512), o_tok(128), o_tok(256), o_tok(512), o_tok(512), o_tok(1024),
                   jax.ShapeDtypeStruct((seq, batch * 512), BF16),
                   jax.ShapeDtypeStruct((seq, batch * 512), BF16)],
        compiler_params=_cparams("arbitrary"),
        name="inproj",
    )(x2, mod3, w_qkvu, q_gain, k_gain, cos_t, sin_t, g512, g128, dft_c)


def _attend_t_multi(qps, chunk_lists, shift=None):
    n = len(qps)
    m = [None] * n
    acc = [None] * n
    units = [(i, c) for c in range(len(chunk_lists[0])) for i in range(n)]

    def scores(u):
        i, c = u
        chunk = chunk_lists[i][c]
        s = _dot_nt(chunk[0], qps[i])
        return s if chunk[2] is None or len(chunk) == 4 else s + chunk[2]

    def banded(s, bias_fn, band):
        cols = [slice(b * LANES, (b + 1) * LANES) for b in range(len(band))]
        subs = [s[lo:hi, cs] + bias_fn(lo, hi, cs) for (lo, hi), cs in zip(band, cols)]
        mc = jnp.concatenate([jnp.max(x, axis=0, keepdims=True) for x in subs], axis=1)

        def probs(m_ref):
            out = []
            for x, (lo, hi), cs in zip(subs, band, cols):
                parts = [jnp.exp2(x - m_ref[:, cs]).astype(BF16)]
                if lo:
                    parts.insert(0, jnp.zeros((lo, LANES), BF16))
                if s.shape[0] - hi:
                    parts.append(jnp.zeros((s.shape[0] - hi, LANES), BF16))
                out.append(parts[0] if len(parts) == 1 else jnp.concatenate(parts, axis=0))
            return jnp.concatenate(out, axis=1)

        return mc, probs

    s_next = scores(units[0])
    for idx, (i, c) in enumerate(units):
        s = s_next
        if idx + 1 < len(units):
            s_next = scores(units[idx + 1])
        chunk = chunk_lists[i][c]
        v1 = chunk[1]
        if shift is not None:
            d = _dot_tn(v1, jnp.exp2(s - shift).astype(BF16))
            acc[i] = d if acc[i] is None else acc[i] + d
            continue
        if len(chunk) == 4:
            mc, probs = banded(s, chunk[2], chunk[3])
        else:
            mc, probs = jnp.max(s, axis=0, keepdims=True), lambda m_ref, s=s: jnp.exp2(s - m_ref).astype(BF16)
        if m[i] is None:
            m[i] = mc
            acc[i] = _dot_tn(v1, probs(mc))
        else:
            m_new = jnp.maximum(m[i], mc)
            acc[i] = jnp.exp2(m[i] - m_new) * acc[i] + _dot_tn(v1, probs(m_new))
            m[i] = m_new
    return [a[0:HEAD_DIM, :] / a[HEAD_DIM:HEAD_DIM + 1, :] for a in acc]


def _place_head(q128, src_half, dst_half, lane_hi):
    x = q128.astype(F32)
    if src_half != dst_half:
        x = pltpu.roll(x, 64, 1)
    keep = lane_hi if dst_half == 1 else jnp.logical_not(lane_hi)
    return jnp.where(keep, x, 0.0).astype(BF16)


def _gqa_tile_t(q, chunks_fn, tq, lane_hi, shift=None):
    qps = []
    for g in range(A_KV_HEADS):
        parts = []
        for j in range(A_GROUP):
            hd = A_GROUP * g + j
            parts.append(_place_head(q[:, LANES * (hd // 2):LANES * (hd // 2 + 1)], hd % 2, g, lane_hi))
        qps.append(jnp.concatenate(parts, axis=0))
    outs = _attend_t_multi(qps, [chunks_fn(g) for g in range(A_KV_HEADS)], shift)
    heads_t = [o_t[:, j * tq:(j + 1) * tq] for o_t in outs for j in range(A_GROUP)]
    return jnp.concatenate(heads_t, axis=0).T


def _mha_pair_t(q128, chunks_fn, lane_hi):
    outs = _attend_t_multi([_place_head(q128, hh, hh, lane_hi) for hh in range(2)],
                           [chunks_fn(hh) for hh in range(2)])
    return jnp.concatenate(outs, axis=0).T


def _attn_a_kernel(bound_ref, q_ref, k_ref, v_ref, kc_ref, vc_ref, o_ref, *, tq, chunk_bounds):
    lane_hi = lax.broadcasted_iota(jnp.int32, (tq, LANES), 1) >= 64

    def chunks(g):
        sl = slice(g * LANES, (g + 1) * LANES)
        return ([(kc_ref[...], vc_ref[:, sl], None)]
                + [(k_ref[s:e, :], v_ref[s:e, sl], None) for s, e in chunk_bounds])

    bound = bound_ref[0]
    o_ref[...] = lax.cond(
        bound <= MAX_SHIFT_BOUND,
        lambda: _gqa_tile_t(q_ref[...], chunks, tq, lane_hi, shift=bound),
        lambda: _gqa_tile_t(q_ref[...], chunks, tq, lane_hi),
    ).astype(BF16)


def _attn_a(score_bound, qa, ka, va1, kac, vac1, *, batch, seq, ctx_len, tq, tk):
    bounds = [(i, min(i + tk, seq)) for i in range(0, seq, tk)]
    nq = seq // tq
    return pl.pallas_call(
        functools.partial(_attn_a_kernel, tq=tq, chunk_bounds=bounds),
        grid=(batch, nq),
        in_specs=[pl.BlockSpec(memory_space=pltpu.SMEM),
                  pl.BlockSpec((tq, 512), lambda b, i: (b * nq + i, 0)),
                  pl.BlockSpec((seq, LANES), lambda b, i: (b, 0)),
                  pl.BlockSpec((seq, 2 * LANES), lambda b, i: (b, 0)),
                  pl.BlockSpec((ctx_len, LANES), lambda b, i: (b, 0)),
                  pl.BlockSpec((ctx_len, 2 * LANES), lambda b, i: (b, 0))],
        out_specs=pl.BlockSpec((tq, 512), lambda b, i: (b * nq + i, 0)),
        out_shape=jax.ShapeDtypeStruct((batch * seq, 512), BF16),
        compiler_params=_cparams("arbitrary", "arbitrary"),
        name="attn_a",
    )(score_bound, qa, ka, va1, kac, vac1)


def _attn_b_kernel(q_ref, k_ref, v_ref, kc_ref, vc_ref, bias_ref, o_ref, *, rows_n, tk):
    nq = Q_ROWS * GRID_W
    nw = WIN_ROWS * GRID_W
    n_groups = rows_n // Q_ROWS
    lane_hi = lax.broadcasted_iota(jnp.int32, (nq, LANES), 1) >= 64
    kc = kc_ref[...]

    def key_band(g_pattern, b):
        w0 = int(np.clip(Q_ROWS * g_pattern - (WIN_ROWS - Q_ROWS) // 2, 0, rows_n - WIN_ROWS))
        r0s = [int(np.clip(Q_ROWS * g_pattern + a - NA_ROWS // 2, 0, rows_n - NA_ROWS)) - w0
               for a in range(b * LANES // GRID_W, (b + 1) * LANES // GRID_W)]
        return min(r0s) * GRID_W, (max(r0s) + NA_ROWS) * GRID_W

    def run(groups):
        qps, chunk_lists, q0s = [], [], []
        for g, var, g_pattern in groups:
            if isinstance(g, int):
                q0 = g * nq
                k0 = int(np.clip(Q_ROWS * g - (WIN_ROWS - Q_ROWS) // 2, 0, rows_n - WIN_ROWS)) * GRID_W
            else:
                q0 = pl.multiple_of(g * nq, nq)
                w0 = jnp.clip(Q_ROWS * g - (WIN_ROWS - Q_ROWS) // 2, 0, rows_n - WIN_ROWS)
                k0 = pl.multiple_of(w0 * GRID_W, GRID_W)
            q128 = q_ref[pl.ds(q0, nq), :]
            q0s.append(q0)
            for hh in range(2):
                sl = slice(hh * LANES, (hh + 1) * LANES)
                qps.append(_place_head(q128, hh, hh, lane_hi))
                chunks = []
                for s in range(0, nw, tk):
                    k = k_ref[pl.ds(k0 + s, tk), :]
                    v = v_ref[pl.ds(k0 + s, tk), sl]
                    band = []
                    for b in range(nq // LANES):
                        lo, hi = key_band(g_pattern, b)
                        band.append((max(lo, s) - s, min(hi, s + tk) - s))
                    if all(lo < hi for lo, hi in band):
                        bias_fn = (lambda lo, hi, cs, s=s, hh=hh, var=var:
                                   bias_ref[var, hh, s + lo:s + hi, cs])
                        chunks.append((k, v, bias_fn, band))
                    else:
                        chunks.append((k, v, bias_ref[var, hh, s:s + tk, :]))
                chunk_lists.append(chunks + [(kc, vc_ref[:, sl], None)])
        outs = _attend_t_multi(qps, chunk_lists)
        for u in range(len(groups)):
            o_ref[pl.ds(q0s[u], nq), :] = jnp.concatenate(outs[2 * u:2 * u + 2], axis=0).T.astype(BF16)

    edges = [(0, 0, 0), (n_groups - 1, 2, n_groups - 1)]
    interior = n_groups - 2
    per_trip = max([d for d in range(1, MAX_GROUPS_PER_TRIP + 1) if interior % d == 0], default=0)
    if interior == per_trip:
        run(edges + [(1 + u, 1, 1) for u in range(per_trip)])
    else:
        run(edges)

        def body(t, carry):
            run([(1 + t * per_trip + u, 1, 1) for u in range(per_trip)])
            return carry

        lax.fori_loop(0, interior // per_trip, body, 0)


def _attn_b(qb, kb, vb1, kbc, vbc1, bias_t, *, batch, seq, ctx_len, tk):
    rows_n = seq // GRID_W
    nq = Q_ROWS * GRID_W
    nw = WIN_ROWS * GRID_W
    lat = lambda w: pl.BlockSpec((seq, w), lambda hp, b: (b, hp))
    ctx = lambda w: pl.BlockSpec((ctx_len, w), lambda hp, b: (b, hp))
    return pl.pallas_call(
        functools.partial(_attn_b_kernel, rows_n=rows_n, tk=tk),
        grid=(B_HEADS // 2, batch),
        in_specs=[lat(LANES), lat(LANES), lat(2 * LANES), ctx(LANES), ctx(2 * LANES),
                  pl.BlockSpec((3, 2, nw, nq), lambda hp, b: (0, hp, 0, 0))],
        out_specs=lat(LANES),
        out_shape=jax.ShapeDtypeStruct((batch * seq, 512), BF16),
        compiler_params=_cparams("arbitrary", "arbitrary"),
        name="attn_b",
    )(qb, kb, vb1, kbc, vbc1, bias_t)


def _neighbourhood_bias_t(rpb, rows_n):
    kc = NA_COLS
    cols = np.arange(GRID_W)
    c0 = np.clip(cols - kc // 2, 0, GRID_W - kc)
    col_ok = (cols[None, :] >= c0[:, None]) & (cols[None, :] < c0[:, None] + kc)
    heads, n_dr, n_dc = rpb.shape
    dc = cols[:, None] - cols[None, :] + NA_COLS - 1
    onehot = (np.arange(n_dc)[:, None, None] == dc[None]).astype(np.float32)
    pad = Q_ROWS
    e_max = n_dr - 1 + 2 * pad
    t = jnp.einsum("hdj,jxy->hxdy", rpb[:, ::-1] * LOG2E, jnp.asarray(onehot), precision=lax.Precision.HIGHEST)
    t = jnp.where(col_ok.T[None, :, None, :], t, NEG)
    rev = jnp.pad(t, ((0, 0), (0, 0), (pad, pad), (0, 0))).reshape(heads, GRID_W, (e_max + 1) * GRID_W)
    nq = Q_ROWS * GRID_W
    width = -(-((e_max + 1) * GRID_W + LANES) // LANES) * LANES
    rev = jnp.pad(rev, ((0, 0), (0, 0), (0, width - rev.shape[2])))
    n_groups = rows_n // Q_ROWS
    plans = []
    for g in (0, min(1, n_groups - 1), n_groups - 1):
        w0 = int(np.clip(Q_ROWS * g - (WIN_ROWS - Q_ROWS) // 2, 0, rows_n - WIN_ROWS))
        plan = []
        for i in range(WIN_ROWS):
            d0 = w0 + i - Q_ROWS * g + NA_ROWS - 1
            ok = [a for a in range(Q_ROWS)
                  if int(np.clip(Q_ROWS * g + a - NA_ROWS // 2, 0, rows_n - NA_ROWS)) <= w0 + i
                  < int(np.clip(Q_ROWS * g + a - NA_ROWS // 2, 0, rows_n - NA_ROWS)) + NA_ROWS]
            assert ok == list(range(ok[0], ok[-1] + 1)) if ok else True
            plan.append(((e_max - pad - d0) * GRID_W, (ok[0], ok[-1] + 1) if ok else (0, 0)))
        plans.append(plan)
    return pl.pallas_call(
        functools.partial(_bias_kernel, plans=plans),
        grid=(3, heads),
        in_specs=[pl.BlockSpec((None, GRID_W, width), lambda v, h: (h, 0, 0))],
        out_specs=pl.BlockSpec((None, None, WIN_ROWS * GRID_W, nq), lambda v, h: (v, h, 0, 0)),
        out_shape=jax.ShapeDtypeStruct((3, heads, WIN_ROWS * GRID_W, nq), F32),
        compiler_params=_cparams("arbitrary", "arbitrary"),
        name="nbr_bias",
    )(rev)


def _bias_kernel(rev_ref, o_ref, *, plans):
    nq = o_ref.shape[1]
    q_row = lax.broadcasted_iota(jnp.int32, (GRID_W, nq), 1) // GRID_W
    for v, plan in enumerate(plans):
        @pl.when(pl.program_id(0) == v)
        def _(plan=plan):
            for i, (off, (lo, hi)) in enumerate(plan):
                rows = slice(i * GRID_W, (i + 1) * GRID_W)
                if lo >= hi:
                    o_ref[rows, :] = jnp.full((GRID_W, nq), NEG, F32)
                    continue
                al = off // LANES * LANES
                strip = rev_ref[:, al:al + nq + LANES][:, off - al:off - al + nq]
                o_ref[rows, :] = jnp.where((q_row >= lo) & (q_row < hi), strip, NEG)


def _attn_ctx_kernel(qa_ref, ka_ref, va_ref, qb_ref, kb_ref, vb_ref, oa_ref, ob_ref, *, lc):
    lane_hi = lax.broadcasted_iota(jnp.int32, (lc, LANES), 1) >= 64
    ka = ka_ref[...]
    oa_ref[...] = _gqa_tile_t(
        qa_ref[...], lambda g: [(ka, va_ref[:, g * LANES:(g + 1) * LANES], None)], lc, lane_hi).astype(BF16)
    for hp in range(B_HEADS // 2):
        sl = slice(hp * LANES, (hp + 1) * LANES)
        k = kb_ref[:, sl]
        chunks = lambda hh: [(k, vb_ref[:, (2 * hp + hh) * LANES:(2 * hp + hh + 1) * LANES], None)]
        ob_ref[:, sl] = _mha_pair_t(qb_ref[:, sl], chunks, lane_hi).astype(BF16)


def _attn_ctx(qac, kac, vac1, qbc, kbc, vbc1, *, batch, ctx_len):
    t = lambda w: pl.BlockSpec((ctx_len, w), lambda b: (b, 0))
    o = jax.ShapeDtypeStruct((batch * ctx_len, 512), BF16)
    return pl.pallas_call(
        functools.partial(_attn_ctx_kernel, lc=ctx_len),
        grid=(batch,),
        in_specs=[t(512), t(128), t(256), t(512), t(512), t(1024)],
        out_specs=[t(512), t(512)],
        out_shape=[o, o],
        compiler_params=_cparams("arbitrary"),
        name="attn_ctx",
    )(qac, kac, vac1, qbc, kbc, vbc1)


def _dft_kernel(ac_ref, as_ref, bc_ref, bs_ref, uc_ref, us_ref, o_ref, ct_ref, st_ref, *, bm):
    i = pl.program_id(0)

    @pl.when(pl.program_id(1) == 0)
    def _():
        bc = bc_ref[...]
        bs = bs_ref[...]
        for r in range(bm // GRID_W):
            a_c = ac_ref[pl.ds(i * (bm // GRID_W) + r, 1), :]
            a_s = as_ref[pl.ds(i * (bm // GRID_W) + r, 1), :]
            ct_ref[r * GRID_W:(r + 1) * GRID_W, :] = (a_c * bc - a_s * bs).astype(BF16)
            st_ref[r * GRID_W:(r + 1) * GRID_W, :] = (a_s * bc + a_c * bs).astype(BF16)

    o_ref[...] = (_dot(ct_ref[...], uc_ref[...]) - _dot(st_ref[...], us_ref[...])).astype(BF16)


def _dft(tabs, uc, us):
    n, width = uc.shape
    bm = min(n, 1024)
    bn = min(width, 512)
    u_spec = pl.BlockSpec((n, bn), lambda i, j: (0, j))
    return pl.pallas_call(
        functools.partial(_dft_kernel, bm=bm),
        grid=(n // bm, width // bn),
        in_specs=[_resident_spec(t) for t in tabs] + [u_spec, u_spec],
        out_specs=pl.BlockSpec((bm, bn), lambda i, j: (i, j)),
        out_shape=jax.ShapeDtypeStruct((n, width), BF16),
        scratch_shapes=[pltpu.VMEM((bm, n), BF16), pltpu.VMEM((bm, n), BF16)],
        compiler_params=_cparams("arbitrary", "arbitrary"),
        name="dft",
    )(*tabs, uc, us)


def _dft_factor_tables(n):
    r = n // GRID_W
    k = np.arange(n, dtype=np.int64)
    ang_a = (2.0 * np.pi / r) * ((np.arange(r)[:, None] * k[None, :]) % r)
    ang_b = (2.0 * np.pi / n) * ((np.arange(GRID_W)[:, None] * k[None, :]) % n)
    pad = ((0, (-r) % 8), (0, 0))
    scale = n ** -0.5
    return tuple(jnp.asarray(t, dtype=F32) for t in (
        np.pad(np.cos(ang_a), pad), np.pad(np.sin(ang_a), pad), np.cos(ang_b) * scale, np.sin(ang_b) * scale))


def _merge_kernel(x_ref, mod_ref, oa_ref, ob_ref, oc_ref, wg_ref, wb_ref, wo_ref, g_ref, b_ref, o_ref, *, alpha):
    mod = mod_ref[...]
    sh = mod[:, 0:D_MODEL]
    sc = mod[:, D_MODEL:2 * D_MODEL]
    gate = mod[:, 2 * D_MODEL:3 * D_MODEL]
    blocks = _row_blocks(x_ref.shape[0])
    xs = [x_ref[b, :] for b in blocks]
    hs = [(_layernorm(x) * (1.0 + sc) + sh).astype(BF16) for x in xs]
    ys = [None] * len(blocks)
    for i, br_ref in enumerate((oa_ref, ob_ref, oc_ref)):
        for r, b in enumerate(blocks):
            gz = _dot(hs[r], wg_ref[:, i * D_MODEL:(i + 1) * D_MODEL])
            t = _dot(br_ref[b, :], wb_ref[i]) * (1.0 / (1.0 + jnp.exp(-gz)))
            ys[r] = t if ys[r] is None else ys[r] + t
    for r, b in enumerate(blocks):
        yo = _dot(ys[r].astype(BF16), wo_ref[...])
        o_ref[b, :] = _layernorm(alpha * xs[r] + gate * yo) * g_ref[...] + b_ref[...]


def _merge(x2, mod3, oa, ob, oc, w_gate, w_branch, w_out, ln_g, ln_b, *, batch, seq, tm, alpha):
    rows = batch * seq
    nl = seq // tm
    tok = lambda w: pl.BlockSpec((tm, w), lambda i: (i, 0))
    full = _resident_spec
    return pl.pallas_call(
        functools.partial(_merge_kernel, alpha=alpha),
        grid=(rows // tm,),
        in_specs=[
            tok(D_MODEL),
            pl.BlockSpec((None, 1, mod3.shape[2]), lambda i: (i // nl, 0, 0)),
            tok(512), tok(512),
            pl.BlockSpec((tm, 512), lambda i: (i % nl, i // nl)),
            full(w_gate), full(w_branch), full(w_out), full(ln_g), full(ln_b),
        ],
        out_specs=tok(D_MODEL),
        out_shape=jax.ShapeDtypeStruct((rows, D_MODEL), F32),
        compiler_params=_cparams("arbitrary"),
        name="merge",
    )(x2, mod3, oa, ob, oc, w_gate, w_branch, w_out, ln_g, ln_b)


FF_CHUNKS = ((0, 768), (768, 1536), (1536, 2304), (2304, 2816))


def _ffn_kernel(x_ref, mod_ref, wgu_ref, wd_ref, g_ref, b_ref, o_ref, *, alpha):
    mod = mod_ref[...]
    sh = mod[:, 3 * D_MODEL:4 * D_MODEL]
    sc = mod[:, 4 * D_MODEL:5 * D_MODEL]
    gate = mod[:, 5 * D_MODEL:6 * D_MODEL]
    blocks = _row_blocks(x_ref.shape[0])
    xs = [x_ref[b, :] for b in blocks]
    hs = [(_layernorm(x) * (1.0 + sc) + sh).astype(BF16) for x in xs]
    fs = [None] * len(blocks)
    for s, e in FF_CHUNKS:
        for i in range(len(blocks)):
            g = _dot(hs[i], wgu_ref[:, s:e])
            u = _dot(hs[i], wgu_ref[:, D_FF + s:D_FF + e])
            a = (g * (1.0 / (1.0 + jnp.exp(-g))) * u).astype(BF16)
            t = _dot(a, wd_ref[s:e, :])
            fs[i] = t if fs[i] is None else fs[i] + t
    for i, b in enumerate(blocks):
        o_ref[b, :] = _layernorm(alpha * xs[i] + gate * fs[i]) * g_ref[...] + b_ref[...]


def _ffn(x2, mod3, w_gu, w_down, ln_g, ln_b, *, batch, seq, tm, alpha):
    rows = batch * seq
    nl = seq // tm
    tok = pl.BlockSpec((tm, D_MODEL), lambda i: (i, 0))
    full = _resident_spec
    return pl.pallas_call(
        functools.partial(_ffn_kernel, alpha=alpha),
        grid=(rows // tm,),
        in_specs=[tok, pl.BlockSpec((None, 1, mod3.shape[2]), lambda i: (i // nl, 0, 0)),
                  full(w_gu), full(w_down), full(ln_g), full(ln_b)],
        out_specs=tok,
        out_shape=jax.ShapeDtypeStruct((rows, D_MODEL), F32),
        compiler_params=_cparams("arbitrary"),
        name="ffn",
    )(x2, mod3, w_gu, w_down, ln_g, ln_b)


def _rope_tables(seq):
    quarter = HEAD_DIM // 4
    pos = np.arange(seq)
    freqs = ROPE_THETA ** (-np.arange(quarter, dtype=np.float64) / quarter)
    ar = (pos // GRID_W)[:, None] * freqs
    ac = (pos % GRID_W)[:, None] * freqs
    cos64 = np.concatenate([np.cos(ar), np.cos(ar), np.cos(ac), np.cos(ac)], axis=1)
    sin64 = np.concatenate([-np.sin(ar), np.sin(ar), -np.sin(ac), np.sin(ac)], axis=1)
    return jnp.asarray(np.tile(cos64, (1, 2)), dtype=F32), jnp.asarray(np.tile(sin64, (1, 2)), dtype=F32)


def _head_mean_matrix(width):
    idx = np.arange(width) // HEAD_DIM
    return jnp.asarray((idx[:, None] == idx[None, :]).astype(np.float32) / HEAD_DIM).astype(BF16)


def _channel_dft_matrix():
    idx = np.arange(C_GROUP_W)
    ang = (2.0 * np.pi / C_GROUP_W) * ((idx[:, None] * idx[None, :]) % C_GROUP_W)
    scale = C_GROUP_W ** -0.5
    mat = np.concatenate([np.cos(ang), np.sin(ang)], axis=1) * scale
    return jnp.asarray(mat, dtype=F32).astype(BF16)


def kernel(x, c, ctx, c_ctx, w_ada, b_ada, w_in, q_norm, k_norm, rpb, w_branch, w_out,
           ln1_g, ln1_b, w_gu, w_down, ln2_g, ln2_b):
    batch, seq, d = x.shape
    ctx_len = ctx.shape[1]
    depth = w_ada.shape[0]
    n_groups = seq // (Q_ROWS * GRID_W)
    assert d == D_MODEL and seq % (Q_ROWS * GRID_W) == 0 and seq // GRID_W >= WIN_ROWS
    assert n_groups >= 2
    assert ctx_len % LANES == 0
    alpha = (2.0 * depth) ** 0.25
    tm = 1024
    tmc = min(ctx_len, 1024)
    assert seq % tm == 0 and tm % ROW_BLOCK == 0 and tmc % ROW_BLOCK == 0

    pad = (-(batch + 1)) % 8
    cc = jnp.concatenate([c, c_ctx[None, :], jnp.zeros((pad, d), F32)], axis=0)
    mods = _ada(cc, w_ada, b_ada)

    cos_t, sin_t = _rope_tables(seq)
    cos_c = jnp.ones((ctx_len, LANES), F32)
    sin_c = jnp.zeros((ctx_len, LANES), F32)
    g512 = _head_mean_matrix(512)
    g128 = _head_mean_matrix(128)
    dft_c = _channel_dft_matrix()
    tabs_l = _dft_factor_tables(seq)
    tabs_c = _dft_factor_tables(ctx_len)

    xl = x.reshape(batch * seq, d)
    xc = ctx.reshape(batch * ctx_len, d)
    for l in range(depth):
        with_ctx = l < depth - 1
        w_qkvu = w_in[l, :, :N_QKVU].astype(BF16)
        w_gate = w_in[l, :, N_QKVU:].astype(BF16)
        wb = w_branch[l].astype(BF16)
        wo = w_out[l].astype(BF16)
        wgu = w_gu[l].astype(BF16)
        wd = w_down[l].astype(BF16)
        q_gain = jnp.tile(q_norm[l] * Q_SCALE, A_HEADS)[None, :]
        k_gain = jnp.tile(k_norm[l], A_KV_HEADS)[None, :]
        mod_l = mods[l, :batch][:, None, :]
        mod_c = jnp.broadcast_to(mods[l, batch][None, None, :], (batch, 1, 6 * d))
        ln1 = (ln1_g[l][None, :], ln1_b[l][None, :])
        ln2 = (ln2_g[l][None, :], ln2_b[l][None, :])

        qa, ka, va1, qb, kb, vb1, uc, us = _inproj(
            xl, mod_l, w_qkvu, q_gain, k_gain, cos_t, sin_t, g512, g128, dft_c, batch=batch, seq=seq, tm=tm)
        qac, kac, vac1, qbc, kbc, vbc1, ucc, usc = _inproj(
            xc, mod_c, w_qkvu, q_gain, k_gain, cos_c, sin_c, g512, g128, dft_c, batch=batch, seq=ctx_len, tm=tmc)

        score_bound = (1.01 * HEAD_DIM * jnp.max(jnp.abs(q_gain)) * jnp.max(jnp.abs(k_gain))).reshape(1)
        oa = _attn_a(score_bound, qa, ka, va1, kac, vac1, batch=batch, seq=seq, ctx_len=ctx_len, tq=256, tk=256)
        bias_t = _neighbourhood_bias_t(rpb[l], seq // GRID_W)
        ob = _attn_b(qb, kb, vb1, kbc, vbc1, bias_t, batch=batch, seq=seq, ctx_len=ctx_len, tk=512)
        oc = _dft(tabs_l, uc, us)

        x1 = _merge(xl, mod_l, oa, ob, oc, w_gate, wb, wo, *ln1, batch=batch, seq=seq, tm=tm, alpha=alpha)
        xl = _ffn(x1, mod_l, wgu, wd, *ln2, batch=batch, seq=seq, tm=tm, alpha=alpha)

        if with_ctx:
            oac, obc = _attn_ctx(qac, kac, vac1, qbc, kbc, vbc1, batch=batch, ctx_len=ctx_len)
            occ = _dft(tabs_c, ucc, usc)
            xc1 = _merge(xc, mod_c, oac, obc, occ, w_gate, wb, wo, *ln1,
                         batch=batch, seq=ctx_len, tm=tmc, alpha=alpha)
            xc = _ffn(xc1, mod_c, wgu, wd, *ln2, batch=batch, seq=ctx_len, tm=tmc, alpha=alpha)
    return xl.reshape(batch, seq, d)
```

```python
import functools
import math

import numpy as np
import jax
import jax.numpy as jnp
from jax import lax
from jax.experimental import pallas as pl
from jax.experimental.pallas import tpu as pltpu

F32 = jnp.float32
BF16 = jnp.bfloat16

D_MODEL = 1024
GRID_W = 64
HEAD_DIM = 64
A_HEADS = 8
A_KV_HEADS = 2
A_GROUP = A_HEADS // A_KV_HEADS
B_HEADS = 8
NA_ROWS = 8
NA_COLS = 16
C_GROUPS = 4
C_GROUP_W = 128
D_FF = 2816
ROPE_THETA = 10000.0
LN_EPS = 1e-6
RMS_EPS = 1e-6
N_QKVU = 2816
Q_ROWS = 8
WIN_ROWS = 16
MAX_GROUPS_PER_TRIP = 6
ROW_BLOCK = 256
MAX_SHIFT_BOUND = 40.0
NEG = -1e30
LOG2E = math.log2(math.e)
Q_SCALE = HEAD_DIM ** -0.5 * LOG2E
LANES = 128
VMEM_LIMIT = 56 * 1024 * 1024


def _cparams(*sem):
    return pltpu.CompilerParams(dimension_semantics=sem, vmem_limit_bytes=VMEM_LIMIT)


def _resident_spec(a):
    return pl.BlockSpec(a.shape, lambda *_: (0,) * a.ndim, pipeline_mode=pl.Buffered(1))


def _dot(a, b):
    return jnp.dot(a, b, preferred_element_type=F32)


def _dot_nt(a, b):
    return lax.dot_general(a, b, (((1,), (1,)), ((), ())), preferred_element_type=F32)


def _dot_tn(a, b):
    return lax.dot_general(a, b, (((0,), (0,)), ((), ())), preferred_element_type=F32)


def _row_blocks(rows):
    return [slice(r, r + ROW_BLOCK) for r in range(0, rows, ROW_BLOCK)]


def _layernorm(x):
    mu = jnp.mean(x, axis=-1, keepdims=True)
    xc = x - mu
    var = jnp.mean(xc * xc, axis=-1, keepdims=True)
    return xc * lax.rsqrt(var + LN_EPS)


def _ada_kernel(c_ref, w_ref, b_ref, o_ref):
    c = c_ref[...]
    a = c * (1.0 / (1.0 + jnp.exp(-c)))
    a_hi = a.astype(BF16)
    a_lo = (a - a_hi.astype(F32)).astype(BF16)
    w = w_ref[...]
    w_hi = w.astype(BF16)
    w_lo = (w - w_hi.astype(F32)).astype(BF16)
    o_ref[...] = _dot(a_hi, w_hi) + _dot(a_hi, w_lo) + _dot(a_lo, w_hi) + b_ref[...]


def _ada(cc, w_ada, b_ada):
    depth, d, n = w_ada.shape
    tn = 1536
    return pl.pallas_call(
        _ada_kernel,
        grid=(depth, n // tn),
        in_specs=[
            pl.BlockSpec((cc.shape[0], d), lambda l, j: (0, 0)),
            pl.BlockSpec((None, d, tn), lambda l, j: (l, 0, j)),
            pl.BlockSpec((None, 1, tn), lambda l, j: (l, 0, j)),
        ],
        out_specs=pl.BlockSpec((None, cc.shape[0], tn), lambda l, j: (l, 0, j)),
        out_shape=jax.ShapeDtypeStruct((depth, cc.shape[0], n), F32),
        compiler_params=_cparams("arbitrary", "arbitrary"),
        name="ada",
    )(cc, w_ada, b_ada.reshape(depth, 1, n))


def _swap16(x, lane_lo):
    up = pltpu.roll(x, 16, 1)
    dn = pltpu.roll(x, LANES - 16, 1)
    return jnp.where(lane_lo, dn, up)


def _with_ones(v):
    ones = jnp.ones((v.shape[0], HEAD_DIM), F32)
    parts = []
    for hd in range(v.shape[1] // HEAD_DIM):
        parts += [v[:, hd * HEAD_DIM:(hd + 1) * HEAD_DIM], ones]
    return jnp.concatenate(parts, axis=1).astype(BF16)


def _inproj_kernel(x_ref, mod_ref, w_ref, qg_ref, kg_ref, cos_ref, sin_ref, g512_ref, g128_ref, dft_ref,
                   qa_ref, ka_ref, va_ref, qb_ref, kb_ref, vb_ref, uc_ref, us_ref):
    mod = mod_ref[...]
    sh = mod[:, 0:D_MODEL]
    sc = mod[:, D_MODEL:2 * D_MODEL]
    h = (_layernorm(x_ref[...]) * (1.0 + sc) + sh).astype(BF16)

    cos = cos_ref[...]
    sin = sin_ref[...]
    lane_lo = (lax.broadcasted_iota(jnp.int32, cos.shape, 1) % 32) < 16

    def norm_rope(v, ms, gain):
        vn = v * lax.rsqrt(ms + RMS_EPS) * gain
        outs = []
        for j in range(v.shape[1] // LANES):
            t = vn[:, j * LANES:(j + 1) * LANES]
            outs.append(t * cos + _swap16(t, lane_lo) * sin)
        return outs[0] if len(outs) == 1 else jnp.concatenate(outs, axis=1)

    q = _dot(h, w_ref[:, 0:512])
    kv = _dot(h, w_ref[:, 512:768])
    qb_ref[...] = (_dot(h, w_ref[:, 768:1280]) * Q_SCALE).astype(BF16)
    kb_ref[...] = _dot(h, w_ref[:, 1280:1792]).astype(BF16)
    k = kv[:, 0:128]
    ms_q = _dot((q * q).astype(BF16), g512_ref[...])
    ms_k = _dot((k * k).astype(BF16), g128_ref[...])
    vb_ref[...] = _with_ones(_dot(h, w_ref[:, 1792:2304]))
    u = _dot(h, w_ref[:, 2304:2816]).astype(BF16)
    qa_ref[...] = norm_rope(q, ms_q, qg_ref[...]).astype(BF16)
    ka_ref[...] = norm_rope(k, ms_k, kg_ref[...]).astype(BF16)
    va_ref[...] = _with_ones(kv[:, 128:256])
    dft = dft_ref[...]
    for g in range(C_GROUPS):
        z = _dot(u[:, g * C_GROUP_W:(g + 1) * C_GROUP_W], dft)
        uc_ref[:, g * C_GROUP_W:(g + 1) * C_GROUP_W] = z[:, 0:C_GROUP_W].astype(BF16)
        us_ref[:, g * C_GROUP_W:(g + 1) * C_GROUP_W] = z[:, C_GROUP_W:2 * C_GROUP_W].astype(BF16)


def _inproj(x2, mod3, w_qkvu, q_gain, k_gain, cos_t, sin_t, g512, g128, dft_c, *, batch, seq, tm):
    rows = batch * seq
    nl = seq // tm
    tok = lambda w: pl.BlockSpec((tm, w), lambda i: (i, 0))
    full = _resident_spec
    fo = pl.BlockSpec((tm, 512), lambda i: (i % nl, i // nl))
    o_tok = lambda w: jax.ShapeDtypeStruct((rows, w), BF16)
    return pl.pallas_call(
        _inproj_kernel,
        grid=(rows // tm,),
        in_specs=[
            tok(D_MODEL),
            pl.BlockSpec((None, 1, mod3.shape[2]), lambda i: (i // nl, 0, 0)),
            full(w_qkvu), full(q_gain), full(k_gain),
            pl.BlockSpec((tm, LANES), lambda i: (i % nl, 0)),
            pl.BlockSpec((tm, LANES), lambda i: (i % nl, 0)),
            full(g512), full(g128), full(dft_c),
        ],
        out_specs=[tok(512), tok(128), tok(256), tok(512), tok(512), tok(1024), fo, fo],
        out_shape=[o_tok(512), o_tok(128), o_tok(256), o_tok(512), o_tok(512), o_tok(1024),
                   jax.ShapeDtypeStruct((seq, batch * 512), BF16),
                   jax.ShapeDtypeStruct((seq, batch * 512), BF16)],
        compiler_params=_cparams("arbitrary"),
        name="inproj",
    )(x2, mod3, w_qkvu, q_gain, k_gain, cos_t, sin_t, g512, g128, dft_c)


def _attend_t_multi(qps, chunk_lists, shift=None):
    n = len(qps)
    m = [None] * n
    acc = [None] * n
    units = [(i, c) for c in range(len(chunk_lists[0])) for i in range(n)]

    def scores(u):
        i, c = u
        chunk = chunk_lists[i][c]
        s = _dot_nt(chunk[0], qps[i])
        return s if chunk[2] is None or len(chunk) == 4 else s + chunk[2]

    def banded(s, bias_fn, band):
        cols = [slice(b * LANES, (b + 1) * LANES) for b in range(len(band))]
        subs = [s[lo:hi, cs] + bias_fn(lo, hi, cs) for (lo, hi), cs in zip(band, cols)]
        mc = jnp.concatenate([jnp.max(x, axis=0, keepdims=True) for x in subs], axis=1)

        def probs(m_ref):
            out = []
            for x, (lo, hi), cs in zip(subs, band, cols):
                parts = [jnp.exp2(x - m_ref[:, cs]).astype(BF16)]
                if lo:
                    parts.insert(0, jnp.zeros((lo, LANES), BF16))
                if s.shape[0] - hi:
                    parts.append(jnp.zeros((s.shape[0] - hi, LANES), BF16))
                out.append(parts[0] if len(parts) == 1 else jnp.concatenate(parts, axis=0))
            return jnp.concatenate(out, axis=1)

        return mc, probs

    s_next = scores(units[0])
    for idx, (i, c) in enumerate(units):
        s = s_next
        if idx + 1 < len(units):
            s_next = scores(units[idx + 1])
        chunk = chunk_lists[i][c]
        v1 = chunk[1]
        if shift is not None:
            d = _dot_tn(v1, jnp.exp2(s - shift).astype(BF16))
            acc[i] = d if acc[i] is None else acc[i] + d
            continue
        if len(chunk) == 4:
            mc, probs = banded(s, chunk[2], chunk[3])
        else:
            mc, probs = jnp.max(s, axis=0, keepdims=True), lambda m_ref, s=s: jnp.exp2(s - m_ref).astype(BF16)
        if m[i] is None:
            m[i] = mc
            acc[i] = _dot_tn(v1, probs(mc))
        else:
            m_new = jnp.maximum(m[i], mc)
            acc[i] = jnp.exp2(m[i] - m_new) * acc[i] + _dot_tn(v1, probs(m_new))
            m[i] = m_new
    return [a[0:HEAD_DIM, :] / a[HEAD_DIM:HEAD_DIM + 1, :] for a in acc]


def _place_head(q128, src_half, dst_half, lane_hi):
    x = q128.astype(F32)
    if src_half != dst_half:
        x = pltpu.roll(x, 64, 1)
    keep = lane_hi if dst_half == 1 else jnp.logical_not(lane_hi)
    return jnp.where(keep, x, 0.0).astype(BF16)


def _gqa_tile_t(q, chunks_fn, tq, lane_hi, shift=None):
    qps = []
    for g in range(A_KV_HEADS):
        parts = []
        for j in range(A_GROUP):
            hd = A_GROUP * g + j
            parts.append(_place_head(q[:, LANES * (hd // 2):LANES * (hd // 2 + 1)], hd % 2, g, lane_hi))
        qps.append(jnp.concatenate(parts, axis=0))
    outs = _attend_t_multi(qps, [chunks_fn(g) for g in range(A_KV_HEADS)], shift)
    heads_t = [o_t[:, j * tq:(j + 1) * tq] for o_t in outs for j in range(A_GROUP)]
    return jnp.concatenate(heads_t, axis=0).T


def _mha_pair_t(q128, chunks_fn, lane_hi):
    outs = _attend_t_multi([_place_head(q128, hh, hh, lane_hi) for hh in range(2)],
                           [chunks_fn(hh) for hh in range(2)])
    return jnp.concatenate(outs, axis=0).T


def _attn_a_kernel(bound_ref, q_ref, k_ref, v_ref, kc_ref, vc_ref, o_ref, *, tq, chunk_bounds, use_bound):
    lane_hi = lax.broadcasted_iota(jnp.int32, (tq, LANES), 1) >= 64

    def chunks(g):
        sl = slice(g * LANES, (g + 1) * LANES)
        return ([(kc_ref[...], vc_ref[:, sl], None)]
                + [(k_ref[s:e, :], v_ref[s:e, sl], None) for s, e in chunk_bounds])

    shift = bound_ref[0] if use_bound else None
    o_ref[...] = _gqa_tile_t(q_ref[...], chunks, tq, lane_hi, shift=shift).astype(BF16)


def _attn_a(score_bound, qa, ka, va1, kac, vac1, *, batch, seq, ctx_len, tq, tk, use_bound):
    bounds = [(i, min(i + tk, seq)) for i in range(0, seq, tk)]
    nq = seq // tq
    return pl.pallas_call(
        functools.partial(_attn_a_kernel, tq=tq, chunk_bounds=bounds, use_bound=use_bound),
        grid=(batch, nq),
        in_specs=[pl.BlockSpec(memory_space=pltpu.SMEM),
                  pl.BlockSpec((tq, 512), lambda b, i: (b * nq + i, 0)),
                  pl.BlockSpec((seq, LANES), lambda b, i: (b, 0)),
                  pl.BlockSpec((seq, 2 * LANES), lambda b, i: (b, 0)),
                  pl.BlockSpec((ctx_len, LANES), lambda b, i: (b, 0)),
                  pl.BlockSpec((ctx_len, 2 * LANES), lambda b, i: (b, 0))],
        out_specs=pl.BlockSpec((tq, 512), lambda b, i: (b * nq + i, 0)),
        out_shape=jax.ShapeDtypeStruct((batch * seq, 512), BF16),
        compiler_params=_cparams("arbitrary", "arbitrary"),
        name="attn_a",
    )(score_bound, qa, ka, va1, kac, vac1)


def _attn_b_kernel(q_ref, k_ref, v_ref, kc_ref, vc_ref, bias_ref, o_ref, *, rows_n, tk):
    nq = Q_ROWS * GRID_W
    nw = WIN_ROWS * GRID_W
    n_groups = rows_n // Q_ROWS
    lane_hi = lax.broadcasted_iota(jnp.int32, (nq, LANES), 1) >= 64
    kc = kc_ref[...]

    def key_band(g_pattern, b):
        w0 = int(np.clip(Q_ROWS * g_pattern - (WIN_ROWS - Q_ROWS) // 2, 0, rows_n - WIN_ROWS))
        r0s = [int(np.clip(Q_ROWS * g_pattern + a - NA_ROWS // 2, 0, rows_n - NA_ROWS)) - w0
               for a in range(b * LANES // GRID_W, (b + 1) * LANES // GRID_W)]
        return min(r0s) * GRID_W, (max(r0s) + NA_ROWS) * GRID_W

    def run(groups):
        qps, chunk_lists, q0s = [], [], []
        for g, var, g_pattern in groups:
            if isinstance(g, int):
                q0 = g * nq
                k0 = int(np.clip(Q_ROWS * g - (WIN_ROWS - Q_ROWS) // 2, 0, rows_n - WIN_ROWS)) * GRID_W
            else:
                q0 = pl.multiple_of(g * nq, nq)
                w0 = jnp.clip(Q_ROWS * g - (WIN_ROWS - Q_ROWS) // 2, 0, rows_n - WIN_ROWS)
                k0 = pl.multiple_of(w0 * GRID_W, GRID_W)
            q128 = q_ref[pl.ds(q0, nq), :]
            q0s.append(q0)
            for hh in range(2):
                sl = slice(hh * LANES, (hh + 1) * LANES)
                qps.append(_place_head(q128, hh, hh, lane_hi))
                chunks = []
                for s in range(0, nw, tk):
                    k = k_ref[pl.ds(k0 + s, tk), :]
                    v = v_ref[pl.ds(k0 + s, tk), sl]
                    band = []
                    for b in range(nq // LANES):
                        lo, hi = key_band(g_pattern, b)
                        band.append((max(lo, s) - s, min(hi, s + tk) - s))
                    if all(lo < hi for lo, hi in band):
                        bias_fn = (lambda lo, hi, cs, s=s, hh=hh, var=var:
                                   bias_ref[var, hh, s + lo:s + hi, cs])
                        chunks.append((k, v, bias_fn, band))
                    else:
                        chunks.append((k, v, bias_ref[var, hh, s:s + tk, :]))
                chunk_lists.append(chunks + [(kc, vc_ref[:, sl], None)])
        outs = _attend_t_multi(qps, chunk_lists)
        for u in range(len(groups)):
            o_ref[pl.ds(q0s[u], nq), :] = jnp.concatenate(outs[2 * u:2 * u + 2], axis=0).T.astype(BF16)

    edges = [(0, 0, 0), (n_groups - 1, 2, n_groups - 1)]
    interior = n_groups - 2
    per_trip = max([d for d in range(1, MAX_GROUPS_PER_TRIP + 1) if interior % d == 0], default=0)
    if interior == per_trip:
        run(edges + [(1 + u, 1, 1) for u in range(per_trip)])
    else:
        run(edges)

        def body(t, carry):
            run([(1 + t * per_trip + u, 1, 1) for u in range(per_trip)])
            return carry

        lax.fori_loop(0, interior // per_trip, body, 0)


def _attn_b(qb, kb, vb1, kbc, vbc1, bias_t, *, batch, seq, ctx_len, tk):
    rows_n = seq // GRID_W
    nq = Q_ROWS * GRID_W
    nw = WIN_ROWS * GRID_W
    lat = lambda w: pl.BlockSpec((seq, w), lambda hp, b: (b, hp))
    ctx = lambda w: pl.BlockSpec((ctx_len, w), lambda hp, b: (b, hp))
    return pl.pallas_call(
        functools.partial(_attn_b_kernel, rows_n=rows_n, tk=tk),
        grid=(B_HEADS // 2, batch),
        in_specs=[lat(LANES), lat(LANES), lat(2 * LANES), ctx(LANES), ctx(2 * LANES),
                  pl.BlockSpec((3, 2, nw, nq), lambda hp, b: (0, hp, 0, 0))],
        out_specs=lat(LANES),
        out_shape=jax.ShapeDtypeStruct((batch * seq, 512), BF16),
        compiler_params=_cparams("arbitrary", "arbitrary"),
        name="attn_b",
    )(qb, kb, vb1, kbc, vbc1, bias_t)


def _neighbourhood_bias_t(rpb, rows_n):
    kc = NA_COLS
    cols = np.arange(GRID_W)
    c0 = np.clip(cols - kc // 2, 0, GRID_W - kc)
    col_ok = (cols[None, :] >= c0[:, None]) & (cols[None, :] < c0[:, None] + kc)
    heads, n_dr, n_dc = rpb.shape
    dc = cols[:, None] - cols[None, :] + NA_COLS - 1
    onehot = (np.arange(n_dc)[:, None, None] == dc[None]).astype(np.float32)
    pad = Q_ROWS
    e_max = n_dr - 1 + 2 * pad
    t = jnp.einsum("hdj,jxy->hxdy", rpb[:, ::-1] * LOG2E, jnp.asarray(onehot), precision=lax.Precision.HIGHEST)
    t = jnp.where(col_ok.T[None, :, None, :], t, NEG)
    rev = jnp.pad(t, ((0, 0), (0, 0), (pad, pad), (0, 0))).reshape(heads, GRID_W, (e_max + 1) * GRID_W)
    nq = Q_ROWS * GRID_W
    width = -(-((e_max + 1) * GRID_W + LANES) // LANES) * LANES
    rev = jnp.pad(rev, ((0, 0), (0, 0), (0, width - rev.shape[2])))
    n_groups = rows_n // Q_ROWS
    plans = []
    for g in (0, min(1, n_groups - 1), n_groups - 1):
        w0 = int(np.clip(Q_ROWS * g - (WIN_ROWS - Q_ROWS) // 2, 0, rows_n - WIN_ROWS))
        plan = []
        for i in range(WIN_ROWS):
            d0 = w0 + i - Q_ROWS * g + NA_ROWS - 1
            ok = [a for a in range(Q_ROWS)
                  if int(np.clip(Q_ROWS * g + a - NA_ROWS // 2, 0, rows_n - NA_ROWS)) <= w0 + i
                  < int(np.clip(Q_ROWS * g + a - NA_ROWS // 2, 0, rows_n - NA_ROWS)) + NA_ROWS]
            assert ok == list(range(ok[0], ok[-1] + 1)) if ok else True
            plan.append(((e_max - pad - d0) * GRID_W, (ok[0], ok[-1] + 1) if ok else (0, 0)))
        plans.append(plan)
    return pl.pallas_call(
        functools.partial(_bias_kernel, plans=plans),
        grid=(3, heads),
        in_specs=[pl.BlockSpec((None, GRID_W, width), lambda v, h: (h, 0, 0))],
        out_specs=pl.BlockSpec((None, None, WIN_ROWS * GRID_W, nq), lambda v, h: (v, h, 0, 0)),
        out_shape=jax.ShapeDtypeStruct((3, heads, WIN_ROWS * GRID_W, nq), F32),
        compiler_params=_cparams("arbitrary", "arbitrary"),
        name="nbr_bias",
    )(rev)


def _bias_kernel(rev_ref, o_ref, *, plans):
    nq = o_ref.shape[1]
    q_row = lax.broadcasted_iota(jnp.int32, (GRID_W, nq), 1) // GRID_W
    for v, plan in enumerate(plans):
        @pl.when(pl.program_id(0) == v)
        def _(plan=plan):
            for i, (off, (lo, hi)) in enumerate(plan):
                rows = slice(i * GRID_W, (i + 1) * GRID_W)
                if lo >= hi:
                    o_ref[rows, :] = jnp.full((GRID_W, nq), NEG, F32)
                    continue
                al = off // LANES * LANES
                strip = rev_ref[:, al:al + nq + LANES][:, off - al:off - al + nq]
                o_ref[rows, :] = jnp.where((q_row >= lo) & (q_row < hi), strip, NEG)


def _attn_ctx_kernel(qa_ref, ka_ref, va_ref, qb_ref, kb_ref, vb_ref, oa_ref, ob_ref, *, lc):
    lane_hi = lax.broadcasted_iota(jnp.int32, (lc, LANES), 1) >= 64
    ka = ka_ref[...]
    oa_ref[...] = _gqa_tile_t(
        qa_ref[...], lambda g: [(ka, va_ref[:, g * LANES:(g + 1) * LANES], None)], lc, lane_hi).astype(BF16)
    for hp in range(B_HEADS // 2):
        sl = slice(hp * LANES, (hp + 1) * LANES)
        k = kb_ref[:, sl]
        chunks = lambda hh: [(k, vb_ref[:, (2 * hp + hh) * LANES:(2 * hp + hh + 1) * LANES], None)]
        ob_ref[:, sl] = _mha_pair_t(qb_ref[:, sl], chunks, lane_hi).astype(BF16)


def _attn_ctx(qac, kac, vac1, qbc, kbc, vbc1, *, batch, ctx_len):
    t = lambda w: pl.BlockSpec((ctx_len, w), lambda b: (b, 0))
    o = jax.ShapeDtypeStruct((batch * ctx_len, 512), BF16)
    return pl.pallas_call(
        functools.partial(_attn_ctx_kernel, lc=ctx_len),
        grid=(batch,),
        in_specs=[t(512), t(128), t(256), t(512), t(512), t(1024)],
        out_specs=[t(512), t(512)],
        out_shape=[o, o],
        compiler_params=_cparams("arbitrary"),
        name="attn_ctx",
    )(qac, kac, vac1, qbc, kbc, vbc1)


def _dft_kernel(ac_ref, as_ref, bc_ref, bs_ref, uc_ref, us_ref, o_ref, ct_ref, st_ref, *, bm):
    i = pl.program_id(0)

    @pl.when(pl.program_id(1) == 0)
    def _():
        bc = bc_ref[...]
        bs = bs_ref[...]
        for r in range(bm // GRID_W):
            a_c = ac_ref[pl.ds(i * (bm // GRID_W) + r, 1), :]
            a_s = as_ref[pl.ds(i * (bm // GRID_W) + r, 1), :]
            ct_ref[r * GRID_W:(r + 1) * GRID_W, :] = (a_c * bc - a_s * bs).astype(BF16)
            st_ref[r * GRID_W:(r + 1) * GRID_W, :] = (a_s * bc + a_c * bs).astype(BF16)

    o_ref[...] = (_dot(ct_ref[...], uc_ref[...]) - _dot(st_ref[...], us_ref[...])).astype(BF16)


def _dft(tabs, uc, us):
    n, width = uc.shape
    bm = min(n, 1024)
    bn = min(width, 512)
    u_spec = pl.BlockSpec((n, bn), lambda i, j: (0, j))
    return pl.pallas_call(
        functools.partial(_dft_kernel, bm=bm),
        grid=(n // bm, width // bn),
        in_specs=[_resident_spec(t) for t in tabs] + [u_spec, u_spec],
        out_specs=pl.BlockSpec((bm, bn), lambda i, j: (i, j)),
        out_shape=jax.ShapeDtypeStruct((n, width), BF16),
        scratch_shapes=[pltpu.VMEM((bm, n), BF16), pltpu.VMEM((bm, n), BF16)],
        compiler_params=_cparams("arbitrary", "arbitrary"),
        name="dft",
    )(*tabs, uc, us)


def _dft_factor_tables(n):
    r = n // GRID_W
    k = np.arange(n, dtype=np.int64)
    ang_a = (2.0 * np.pi / r) * ((np.arange(r)[:, None] * k[None, :]) % r)
    ang_b = (2.0 * np.pi / n) * ((np.arange(GRID_W)[:, None] * k[None, :]) % n)
    pad = ((0, (-r) % 8), (0, 0))
    scale = n ** -0.5
    return tuple(jnp.asarray(t, dtype=F32) for t in (
        np.pad(np.cos(ang_a), pad), np.pad(np.sin(ang_a), pad), np.cos(ang_b) * scale, np.sin(ang_b) * scale))


def _merge_kernel(x_ref, mod_ref, oa_ref, ob_ref, oc_ref, wg_ref, wb_ref, wo_ref, g_ref, b_ref, o_ref, *, alpha):
    mod = mod_ref[...]
    sh = mod[:, 0:D_MODEL]
    sc = mod[:, D_MODEL:2 * D_MODEL]
    gate = mod[:, 2 * D_MODEL:3 * D_MODEL]
    blocks = _row_blocks(x_ref.shape[0])
    xs = [x_ref[b, :] for b in blocks]
    hs = [(_layernorm(x) * (1.0 + sc) + sh).astype(BF16) for x in xs]
    ys = [None] * len(blocks)
    for i, br_ref in enumerate((oa_ref, ob_ref, oc_ref)):
        for r, b in enumerate(blocks):
            gz = _dot(hs[r], wg_ref[:, i * D_MODEL:(i + 1) * D_MODEL])
            t = _dot(br_ref[b, :], wb_ref[i]) * (1.0 / (1.0 + jnp.exp(-gz)))
            ys[r] = t if ys[r] is None else ys[r] + t
    for r, b in enumerate(blocks):
        yo = _dot(ys[r].astype(BF16), wo_ref[...])
        o_ref[b, :] = _layernorm(alpha * xs[r] + gate * yo) * g_ref[...] + b_ref[...]


def _merge(x2, mod3, oa, ob, oc, w_gate, w_branch, w_out, ln_g, ln_b, *, batch, seq, tm, alpha):
    rows = batch * seq
    nl = seq // tm
    tok = lambda w: pl.BlockSpec((tm, w), lambda i: (i, 0))
    full = _resident_spec
    return pl.pallas_call(
        functools.partial(_merge_kernel, alpha=alpha),
        grid=(rows // tm,),
        in_specs=[
            tok(D_MODEL),
            pl.BlockSpec((None, 1, mod3.shape[2]), lambda i: (i // nl, 0, 0)),
            tok(512), tok(512),
            pl.BlockSpec((tm, 512), lambda i: (i % nl, i // nl)),
            full(w_gate), full(w_branch), full(w_out), full(ln_g), full(ln_b),
        ],
        out_specs=tok(D_MODEL),
        out_shape=jax.ShapeDtypeStruct((rows, D_MODEL), F32),
        compiler_params=_cparams("arbitrary"),
        name="merge",
    )(x2, mod3, oa, ob, oc, w_gate, w_branch, w_out, ln_g, ln_b)


FF_CHUNKS = ((0, 768), (768, 1536), (1536, 2304), (2304, 2816))


def _ffn_kernel(x_ref, mod_ref, wgu_ref, wd_ref, g_ref, b_ref, o_ref, *, alpha):
    mod = mod_ref[...]
    sh = mod[:, 3 * D_MODEL:4 * D_MODEL]
    sc = mod[:, 4 * D_MODEL:5 * D_MODEL]
    gate = mod[:, 5 * D_MODEL:6 * D_MODEL]
    blocks = _row_blocks(x_ref.shape[0])
    xs = [x_ref[b, :] for b in blocks]
    hs = [(_layernorm(x) * (1.0 + sc) + sh).astype(BF16) for x in xs]
    fs = [None] * len(blocks)
    for s, e in FF_CHUNKS:
        for i in range(len(blocks)):
            g = _dot(hs[i], wgu_ref[:, s:e])
            u = _dot(hs[i], wgu_ref[:, D_FF + s:D_FF + e])
            a = (g * (1.0 / (1.0 + jnp.exp(-g))) * u).astype(BF16)
            t = _dot(a, wd_ref[s:e, :])
            fs[i] = t if fs[i] is None else fs[i] + t
    for i, b in enumerate(blocks):
        o_ref[b, :] = _layernorm(alpha * xs[i] + gate * fs[i]) * g_ref[...] + b_ref[...]


def _ffn(x2, mod3, w_gu, w_down, ln_g, ln_b, *, batch, seq, tm, alpha):
    rows = batch * seq
    nl = seq // tm
    tok = pl.BlockSpec((tm, D_MODEL), lambda i: (i, 0))
    full = _resident_spec
    return pl.pallas_call(
        functools.partial(_ffn_kernel, alpha=alpha),
        grid=(rows // tm,),
        in_specs=[tok, pl.BlockSpec((None, 1, mod3.shape[2]), lambda i: (i // nl, 0, 0)),
                  full(w_gu), full(w_down), full(ln_g), full(ln_b)],
        out_specs=tok,
        out_shape=jax.ShapeDtypeStruct((rows, D_MODEL), F32),
        compiler_params=_cparams("arbitrary"),
        name="ffn",
    )(x2, mod3, w_gu, w_down, ln_g, ln_b)


def _rope_tables(seq):
    quarter = HEAD_DIM // 4
    pos = np.arange(seq)
    freqs = ROPE_THETA ** (-np.arange(quarter, dtype=np.float64) / quarter)
    ar = (pos // GRID_W)[:, None] * freqs
    ac = (pos % GRID_W)[:, None] * freqs
    cos64 = np.concatenate([np.cos(ar), np.cos(ar), np.cos(ac), np.cos(ac)], axis=1)
    sin64 = np.concatenate([-np.sin(ar), np.sin(ar), -np.sin(ac), np.sin(ac)], axis=1)
    return jnp.asarray(np.tile(cos64, (1, 2)), dtype=F32), jnp.asarray(np.tile(sin64, (1, 2)), dtype=F32)


def _head_mean_matrix(width):
    idx = np.arange(width) // HEAD_DIM
    return jnp.asarray((idx[:, None] == idx[None, :]).astype(np.float32) / HEAD_DIM).astype(BF16)


def _channel_dft_matrix():
    idx = np.arange(C_GROUP_W)
    ang = (2.0 * np.pi / C_GROUP_W) * ((idx[:, None] * idx[None, :]) % C_GROUP_W)
    scale = C_GROUP_W ** -0.5
    mat = np.concatenate([np.cos(ang), np.sin(ang)], axis=1) * scale
    return jnp.asarray(mat, dtype=F32).astype(BF16)


def kernel(x, c, ctx, c_ctx, w_ada, b_ada, w_in, q_norm, k_norm, rpb, w_branch, w_out,
           ln1_g, ln1_b, w_gu, w_down, ln2_g, ln2_b):
    batch, seq, d = x.shape
    ctx_len = ctx.shape[1]
    depth = w_ada.shape[0]
    n_groups = seq // (Q_ROWS * GRID_W)
    assert d == D_MODEL and seq % (Q_ROWS * GRID_W) == 0 and seq // GRID_W >= WIN_ROWS
    assert n_groups >= 2
    assert ctx_len % LANES == 0
    alpha = (2.0 * depth) ** 0.25
    tm = 1024
    tmc = min(ctx_len, 1024)
    assert seq % tm == 0 and tm % ROW_BLOCK == 0 and tmc % ROW_BLOCK == 0

    pad = (-(batch + 1)) % 8
    cc = jnp.concatenate([c, c_ctx[None, :], jnp.zeros((pad, d), F32)], axis=0)
    mods = _ada(cc, w_ada, b_ada)

    cos_t, sin_t = _rope_tables(seq)
    cos_c = jnp.ones((ctx_len, LANES), F32)
    sin_c = jnp.zeros((ctx_len, LANES), F32)
    g512 = _head_mean_matrix(512)
    g128 = _head_mean_matrix(128)
    dft_c = _channel_dft_matrix()
    tabs_l = _dft_factor_tables(seq)
    tabs_c = _dft_factor_tables(ctx_len)

    xl = x.reshape(batch * seq, d)
    xc = ctx.reshape(batch * ctx_len, d)
    for l in range(depth):
        with_ctx = l < depth - 1
        w_qkvu = w_in[l, :, :N_QKVU].astype(BF16)
        w_gate = w_in[l, :, N_QKVU:].astype(BF16)
        wb = w_branch[l].astype(BF16)
        wo = w_out[l].astype(BF16)
        wgu = w_gu[l].astype(BF16)
        wd = w_down[l].astype(BF16)
        q_gain = jnp.tile(q_norm[l] * Q_SCALE, A_HEADS)[None, :]
        k_gain = jnp.tile(k_norm[l], A_KV_HEADS)[None, :]
        mod_l = mods[l, :batch][:, None, :]
        mod_c = jnp.broadcast_to(mods[l, batch][None, None, :], (batch, 1, 6 * d))
        ln1 = (ln1_g[l][None, :], ln1_b[l][None, :])
        ln2 = (ln2_g[l][None, :], ln2_b[l][None, :])

        qa, ka, va1, qb, kb, vb1, uc, us = _inproj(
            xl, mod_l, w_qkvu, q_gain, k_gain, cos_t, sin_t, g512, g128, dft_c, batch=batch, seq=seq, tm=tm)
        qac, kac, vac1, qbc, kbc, vbc1, ucc, usc = _inproj(
            xc, mod_c, w_qkvu, q_gain, k_gain, cos_c, sin_c, g512, g128, dft_c, batch=batch, seq=ctx_len, tm=tmc)

        score_bound = (1.01 * HEAD_DIM * jnp.max(jnp.abs(q_gain)) * jnp.max(jnp.abs(k_gain))).reshape(1)
        attn_a = functools.partial(_attn_a, score_bound, qa, ka, va1, kac, vac1,
                                   batch=batch, seq=seq, ctx_len=ctx_len, tk=256)
        oa = lax.cond(score_bound[0] <= MAX_SHIFT_BOUND,
                      lambda: attn_a(tq=512, use_bound=True), lambda: attn_a(tq=256, use_bound=False))
        bias_t = _neighbourhood_bias_t(rpb[l], seq // GRID_W)
        ob = _attn_b(qb, kb, vb1, kbc, vbc1, bias_t, batch=batch, seq=seq, ctx_len=ctx_len, tk=512)
        oc = _dft(tabs_l, uc, us)

        x1 = _merge(xl, mod_l, oa, ob, oc, w_gate, wb, wo, *ln1, batch=batch, seq=seq, tm=tm, alpha=alpha)
        xl = _ffn(x1, mod_l, wgu, wd, *ln2, batch=batch, seq=seq, tm=tm, alpha=alpha)

        if with_ctx:
            oac, obc = _attn_ctx(qac, kac, vac1, qbc, kbc, vbc1, batch=batch, ctx_len=ctx_len)
            occ = _dft(tabs_c, ucc, usc)
            xc1 = _merge(xc, mod_c, oac, obc, occ, w_gate, wb, wo, *ln1,
                         batch=batch, seq=ctx_len, tm=tmc, alpha=alpha)
            xc = _ffn(xc1, mod_c, wgu, wd, *ln2, batch=batch, seq=ctx_len, tm=tmc, alpha=alpha)
    return xl.reshape(batch, seq, d)
```

```python
import functools
import math

import numpy as np
import jax
import jax.numpy as jnp
from jax import lax
from jax.experimental import pallas as pl
from jax.experimental.pallas import tpu as pltpu

F32 = jnp.float32
BF16 = jnp.bfloat16

D_MODEL = 1024
GRID_W = 64
HEAD_DIM = 64
A_HEADS = 8
A_KV_HEADS = 2
A_GROUP = A_HEADS // A_KV_HEADS
B_HEADS = 8
NA_ROWS = 8
NA_COLS = 16
C_GROUPS = 4
C_GROUP_W = 128
D_FF = 2816
ROPE_THETA = 10000.0
LN_EPS = 1e-6
RMS_EPS = 1e-6
N_QKVU = 2816
Q_ROWS = 8
WIN_ROWS = 16
MAX_GROUPS_PER_TRIP = 6
ROW_BLOCK = 256
DFT_K_SLAB = 1024
MAX_SHIFT_BOUND = 40.0
NEG = -1e30
LOG2E = math.log2(math.e)
Q_SCALE = HEAD_DIM ** -0.5 * LOG2E
LANES = 128
VMEM_LIMIT = 56 * 1024 * 1024


def _cparams(*sem):
    return pltpu.CompilerParams(dimension_semantics=sem, vmem_limit_bytes=VMEM_LIMIT)


def _resident_spec(a):
    return pl.BlockSpec(a.shape, lambda *_: (0,) * a.ndim, pipeline_mode=pl.Buffered(1))


def _dot(a, b):
    return jnp.dot(a, b, preferred_element_type=F32)


def _dot_nt(a, b):
    return lax.dot_general(a, b, (((1,), (1,)), ((), ())), preferred_element_type=F32)


def _dot_tn(a, b):
    return lax.dot_general(a, b, (((0,), (0,)), ((), ())), preferred_element_type=F32)


def _row_blocks(rows):
    return [slice(r, r + ROW_BLOCK) for r in range(0, rows, ROW_BLOCK)]


def _layernorm(x):
    mu = jnp.mean(x, axis=-1, keepdims=True)
    xc = x - mu
    var = jnp.mean(xc * xc, axis=-1, keepdims=True)
    return xc * lax.rsqrt(var + LN_EPS)


def _ada_kernel(c_ref, w_ref, b_ref, o_ref):
    c = c_ref[...]
    a = c * (1.0 / (1.0 + jnp.exp(-c)))
    a_hi = a.astype(BF16)
    a_lo = (a - a_hi.astype(F32)).astype(BF16)
    w = w_ref[...]
    w_hi = w.astype(BF16)
    w_lo = (w - w_hi.astype(F32)).astype(BF16)
    o_ref[...] = _dot(a_hi, w_hi) + _dot(a_hi, w_lo) + _dot(a_lo, w_hi) + b_ref[...]


def _ada(cc, w_ada, b_ada):
    depth, d, n = w_ada.shape
    tn = 1536
    return pl.pallas_call(
        _ada_kernel,
        grid=(depth, n // tn),
        in_specs=[
            pl.BlockSpec((cc.shape[0], d), lambda l, j: (0, 0)),
            pl.BlockSpec((None, d, tn), lambda l, j: (l, 0, j)),
            pl.BlockSpec((None, 1, tn), lambda l, j: (l, 0, j)),
        ],
        out_specs=pl.BlockSpec((None, cc.shape[0], tn), lambda l, j: (l, 0, j)),
        out_shape=jax.ShapeDtypeStruct((depth, cc.shape[0], n), F32),
        compiler_params=_cparams("arbitrary", "arbitrary"),
        name="ada",
    )(cc, w_ada, b_ada.reshape(depth, 1, n))


def _swap16(x, lane_lo):
    up = pltpu.roll(x, 16, 1)
    dn = pltpu.roll(x, LANES - 16, 1)
    return jnp.where(lane_lo, dn, up)


def _with_ones(v):
    ones = jnp.ones((v.shape[0], HEAD_DIM), F32)
    parts = []
    for hd in range(v.shape[1] // HEAD_DIM):
        parts += [v[:, hd * HEAD_DIM:(hd + 1) * HEAD_DIM], ones]
    return jnp.concatenate(parts, axis=1).astype(BF16)


def _inproj_kernel(x_ref, mod_ref, w_ref, qg_ref, kg_ref, cos_ref, sin_ref, g512_ref, g128_ref, dft_ref,
                   qa_ref, ka_ref, va_ref, qb_ref, kb_ref, vb_ref, uc_ref, us_ref):
    mod = mod_ref[...]
    sh = mod[:, 0:D_MODEL]
    sc = mod[:, D_MODEL:2 * D_MODEL]
    h = (_layernorm(x_ref[...]) * (1.0 + sc) + sh).astype(BF16)

    cos = cos_ref[...]
    sin = sin_ref[...]
    lane_lo = (lax.broadcasted_iota(jnp.int32, cos.shape, 1) % 32) < 16

    def norm_rope(v, ms, gain):
        vn = v * lax.rsqrt(ms + RMS_EPS) * gain
        outs = []
        for j in range(v.shape[1] // LANES):
            t = vn[:, j * LANES:(j + 1) * LANES]
            outs.append(t * cos + _swap16(t, lane_lo) * sin)
        return outs[0] if len(outs) == 1 else jnp.concatenate(outs, axis=1)

    q = _dot(h, w_ref[:, 0:512])
    kv = _dot(h, w_ref[:, 512:768])
    qb_ref[...] = (_dot(h, w_ref[:, 768:1280]) * Q_SCALE).astype(BF16)
    kb_ref[...] = _dot(h, w_ref[:, 1280:1792]).astype(BF16)
    k = kv[:, 0:128]
    ms_q = _dot((q * q).astype(BF16), g512_ref[...])
    ms_k = _dot((k * k).astype(BF16), g128_ref[...])
    vb_ref[...] = _with_ones(_dot(h, w_ref[:, 1792:2304]))
    u = _dot(h, w_ref[:, 2304:2816]).astype(BF16)
    qa_ref[...] = norm_rope(q, ms_q, qg_ref[...]).astype(BF16)
    ka_ref[...] = norm_rope(k, ms_k, kg_ref[...]).astype(BF16)
    va_ref[...] = _with_ones(kv[:, 128:256])
    dft = dft_ref[...]
    for g in range(C_GROUPS):
        z = _dot(u[:, g * C_GROUP_W:(g + 1) * C_GROUP_W], dft)
        uc_ref[:, g * C_GROUP_W:(g + 1) * C_GROUP_W] = z[:, 0:C_GROUP_W].astype(BF16)
        us_ref[:, g * C_GROUP_W:(g + 1) * C_GROUP_W] = z[:, C_GROUP_W:2 * C_GROUP_W].astype(BF16)


def _inproj(x2, mod3, w_qkvu, q_gain, k_gain, cos_t, sin_t, g512, g128, dft_c, *, batch, seq, tm):
    rows = batch * seq
    nl = seq // tm
    tok = lambda w: pl.BlockSpec((tm, w), lambda i: (i, 0))
    full = _resident_spec
    fo = pl.BlockSpec((tm, 512), lambda i: (i % nl, i // nl))
    o_tok = lambda w: jax.ShapeDtypeStruct((rows, w), BF16)
    return pl.pallas_call(
        _inproj_kernel,
        grid=(rows // tm,),
        in_specs=[
            tok(D_MODEL),
            pl.BlockSpec((None, 1, mod3.shape[2]), lambda i: (i // nl, 0, 0)),
            full(w_qkvu), full(q_gain), full(k_gain),
            pl.BlockSpec((tm, LANES), lambda i: (i % nl, 0)),
            pl.BlockSpec((tm, LANES), lambda i: (i % nl, 0)),
            full(g512), full(g128), full(dft_c),
        ],
        out_specs=[tok(512), tok(128), tok(256), tok(512), tok(512), tok(1024), fo, fo],
        out_shape=[o_tok(512), o_tok(128), o_tok(256), o_tok(512), o_tok(512), o_tok(1024),
                   jax.ShapeDtypeStruct((seq, batch * 512), BF16),
                   jax.ShapeDtypeStruct((seq, batch * 512), BF16)],
        compiler_params=_cparams("arbitrary"),
        name="inproj",
    )(x2, mod3, w_qkvu, q_gain, k_gain, cos_t, sin_t, g512, g128, dft_c)


def _attend_t_multi(qps, chunk_lists, shift=None):
    n = len(qps)
    m = [None] * n
    acc = [None] * n
    units = [(i, c) for c in range(len(chunk_lists[0])) for i in range(n)]

    def scores(u):
        i, c = u
        chunk = chunk_lists[i][c]
        s = _dot_nt(chunk[0], qps[i])
        return s if chunk[2] is None or len(chunk) == 4 else s + chunk[2]

    def banded(s, bias_fn, band):
        cols = [slice(b * LANES, (b + 1) * LANES) for b in range(len(band))]
        subs = [s[lo:hi, cs] + bias_fn(lo, hi, cs) for (lo, hi), cs in zip(band, cols)]
        mc = jnp.concatenate([jnp.max(x, axis=0, keepdims=True) for x in subs], axis=1)

        def probs(m_ref):
            out = []
            for x, (lo, hi), cs in zip(subs, band, cols):
                parts = [jnp.exp2(x - m_ref[:, cs]).astype(BF16)]
                if lo:
                    parts.insert(0, jnp.zeros((lo, LANES), BF16))
                if s.shape[0] - hi:
                    parts.append(jnp.zeros((s.shape[0] - hi, LANES), BF16))
                out.append(parts[0] if len(parts) == 1 else jnp.concatenate(parts, axis=0))
            return jnp.concatenate(out, axis=1)

        return mc, probs

    s_next = scores(units[0])
    for idx, (i, c) in enumerate(units):
        s = s_next
        if idx + 1 < len(units):
            s_next = scores(units[idx + 1])
        chunk = chunk_lists[i][c]
        v1 = chunk[1]
        if shift is not None:
            d = _dot_tn(v1, jnp.exp2(s - shift).astype(BF16))
            acc[i] = d if acc[i] is None else acc[i] + d
            continue
        if len(chunk) == 4:
            mc, probs = banded(s, chunk[2], chunk[3])
        else:
            mc, probs = jnp.max(s, axis=0, keepdims=True), lambda m_ref, s=s: jnp.exp2(s - m_ref).astype(BF16)
        if m[i] is None:
            m[i] = mc
            acc[i] = _dot_tn(v1, probs(mc))
        else:
            m_new = jnp.maximum(m[i], mc)
            acc[i] = jnp.exp2(m[i] - m_new) * acc[i] + _dot_tn(v1, probs(m_new))
            m[i] = m_new
    return [a[0:HEAD_DIM, :] / a[HEAD_DIM:HEAD_DIM + 1, :] for a in acc]


def _place_head(q128, src_half, dst_half, lane_hi):
    x = q128.astype(F32)
    if src_half != dst_half:
        x = pltpu.roll(x, 64, 1)
    keep = lane_hi if dst_half == 1 else jnp.logical_not(lane_hi)
    return jnp.where(keep, x, 0.0).astype(BF16)


def _gqa_tile_t(q, chunks_fn, tq, lane_hi, shift=None):
    qps = []
    for g in range(A_KV_HEADS):
        parts = []
        for j in range(A_GROUP):
            hd = A_GROUP * g + j
            parts.append(_place_head(q[:, LANES * (hd // 2):LANES * (hd // 2 + 1)], hd % 2, g, lane_hi))
        qps.append(jnp.concatenate(parts, axis=0))
    outs = _attend_t_multi(qps, [chunks_fn(g) for g in range(A_KV_HEADS)], shift)
    heads_t = [o_t[:, j * tq:(j + 1) * tq] for o_t in outs for j in range(A_GROUP)]
    return jnp.concatenate(heads_t, axis=0).T


def _mha_pair_t(q128, chunks_fn, lane_hi):
    outs = _attend_t_multi([_place_head(q128, hh, hh, lane_hi) for hh in range(2)],
                           [chunks_fn(hh) for hh in range(2)])
    return jnp.concatenate(outs, axis=0).T


def _attn_a_kernel(bound_ref, q_ref, k_ref, v_ref, kc_ref, vc_ref, o_ref, *, tq, chunk_bounds):
    lane_hi = lax.broadcasted_iota(jnp.int32, (tq, LANES), 1) >= 64

    def chunks(g):
        sl = slice(g * LANES, (g + 1) * LANES)
        return ([(kc_ref[...], vc_ref[:, sl], None)]
                + [(k_ref[s:e, :], v_ref[s:e, sl], None) for s, e in chunk_bounds])

    bound = bound_ref[0]
    o_ref[...] = lax.cond(
        bound <= MAX_SHIFT_BOUND,
        lambda: _gqa_tile_t(q_ref[...], chunks, tq, lane_hi, shift=bound),
        lambda: _gqa_tile_t(q_ref[...], chunks, tq, lane_hi),
    ).astype(BF16)


def _attn_a(score_bound, qa, ka, va1, kac, vac1, *, batch, seq, ctx_len, tq, tk):
    bounds = [(i, min(i + tk, seq)) for i in range(0, seq, tk)]
    nq = seq // tq
    return pl.pallas_call(
        functools.partial(_attn_a_kernel, tq=tq, chunk_bounds=bounds),
        grid=(batch, nq),
        in_specs=[pl.BlockSpec(memory_space=pltpu.SMEM),
                  pl.BlockSpec((tq, 512), lambda b, i: (b * nq + i, 0)),
                  pl.BlockSpec((seq, LANES), lambda b, i: (b, 0)),
                  pl.BlockSpec((seq, 2 * LANES), lambda b, i: (b, 0)),
                  pl.BlockSpec((ctx_len, LANES), lambda b, i: (b, 0)),
                  pl.BlockSpec((ctx_len, 2 * LANES), lambda b, i: (b, 0))],
        out_specs=pl.BlockSpec((tq, 512), lambda b, i: (b * nq + i, 0)),
        out_shape=jax.ShapeDtypeStruct((batch * seq, 512), BF16),
        compiler_params=_cparams("arbitrary", "arbitrary"),
        name="attn_a",
    )(score_bound, qa, ka, va1, kac, vac1)


def _attn_b_kernel(q_ref, k_ref, v_ref, kc_ref, vc_ref, bias_ref, o_ref, *, rows_n, tk):
    nq = Q_ROWS * GRID_W
    nw = WIN_ROWS * GRID_W
    n_groups = rows_n // Q_ROWS
    lane_hi = lax.broadcasted_iota(jnp.int32, (nq, LANES), 1) >= 64
    kc = kc_ref[...]

    def key_band(g_pattern, b):
        w0 = int(np.clip(Q_ROWS * g_pattern - (WIN_ROWS - Q_ROWS) // 2, 0, rows_n - WIN_ROWS))
        r0s = [int(np.clip(Q_ROWS * g_pattern + a - NA_ROWS // 2, 0, rows_n - NA_ROWS)) - w0
               for a in range(b * LANES // GRID_W, (b + 1) * LANES // GRID_W)]
        return min(r0s) * GRID_W, (max(r0s) + NA_ROWS) * GRID_W

    def run(groups):
        qps, chunk_lists, q0s = [], [], []
        for g, var, g_pattern in groups:
            if isinstance(g, int):
                q0 = g * nq
                k0 = int(np.clip(Q_ROWS * g - (WIN_ROWS - Q_ROWS) // 2, 0, rows_n - WIN_ROWS)) * GRID_W
            else:
                q0 = pl.multiple_of(g * nq, nq)
                w0 = jnp.clip(Q_ROWS * g - (WIN_ROWS - Q_ROWS) // 2, 0, rows_n - WIN_ROWS)
                k0 = pl.multiple_of(w0 * GRID_W, GRID_W)
            q128 = q_ref[pl.ds(q0, nq), :]
            q0s.append(q0)
            for hh in range(2):
                sl = slice(hh * LANES, (hh + 1) * LANES)
                qps.append(_place_head(q128, hh, hh, lane_hi))
                chunks = []
                for s in range(0, nw, tk):
                    k = k_ref[pl.ds(k0 + s, tk), :]
                    v = v_ref[pl.ds(k0 + s, tk), sl]
                    band = []
                    for b in range(nq // LANES):
                        lo, hi = key_band(g_pattern, b)
                        band.append((max(lo, s) - s, min(hi, s + tk) - s))
                    if all(lo < hi for lo, hi in band):
                        bias_fn = (lambda lo, hi, cs, s=s, hh=hh, var=var:
                                   bias_ref[var, hh, s + lo:s + hi, cs])
                        chunks.append((k, v, bias_fn, band))
                    else:
                        chunks.append((k, v, bias_ref[var, hh, s:s + tk, :]))
                chunk_lists.append(chunks + [(kc, vc_ref[:, sl], None)])
        outs = _attend_t_multi(qps, chunk_lists)
        for u in range(len(groups)):
            o_ref[pl.ds(q0s[u], nq), :] = jnp.concatenate(outs[2 * u:2 * u + 2], axis=0).T.astype(BF16)

    edges = [(0, 0, 0), (n_groups - 1, 2, n_groups - 1)]
    interior = n_groups - 2
    per_trip = max([d for d in range(1, MAX_GROUPS_PER_TRIP + 1) if interior % d == 0], default=0)
    if interior == per_trip:
        run(edges + [(1 + u, 1, 1) for u in range(per_trip)])
    else:
        run(edges)

        def body(t, carry):
            run([(1 + t * per_trip + u, 1, 1) for u in range(per_trip)])
            return carry

        lax.fori_loop(0, interior // per_trip, body, 0)


def _attn_b(qb, kb, vb1, kbc, vbc1, bias_t, *, batch, seq, ctx_len, tk):
    rows_n = seq // GRID_W
    nq = Q_ROWS * GRID_W
    nw = WIN_ROWS * GRID_W
    lat = lambda w: pl.BlockSpec((seq, w), lambda hp, b: (b, hp))
    ctx = lambda w: pl.BlockSpec((ctx_len, w), lambda hp, b: (b, hp))
    return pl.pallas_call(
        functools.partial(_attn_b_kernel, rows_n=rows_n, tk=tk),
        grid=(B_HEADS // 2, batch),
        in_specs=[lat(LANES), lat(LANES), lat(2 * LANES), ctx(LANES), ctx(2 * LANES),
                  pl.BlockSpec((3, 2, nw, nq), lambda hp, b: (0, hp, 0, 0))],
        out_specs=lat(LANES),
        out_shape=jax.ShapeDtypeStruct((batch * seq, 512), BF16),
        compiler_params=_cparams("arbitrary", "arbitrary"),
        name="attn_b",
    )(qb, kb, vb1, kbc, vbc1, bias_t)


def _neighbourhood_bias_t(rpb, rows_n):
    kc = NA_COLS
    cols = np.arange(GRID_W)
    c0 = np.clip(cols - kc // 2, 0, GRID_W - kc)
    col_ok = (cols[None, :] >= c0[:, None]) & (cols[None, :] < c0[:, None] + kc)
    heads, n_dr, n_dc = rpb.shape
    dc = cols[:, None] - cols[None, :] + NA_COLS - 1
    onehot = (np.arange(n_dc)[:, None, None] == dc[None]).astype(np.float32)
    pad = Q_ROWS
    e_max = n_dr - 1 + 2 * pad
    t = jnp.einsum("hdj,jxy->hxdy", rpb[:, ::-1] * LOG2E, jnp.asarray(onehot), precision=lax.Precision.HIGHEST)
    t = jnp.where(col_ok.T[None, :, None, :], t, NEG)
    rev = jnp.pad(t, ((0, 0), (0, 0), (pad, pad), (0, 0))).reshape(heads, GRID_W, (e_max + 1) * GRID_W)
    nq = Q_ROWS * GRID_W
    width = -(-((e_max + 1) * GRID_W + LANES) // LANES) * LANES
    rev = jnp.pad(rev, ((0, 0), (0, 0), (0, width - rev.shape[2])))
    n_groups = rows_n // Q_ROWS
    plans = []
    for g in (0, min(1, n_groups - 1), n_groups - 1):
        w0 = int(np.clip(Q_ROWS * g - (WIN_ROWS - Q_ROWS) // 2, 0, rows_n - WIN_ROWS))
        plan = []
        for i in range(WIN_ROWS):
            d0 = w0 + i - Q_ROWS * g + NA_ROWS - 1
            ok = [a for a in range(Q_ROWS)
                  if int(np.clip(Q_ROWS * g + a - NA_ROWS // 2, 0, rows_n - NA_ROWS)) <= w0 + i
                  < int(np.clip(Q_ROWS * g + a - NA_ROWS // 2, 0, rows_n - NA_ROWS)) + NA_ROWS]
            assert ok == list(range(ok[0], ok[-1] + 1)) if ok else True
            plan.append(((e_max - pad - d0) * GRID_W, (ok[0], ok[-1] + 1) if ok else (0, 0)))
        plans.append(plan)
    return pl.pallas_call(
        functools.partial(_bias_kernel, plans=plans),
        grid=(3, heads),
        in_specs=[pl.BlockSpec((None, GRID_W, width), lambda v, h: (h, 0, 0))],
        out_specs=pl.BlockSpec((None, None, WIN_ROWS * GRID_W, nq), lambda v, h: (v, h, 0, 0)),
        out_shape=jax.ShapeDtypeStruct((3, heads, WIN_ROWS * GRID_W, nq), F32),
        compiler_params=_cparams("arbitrary", "arbitrary"),
        name="nbr_bias",
    )(rev)


def _bias_kernel(rev_ref, o_ref, *, plans):
    nq = o_ref.shape[1]
    q_row = lax.broadcasted_iota(jnp.int32, (GRID_W, nq), 1) // GRID_W
    for v, plan in enumerate(plans):
        @pl.when(pl.program_id(0) == v)
        def _(plan=plan):
            for i, (off, (lo, hi)) in enumerate(plan):
                rows = slice(i * GRID_W, (i + 1) * GRID_W)
                if lo >= hi:
                    o_ref[rows, :] = jnp.full((GRID_W, nq), NEG, F32)
                    continue
                al = off // LANES * LANES
                strip = rev_ref[:, al:al + nq + LANES][:, off - al:off - al + nq]
                o_ref[rows, :] = jnp.where((q_row >= lo) & (q_row < hi), strip, NEG)


def _attn_ctx_kernel(qa_ref, ka_ref, va_ref, qb_ref, kb_ref, vb_ref, oa_ref, ob_ref, *, lc):
    lane_hi = lax.broadcasted_iota(jnp.int32, (lc, LANES), 1) >= 64
    ka = ka_ref[...]
    oa_ref[...] = _gqa_tile_t(
        qa_ref[...], lambda g: [(ka, va_ref[:, g * LANES:(g + 1) * LANES], None)], lc, lane_hi).astype(BF16)
    for hp in range(B_HEADS // 2):
        sl = slice(hp * LANES, (hp + 1) * LANES)
        k = kb_ref[:, sl]
        chunks = lambda hh: [(k, vb_ref[:, (2 * hp + hh) * LANES:(2 * hp + hh + 1) * LANES], None)]
        ob_ref[:, sl] = _mha_pair_t(qb_ref[:, sl], chunks, lane_hi).astype(BF16)


def _attn_ctx(qac, kac, vac1, qbc, kbc, vbc1, *, batch, ctx_len):
    t = lambda w: pl.BlockSpec((ctx_len, w), lambda b: (b, 0))
    o = jax.ShapeDtypeStruct((batch * ctx_len, 512), BF16)
    return pl.pallas_call(
        functools.partial(_attn_ctx_kernel, lc=ctx_len),
        grid=(batch,),
        in_specs=[t(512), t(128), t(256), t(512), t(512), t(1024)],
        out_specs=[t(512), t(512)],
        out_shape=[o, o],
        compiler_params=_cparams("arbitrary"),
        name="attn_ctx",
    )(qac, kac, vac1, qbc, kbc, vbc1)


def _dft_kernel(ac_ref, as_ref, bc_ref, bs_ref, uc_ref, us_ref, o_ref, ct_ref, st_ref, *, bm):
    i = pl.program_id(0)

    n = ct_ref.shape[1]
    slab = min(n, DFT_K_SLAB)

    @pl.when(pl.program_id(1) == 0)
    def _():
        acc = None
        for s in range(0, n, slab):
            ks = slice(s, s + slab)
            bc = bc_ref[:, ks]
            bs = bs_ref[:, ks]
            for r in range(bm // GRID_W):
                a_c = ac_ref[pl.ds(i * (bm // GRID_W) + r, 1), ks]
                a_s = as_ref[pl.ds(i * (bm // GRID_W) + r, 1), ks]
                ct_ref[r * GRID_W:(r + 1) * GRID_W, ks] = (a_c * bc - a_s * bs).astype(BF16)
                st_ref[r * GRID_W:(r + 1) * GRID_W, ks] = (a_s * bc + a_c * bs).astype(BF16)
            d = _dot(ct_ref[:, ks], uc_ref[ks, :]) - _dot(st_ref[:, ks], us_ref[ks, :])
            acc = d if acc is None else acc + d
        o_ref[...] = acc.astype(BF16)

    @pl.when(pl.program_id(1) != 0)
    def _():
        o_ref[...] = (_dot(ct_ref[...], uc_ref[...]) - _dot(st_ref[...], us_ref[...])).astype(BF16)


def _dft(tabs, uc, us):
    n, width = uc.shape
    bm = min(n, 1024)
    bn = min(width, 512)
    u_spec = pl.BlockSpec((n, bn), lambda i, j: (0, j))
    return pl.pallas_call(
        functools.partial(_dft_kernel, bm=bm),
        grid=(n // bm, width // bn),
        in_specs=[_resident_spec(t) for t in tabs] + [u_spec, u_spec],
        out_specs=pl.BlockSpec((bm, bn), lambda i, j: (i, j)),
        out_shape=jax.ShapeDtypeStruct((n, width), BF16),
        scratch_shapes=[pltpu.VMEM((bm, n), BF16), pltpu.VMEM((bm, n), BF16)],
        compiler_params=_cparams("arbitrary", "arbitrary"),
        name="dft",
    )(*tabs, uc, us)


def _dft_factor_tables(n):
    r = n // GRID_W
    k = np.arange(n, dtype=np.int64)
    ang_a = (2.0 * np.pi / r) * ((np.arange(r)[:, None] * k[None, :]) % r)
    ang_b = (2.0 * np.pi / n) * ((np.arange(GRID_W)[:, None] * k[None, :]) % n)
    pad = ((0, (-r) % 8), (0, 0))
    scale = n ** -0.5
    return tuple(jnp.asarray(t, dtype=F32) for t in (
        np.pad(np.cos(ang_a), pad), np.pad(np.sin(ang_a), pad), np.cos(ang_b) * scale, np.sin(ang_b) * scale))


def _merge_kernel(x_ref, mod_ref, oa_ref, ob_ref, oc_ref, wg_ref, wb_ref, wo_ref, g_ref, b_ref, o_ref, *, alpha):
    mod = mod_ref[...]
    sh = mod[:, 0:D_MODEL]
    sc = mod[:, D_MODEL:2 * D_MODEL]
    gate = mod[:, 2 * D_MODEL:3 * D_MODEL]
    blocks = _row_blocks(x_ref.shape[0])
    xs = [x_ref[b, :] for b in blocks]
    hs = [(_layernorm(x) * (1.0 + sc) + sh).astype(BF16) for x in xs]
    ys = [None] * len(blocks)
    for i, br_ref in enumerate((oa_ref, ob_ref, oc_ref)):
        for r, b in enumerate(blocks):
            gz = _dot(hs[r], wg_ref[:, i * D_MODEL:(i + 1) * D_MODEL])
            t = _dot(br_ref[b, :], wb_ref[i]) * (1.0 / (1.0 + jnp.exp(-gz)))
            ys[r] = t if ys[r] is None else ys[r] + t
    for r, b in enumerate(blocks):
        yo = _dot(ys[r].astype(BF16), wo_ref[...])
        o_ref[b, :] = _layernorm(alpha * xs[r] + gate * yo) * g_ref[...] + b_ref[...]


def _merge(x2, mod3, oa, ob, oc, w_gate, w_branch, w_out, ln_g, ln_b, *, batch, seq, tm, alpha):
    rows = batch * seq
    nl = seq // tm
    tok = lambda w: pl.BlockSpec((tm, w), lambda i: (i, 0))
    full = _resident_spec
    return pl.pallas_call(
        functools.partial(_merge_kernel, alpha=alpha),
        grid=(rows // tm,),
        in_specs=[
            tok(D_MODEL),
            pl.BlockSpec((None, 1, mod3.shape[2]), lambda i: (i // nl, 0, 0)),
            tok(512), tok(512),
            pl.BlockSpec((tm, 512), lambda i: (i % nl, i // nl)),
            full(w_gate), full(w_branch), full(w_out), full(ln_g), full(ln_b),
        ],
        out_specs=tok(D_MODEL),
        out_shape=jax.ShapeDtypeStruct((rows, D_MODEL), F32),
        compiler_params=_cparams("arbitrary"),
        name="merge",
    )(x2, mod3, oa, ob, oc, w_gate, w_branch, w_out, ln_g, ln_b)


FF_CHUNKS = ((0, 768), (768, 1536), (1536, 2304), (2304, 2816))


def _ffn_kernel(x_ref, mod_ref, wgu_ref, wd_ref, g_ref, b_ref, o_ref, *, alpha):
    mod = mod_ref[...]
    sh = mod[:, 3 * D_MODEL:4 * D_MODEL]
    sc = mod[:, 4 * D_MODEL:5 * D_MODEL]
    gate = mod[:, 5 * D_MODEL:6 * D_MODEL]
    blocks = _row_blocks(x_ref.shape[0])
    xs = [x_ref[b, :] for b in blocks]
    hs = [(_layernorm(x) * (1.0 + sc) + sh).astype(BF16) for x in xs]
    fs = [None] * len(blocks)
    for s, e in FF_CHUNKS:
        for i in range(len(blocks)):
            g = _dot(hs[i], wgu_ref[:, s:e])
            u = _dot(hs[i], wgu_ref[:, D_FF + s:D_FF + e])
            a = (g * (1.0 / (1.0 + jnp.exp(-g))) * u).astype(BF16)
            t = _dot(a, wd_ref[s:e, :])
            fs[i] = t if fs[i] is None else fs[i] + t
    for i, b in enumerate(blocks):
        o_ref[b, :] = _layernorm(alpha * xs[i] + gate * fs[i]) * g_ref[...] + b_ref[...]


def _ffn(x2, mod3, w_gu, w_down, ln_g, ln_b, *, batch, seq, tm, alpha):
    rows = batch * seq
    nl = seq // tm
    tok = pl.BlockSpec((tm, D_MODEL), lambda i: (i, 0))
    full = _resident_spec
    return pl.pallas_call(
        functools.partial(_ffn_kernel, alpha=alpha),
        grid=(rows // tm,),
        in_specs=[tok, pl.BlockSpec((None, 1, mod3.shape[2]), lambda i: (i // nl, 0, 0)),
                  full(w_gu), full(w_down), full(ln_g), full(ln_b)],
        out_specs=tok,
        out_shape=jax.ShapeDtypeStruct((rows, D_MODEL), F32),
        compiler_params=_cparams("arbitrary"),
        name="ffn",
    )(x2, mod3, w_gu, w_down, ln_g, ln_b)


def _rope_tables(seq):
    quarter = HEAD_DIM // 4
    pos = np.arange(seq)
    freqs = ROPE_THETA ** (-np.arange(quarter, dtype=np.float64) / quarter)
    ar = (pos // GRID_W)[:, None] * freqs
    ac = (pos % GRID_W)[:, None] * freqs
    cos64 = np.concatenate([np.cos(ar), np.cos(ar), np.cos(ac), np.cos(ac)], axis=1)
    sin64 = np.concatenate([-np.sin(ar), np.sin(ar), -np.sin(ac), np.sin(ac)], axis=1)
    return jnp.asarray(np.tile(cos64, (1, 2)), dtype=F32), jnp.asarray(np.tile(sin64, (1, 2)), dtype=F32)


def _head_mean_matrix(width):
    idx = np.arange(width) // HEAD_DIM
    return jnp.asarray((idx[:, None] == idx[None, :]).astype(np.float32) / HEAD_DIM).astype(BF16)


def _channel_dft_matrix():
    idx = np.arange(C_GROUP_W)
    ang = (2.0 * np.pi / C_GROUP_W) * ((idx[:, None] * idx[None, :]) % C_GROUP_W)
    scale = C_GROUP_W ** -0.5
    mat = np.concatenate([np.cos(ang), np.sin(ang)], axis=1) * scale
    return jnp.asarray(mat, dtype=F32).astype(BF16)


def kernel(x, c, ctx, c_ctx, w_ada, b_ada, w_in, q_norm, k_norm, rpb, w_branch, w_out,
           ln1_g, ln1_b, w_gu, w_down, ln2_g, ln2_b):
    batch, seq, d = x.shape
    ctx_len = ctx.shape[1]
    depth = w_ada.shape[0]
    n_groups = seq // (Q_ROWS * GRID_W)
    assert d == D_MODEL and seq % (Q_ROWS * GRID_W) == 0 and seq // GRID_W >= WIN_ROWS
    assert n_groups >= 2
    assert ctx_len % LANES == 0
    alpha = (2.0 * depth) ** 0.25
    tm = 1024
    tmc = min(ctx_len, 1024)
    assert seq % tm == 0 and tm % ROW_BLOCK == 0 and tmc % ROW_BLOCK == 0

    pad = (-(batch + 1)) % 8
    cc = jnp.concatenate([c, c_ctx[None, :], jnp.zeros((pad, d), F32)], axis=0)
    mods = _ada(cc, w_ada, b_ada)

    cos_t, sin_t = _rope_tables(seq)
    cos_c = jnp.ones((ctx_len, LANES), F32)
    sin_c = jnp.zeros((ctx_len, LANES), F32)
    g512 = _head_mean_matrix(512)
    g128 = _head_mean_matrix(128)
    dft_c = _channel_dft_matrix()
    tabs_l = _dft_factor_tables(seq)
    tabs_c = _dft_factor_tables(ctx_len)

    xl = x.reshape(batch * seq, d)
    xc = ctx.reshape(batch * ctx_len, d)
    for l in range(depth):
        with_ctx = l < depth - 1
        w_qkvu = w_in[l, :, :N_QKVU].astype(BF16)
        w_gate = w_in[l, :, N_QKVU:].astype(BF16)
        wb = w_branch[l].astype(BF16)
        wo = w_out[l].astype(BF16)
        wgu = w_gu[l].astype(BF16)
        wd = w_down[l].astype(BF16)
        q_gain = jnp.tile(q_norm[l] * Q_SCALE, A_HEADS)[None, :]
        k_gain = jnp.tile(k_norm[l], A_KV_HEADS)[None, :]
        mod_l = mods[l, :batch][:, None, :]
        mod_c = jnp.broadcast_to(mods[l, batch][None, None, :], (batch, 1, 6 * d))
        ln1 = (ln1_g[l][None, :], ln1_b[l][None, :])
        ln2 = (ln2_g[l][None, :], ln2_b[l][None, :])

        qa, ka, va1, qb, kb, vb1, uc, us = _inproj(
            xl, mod_l, w_qkvu, q_gain, k_gain, cos_t, sin_t, g512, g128, dft_c, batch=batch, seq=seq, tm=tm)
        qac, kac, vac1, qbc, kbc, vbc1, ucc, usc = _inproj(
            xc, mod_c, w_qkvu, q_gain, k_gain, cos_c, sin_c, g512, g128, dft_c, batch=batch, seq=ctx_len, tm=tmc)

        score_bound = (1.01 * HEAD_DIM * jnp.max(jnp.abs(q_gain)) * jnp.max(jnp.abs(k_gain))).reshape(1)
        oa = _attn_a(score_bound, qa, ka, va1, kac, vac1, batch=batch, seq=seq, ctx_len=ctx_len, tq=256, tk=256)
        bias_t = _neighbourhood_bias_t(rpb[l], seq // GRID_W)
        ob = _attn_b(qb, kb, vb1, kbc, vbc1, bias_t, batch=batch, seq=seq, ctx_len=ctx_len, tk=512)
        oc = _dft(tabs_l, uc, us)

        x1 = _merge(xl, mod_l, oa, ob, oc, w_gate, wb, wo, *ln1, batch=batch, seq=seq, tm=tm, alpha=alpha)
        xl = _ffn(x1, mod_l, wgu, wd, *ln2, batch=batch, seq=seq, tm=tm, alpha=alpha)

        if with_ctx:
            oac, obc = _attn_ctx(qac, kac, vac1, qbc, kbc, vbc1, batch=batch, ctx_len=ctx_len)
            occ = _dft(tabs_c, ucc, usc)
            xc1 = _merge(xc, mod_c, oac, obc, occ, w_gate, wb, wo, *ln1,
                         batch=batch, seq=ctx_len, tm=tmc, alpha=alpha)
            xc = _ffn(xc1, mod_c, wgu, wd, *ln2, batch=batch, seq=ctx_len, tm=tmc, alpha=alpha)
    return xl.reshape(batch, seq, d)
```
